```python
import jax, jax.numpy as jnp
from jax import lax
import numpy as np

D_MODEL = 2048
BATCH = 4
SEQ = 2048
DEPTH = 4

N_MIXERS = 3
HEAD_DIM = 128
DIL_GROUPS = ((128, 1), (512, 4), (2048, 16))
N_DIL = len(DIL_GROUPS)
HEADS_PER_GROUP = D_MODEL // HEAD_DIM
ATTN_OUT = HEADS_PER_GROUP * HEAD_DIM
ATTN_QKV = N_DIL * ATTN_OUT
ROPE_DIM = HEAD_DIM // 4
ROPE_THETA = 500000.0
BLOCK = 128
EXPAND = 2
SGU_WIDTH = EXPAND * D_MODEL
SGU_CHUNK = 128
SGU_GROUPS = 16
CONV_WIDTH = EXPAND * D_MODEL
CONV_K = 31
NORM_EPS = 1e-6
N_LAYERS_A = (DEPTH + 2) // 3
N_LAYERS_B = (DEPTH + 1) // 3
N_LAYERS_C = DEPTH // 3

kernel_name = "hybrid_dilated_sgu_conformer_trunk"


def _rms(x):
    xf = x.astype(jnp.float32)
    return xf * lax.rsqrt(jnp.mean(xf * xf, axis=-1, keepdims=True) + NORM_EPS)


def _layernorm(x, g, b):
    xf = x.astype(jnp.float32)
    mu = jnp.mean(xf, axis=-1, keepdims=True)
    xc = xf - mu
    var = jnp.mean(xc * xc, axis=-1, keepdims=True)
    return (xc * lax.rsqrt(var + NORM_EPS) * g + b).astype(x.dtype)


def _rope(x, cos, sin):
    half = ROPE_DIM // 2
    x1 = x[..., :half]
    x2 = x[..., half:ROPE_DIM]
    return jnp.concatenate([x1 * cos - x2 * sin, x2 * cos + x1 * sin, x[..., ROPE_DIM:]], axis=-1)


def _banded_attention(q, k, v, n_back):
    L = q.shape[-2]
    nb = -(-L // BLOCK)
    pad = nb * BLOCK - L
    padcfg = [(0, 0)] * (q.ndim - 2) + [(0, pad), (0, 0)]
    q, k, v = (jnp.pad(t, padcfg) for t in (q, k, v))
    lead = q.shape[:-2]
    qb = q.reshape(*lead, nb, BLOCK, HEAD_DIM)

    def with_prev(t):
        tb = t.reshape(*lead, nb, BLOCK, HEAD_DIM)
        prev = jnp.concatenate([jnp.zeros_like(tb[..., :1, :, :]), tb[..., :-1, :, :]], axis=-3)
        return jnp.concatenate([prev, tb], axis=-2)

    kk, vv = with_prev(k), with_prev(v)
    s = jnp.einsum('...nqd,...nkd->...nqk', qb, kk,
                   preferred_element_type=jnp.float32) * (HEAD_DIM ** -0.5)
    qi = jnp.arange(BLOCK)[:, None] + BLOCK
    ki = jnp.arange(2 * BLOCK)[None, :]
    dist = qi - ki
    band = (dist >= 0) & (dist <= n_back)
    first = (jnp.arange(nb) == 0)[:, None, None] & (ki < BLOCK)[None]
    valid = band[None] & ~first
    s = jnp.where(valid, s, -jnp.inf)
    lse = jax.nn.logsumexp(s, axis=-1)
    p = jnp.exp(s - lse[..., None])
    o = jnp.einsum('...nqk,...nkd->...nqd', p.astype(v.dtype), vv)
    o = o.reshape(*lead, nb * BLOCK, HEAD_DIM)[..., :L, :]
    lse = lse.reshape(*lead, nb * BLOCK)[..., :L]
    return o, lse


def _dilated_attention_mixer(h, w_in, q_gain, k_gain, w_out, cos, sin):
    B, S, _ = h.shape
    proj = h @ w_in
    shp = (B, S, N_DIL, HEADS_PER_GROUP, HEAD_DIM)
    q = proj[..., :ATTN_QKV].reshape(shp)
    k = proj[..., ATTN_QKV:2 * ATTN_QKV].reshape(shp)
    v = proj[..., 2 * ATTN_QKV:3 * ATTN_QKV].reshape(shp)
    z = proj[..., 3 * ATTN_QKV:]
    q = _rope(_rms(q) * q_gain, cos, sin).astype(h.dtype)
    k = _rope(_rms(k) * k_gain, cos, sin).astype(h.dtype)
    outs, lses = [], []
    for g, (window, dil) in enumerate(DIL_GROUPS):
        L = S // dil

        def to_residue(t):
            return t[:, :, g].reshape(B, L, dil, HEADS_PER_GROUP, HEAD_DIM).transpose(0, 2, 3, 1, 4)

        o, lse = _banded_attention(to_residue(q), to_residue(k), to_residue(v), window // dil)
        outs.append(o.transpose(0, 3, 1, 2, 4).reshape(B, S, HEADS_PER_GROUP, HEAD_DIM))
        lses.append(lse.transpose(0, 3, 1, 2).reshape(B, S, HEADS_PER_GROUP))
    wts = jax.nn.softmax(jnp.stack(lses, axis=0), axis=0)
    o = jnp.sum(wts[..., None] * jnp.stack(outs, axis=0).astype(jnp.float32), axis=0)
    y = o.reshape(B, S, ATTN_OUT).astype(h.dtype) * jax.nn.silu(z)
    return y @ w_out


def _sgu_mixer(h, w_in, ln_g, ln_b, ws, bs, w_out):
    B, S, _ = h.shape
    E = SGU_WIDTH
    proj = h @ w_in
    u = jax.nn.gelu(proj[..., :E])
    v = _layernorm(jax.nn.gelu(proj[..., E:2 * E]), ln_g, ln_b)
    z = proj[..., 2 * E:]
    nc = S // SGU_CHUNK
    vc = v.reshape(B, nc, SGU_CHUNK, SGU_GROUPS, E // SGU_GROUPS)
    causal = jnp.tril(jnp.ones((SGU_CHUNK, SGU_CHUNK), dtype=bool))
    wm = jnp.where(causal, ws, jnp.zeros_like(ws))
    sv = jnp.einsum('gts,bnsgc->bntgc', wm, vc) + bs.T[:, :, None]
    y = u * sv.reshape(B, S, E) * jax.nn.silu(z)
    return y @ w_out


def _conv_mixer(h, w_in, dw_w, dw_b, ln_g, ln_b, w_out):
    E = CONV_WIDTH
    proj = h @ w_in
    g = proj[..., :E] * jax.nn.sigmoid(proj[..., E:2 * E])
    z = proj[..., 2 * E:]
    g = lax.conv_general_dilated(g, dw_w[:, None, :], window_strides=(1,),
                                 padding=((CONV_K - 1, 0),),
                                 dimension_numbers=('NWC', 'WIO', 'NWC'),
                                 feature_group_count=E) + dw_b
    g = jax.nn.silu(_layernorm(g, ln_g, ln_b))
    return (g * jax.nn.silu(z)) @ w_out


def _normal(k, shape, scale):
    return jax.random.normal(k, shape, jnp.float32) * scale


def setup_inputs(seed: int = 0) -> dict:
    key = jax.random.key(seed)
    ks = jax.random.split(key, 24)
    D = D_MODEL
    x = _normal(ks[0], (BATCH, SEQ, D), 1.0)
    c = _normal(ks[1], (BATCH, D), 1.0)
    offs = jax.random.randint(ks[2], (BATCH, 1), 0, 4096, dtype=jnp.int32)
    positions = offs + jnp.arange(SEQ, dtype=jnp.int32)[None, :]
    ada_w = _normal(ks[3], (DEPTH, D, 3 * D), D ** -0.5)
    ada_b = _normal(ks[4], (DEPTH, 3 * D), 0.01)
    norm_g = 1.0 + _normal(ks[5], (DEPTH, D), 0.02)
    attn_w_in = _normal(ks[6], (N_LAYERS_A, D, 3 * ATTN_QKV + ATTN_OUT), D ** -0.5)
    attn_q_gain = 1.0 + _normal(ks[7], (N_LAYERS_A, HEAD_DIM), 0.02)
    attn_k_gain = 1.0 + _normal(ks[8], (N_LAYERS_A, HEAD_DIM), 0.02)
    attn_w_out = _normal(ks[9], (N_LAYERS_A, ATTN_OUT, D), ATTN_OUT ** -0.5)
    sgu_w_in = _normal(ks[10], (N_LAYERS_B, D, 3 * SGU_WIDTH), D ** -0.5)
    sgu_ln_g = 1.0 + _normal(ks[11], (N_LAYERS_B, SGU_WIDTH), 0.02)
    sgu_ln_b = _normal(ks[12], (N_LAYERS_B, SGU_WIDTH), 0.02)
    sgu_ws = _normal(ks[13], (N_LAYERS_B, SGU_GROUPS, SGU_CHUNK, SGU_CHUNK), SGU_CHUNK ** -0.5)
    sgu_bs = 1.0 + _normal(ks[14], (N_LAYERS_B, SGU_GROUPS, SGU_CHUNK), 0.1)
    sgu_w_out = _normal(ks[15], (N_LAYERS_B, SGU_WIDTH, D), SGU_WIDTH ** -0.5)
    conv_w_in = _normal(ks[16], (N_LAYERS_C, D, 3 * CONV_WIDTH), D ** -0.5)
    conv_dw_w = _normal(ks[17], (N_LAYERS_C, CONV_K, CONV_WIDTH), CONV_K ** -0.5)
    conv_dw_b = _normal(ks[18], (N_LAYERS_C, CONV_WIDTH), 0.02)
    conv_ln_g = 1.0 + _normal(ks[19], (N_LAYERS_C, CONV_WIDTH), 0.02)
    conv_ln_b = _normal(ks[20], (N_LAYERS_C, CONV_WIDTH), 0.02)
    conv_w_out = _normal(ks[21], (N_LAYERS_C, CONV_WIDTH, D), CONV_WIDTH ** -0.5)
    return {"x": x, "c": c, "positions": positions,
            "ada_w": ada_w, "ada_b": ada_b, "norm_g": norm_g,
            "attn_w_in": attn_w_in, "attn_q_gain": attn_q_gain, "attn_k_gain": attn_k_gain,
            "attn_w_out": attn_w_out,
            "sgu_w_in": sgu_w_in, "sgu_ln_g": sgu_ln_g, "sgu_ln_b": sgu_ln_b,
            "sgu_ws": sgu_ws, "sgu_bs": sgu_bs, "sgu_w_out": sgu_w_out,
            "conv_w_in": conv_w_in, "conv_dw_w": conv_dw_w, "conv_dw_b": conv_dw_b,
            "conv_ln_g": conv_ln_g, "conv_ln_b": conv_ln_b, "conv_w_out": conv_w_out}


def reference(x, c, positions, ada_w, ada_b, norm_g,
              attn_w_in, attn_q_gain, attn_k_gain, attn_w_out,
              sgu_w_in, sgu_ln_g, sgu_ln_b, sgu_ws, sgu_bs, sgu_w_out,
              conv_w_in, conv_dw_w, conv_dw_b, conv_ln_g, conv_ln_b, conv_w_out):
    inv_freq = jnp.power(ROPE_THETA, -jnp.arange(0, ROPE_DIM, 2, dtype=jnp.float32) / ROPE_DIM)
    ang = positions.astype(jnp.float32)[..., None] * inv_freq
    cos = jnp.cos(ang)[:, :, None, None, :]
    sin = jnp.sin(ang)[:, :, None, None, :]
    c_act = jax.nn.silu(c)
    for i in range(DEPTH):
        mod = c_act @ ada_w[i] + ada_b[i]
        shift, scale, gate = jnp.split(mod, 3, axis=-1)
        h = (_rms(x) * norm_g[i] * (1.0 + scale[:, None, :]) + shift[:, None, :]).astype(x.dtype)
        kind, j = i % N_MIXERS, i // N_MIXERS
        if kind == 0:
            y = _dilated_attention_mixer(h, attn_w_in[j], attn_q_gain[j], attn_k_gain[j],
                                         attn_w_out[j], cos, sin)
        elif kind == 1:
            y = _sgu_mixer(h, sgu_w_in[j], sgu_ln_g[j], sgu_ln_b[j], sgu_ws[j], sgu_bs[j],
                           sgu_w_out[j])
        else:
            y = _conv_mixer(h, conv_w_in[j], conv_dw_w[j], conv_dw_b[j], conv_ln_g[j],
                            conv_ln_b[j], conv_w_out[j])
        x = x + gate[:, None, :] * y
    return x
```

```python
import functools

import jax
import jax.numpy as jnp
from jax import lax
from jax.experimental import pallas as pl
from jax.experimental.pallas import tpu as pltpu

D_MODEL = 2048
DEPTH = 4
N_MIXERS = 3
HEAD_DIM = 128
DIL_GROUPS = ((128, 1), (512, 4), (2048, 16))
N_DIL = len(DIL_GROUPS)
HEADS_PER_GROUP = D_MODEL // HEAD_DIM
ATTN_OUT = HEADS_PER_GROUP * HEAD_DIM
ATTN_QKV = N_DIL * ATTN_OUT
ROPE_DIM = HEAD_DIM // 4
ROPE_THETA = 500000.0
BLOCK = 128
SGU_WIDTH = 2 * D_MODEL
SGU_CHUNK = 128
SGU_GROUPS = 16
CONV_WIDTH = 2 * D_MODEL
CONV_K = 31
NORM_EPS = 1e-6

LANES = 128
HALO = 32
VMEM_LIMIT = 56 * 1024 * 1024
MASKED = -1e30

F32 = jnp.float32
BF16 = jnp.bfloat16


def _params(n_axes):
    return pltpu.CompilerParams(dimension_semantics=("arbitrary",) * n_axes,
                                vmem_limit_bytes=VMEM_LIMIT)


def _sigmoid(x):
    return 1.0 / (1.0 + jnp.exp(-x))


def _silu(x):
    return x * _sigmoid(x)


def _gelu_tanh(x):
    return x * (0.5 * (1.0 + jnp.tanh(0.7978845608028654 * (x + 0.044715 * (x * x * x)))))


def _ada_kernel(c_ref, w_ref, b_ref, o_ref):
    c = c_ref[...]
    o_ref[0] = jnp.dot(_silu(c), w_ref[0], preferred_element_type=F32,
                       precision=lax.Precision.HIGHEST) + b_ref[0]


def _ada_modulation(c, ada_w, ada_b):
    B, D = c.shape
    rows = 8
    tn = 1024
    c8 = jnp.pad(c, ((0, rows - B), (0, 0)))
    out = pl.pallas_call(
        _ada_kernel,
        out_shape=jax.ShapeDtypeStruct((DEPTH, rows, 3 * D), F32),
        grid=(DEPTH, 3 * D // tn),
        in_specs=[pl.BlockSpec((rows, D), lambda l, j: (0, 0)),
                  pl.BlockSpec((1, D, tn), lambda l, j: (l, 0, j)),
                  pl.BlockSpec((1, 1, tn), lambda l, j: (l, 0, j))],
        out_specs=pl.BlockSpec((1, rows, tn), lambda l, j: (l, 0, j)),
        compiler_params=_params(2),
        name="ada_modulation",
    )(c8, ada_w, ada_b.reshape(DEPTH, 1, 3 * D))
    return out[:, :B]


def _norm_prologue(x_ref, sc_ref, sh_ref, g_ref, h_ref, slab_ref, *, d, tm):
    rc = 128
    a = g_ref[...] * (1.0 + sc_ref[0])
    sh = sh_ref[0]
    n_slabs = D_MODEL // LANES

    def body(i, carry):
        r0 = pl.multiple_of(i * rc, rc)
        x = x_ref[0, pl.ds(r0, rc), :]
        ms = jnp.mean(x * x, axis=-1, keepdims=True)
        h = x * lax.rsqrt(ms + NORM_EPS) * a + sh
        if d == 1:
            h_ref[pl.ds(r0, rc), :] = h.astype(BF16)
        else:
            for c in range(n_slabs):
                slab_ref[c, pl.ds(r0, rc), :] = h[:, c * LANES:(c + 1) * LANES]
        return carry

    lax.fori_loop(0, tm // rc, body, 0)
    if d > 1:
        rows = tm // d
        for r in range(d):
            for c in range(n_slabs):
                h_ref[r * rows:(r + 1) * rows, c * LANES:(c + 1) * LANES] = (
                    slab_ref[c, pl.ds(r, rows, stride=d), :].astype(BF16))


def _attn_proj_kernel(x_ref, sc_ref, sh_ref, g_ref, w_ref, gain_ref, cos_ref, sa_ref, sb_ref,
                      o_ref, h_ref, acc_ref, *slab, d, tm, tn):
    j = pl.program_id(1)

    @pl.when(j == 0)
    def _():
        _norm_prologue(x_ref, sc_ref, sh_ref, g_ref, h_ref, slab[0] if slab else None, d=d, tm=tm)

    acc_ref[...] = jnp.dot(h_ref[...], w_ref[...], preferred_element_type=F32)

    rows = tm // d
    rc = min(128, rows)
    per_res = rows // rc
    sec = j // (ATTN_OUT // tn)

    def chunk_index(ci):
        r = ci // per_res
        m0 = pl.multiple_of((ci % per_res) * rc, rc)
        p0 = pl.multiple_of(ci * rc, rc)
        return r, m0, p0

    @pl.when(sec < 2)
    def _():
        gain = gain_ref[pl.ds(sec, 1), :]

        def body(ci, carry):
            r, m0, p0 = chunk_index(ci)
            cos = cos_ref[0, r, pl.ds(m0, rc), :]
            sa = sa_ref[0, r, pl.ds(m0, rc), :]
            sb = sb_ref[0, r, pl.ds(m0, rc), :]
            for hc in range(tn // LANES):
                a = acc_ref[pl.ds(p0, rc), hc * LANES:(hc + 1) * LANES]
                ms = jnp.mean(a * a, axis=-1, keepdims=True)
                an = a * lax.rsqrt(ms + NORM_EPS) * gain
                rot = (an * cos + pltpu.roll(an, LANES - ROPE_DIM // 2, 1) * sa
                       + pltpu.roll(an, ROPE_DIM // 2, 1) * sb)
                o_ref[0, r, pl.ds(m0, rc), hc * LANES:(hc + 1) * LANES] = rot.astype(BF16)
            return carry

        lax.fori_loop(0, tm // rc, body, 0)

    @pl.when(sec >= 2)
    def _():
        def body(ci, carry):
            r, m0, p0 = chunk_index(ci)
            o_ref[0, r, pl.ds(m0, rc), :] = acc_ref[pl.ds(p0, rc), :].astype(BF16)
            return carry

        lax.fori_loop(0, tm // rc, body, 0)


def _attn_proj(x, scale, shift, norm_g, w_bf16, gains, tables, *, group, with_gate, tm=512, tn=1024):
    B, S, D = x.shape
    d = DIL_GROUPS[group][1]
    n_sec = 4 if with_gate else 3
    per_sec = ATTN_OUT // tn
    tiles_m = S // tm

    def w_index(i, j):
        sec = j // per_sec
        base = jnp.where(sec < 3, sec * (ATTN_QKV // tn) + group * per_sec, 3 * (ATTN_QKV // tn))
        return (0, base + j % per_sec)

    tab_spec = pl.BlockSpec((1, d, tm // d, LANES), lambda i, j: (i // tiles_m, 0, i % tiles_m, 0))
    scratch = [pltpu.VMEM((tm, D), BF16), pltpu.VMEM((tm, tn), F32)]
    if d > 1:
        scratch.append(pltpu.VMEM((D // LANES, tm, LANES), F32))
    return pl.pallas_call(
        functools.partial(_attn_proj_kernel, d=d, tm=tm, tn=tn),
        out_shape=jax.ShapeDtypeStruct((B, d, S // d, n_sec * ATTN_OUT), BF16),
        grid=(B * tiles_m, n_sec * per_sec),
        in_specs=[pl.BlockSpec((1, tm, D), lambda i, j: (i // tiles_m, i % tiles_m, 0)),
                  pl.BlockSpec((1, 1, D), lambda i, j: (i // tiles_m, 0, 0)),
                  pl.BlockSpec((1, 1, D), lambda i, j: (i // tiles_m, 0, 0)),
                  pl.BlockSpec((1, D), lambda i, j: (0, 0)),
                  pl.BlockSpec((D, tn), w_index),
                  pl.BlockSpec((2, LANES), lambda i, j: (0, 0)),
                  tab_spec, tab_spec, tab_spec],
        out_specs=pl.BlockSpec((1, d, tm // d, tn), lambda i, j: (i // tiles_m, 0, i % tiles_m, j)),
        scratch_shapes=scratch,
        compiler_params=_params(2),
        name=f"attn_proj_g{group}",
    )(x, scale, shift, norm_g, w_bf16, gains, *tables)


def _attn_block(q, k, v, bias):
    s = lax.dot_general(q, k, (((1,), (1,)), ((), ())), preferred_element_type=F32) + bias
    m = jnp.max(s, axis=-1, keepdims=True)
    p = jnp.exp(s - m)
    l = jnp.sum(p, axis=-1, keepdims=True)
    o = jnp.dot(p.astype(BF16), v, preferred_element_type=F32) * (1.0 / l)
    return o, m + jnp.log(l)


def _attn_kernel(q0, k0, v0, z0, q1, k1, v1, q2, k2, v2, y_ref, o_scr, lse_scr, *, seq):
    row = lax.broadcasted_iota(jnp.int32, (BLOCK, 2 * BLOCK), 0)
    col = lax.broadcasted_iota(jnp.int32, (BLOCK, 2 * BLOCK), 1)
    dist = row + BLOCK - col
    bias_both = jnp.where(dist >= 0, jnp.where(dist <= BLOCK, 0.0, MASKED), MASKED).astype(F32)
    bias_first = bias_both[:, BLOCK:]

    groups = ((q0, k0, v0), (q1, k1, v1), (q2, k2, v2))
    for g, (q_ref, k_ref, v_ref) in enumerate(groups):
        d = DIL_GROUPS[g][1]
        nb = seq // d // BLOCK

        def store(r, n, o, lse, g=g, d=d):
            lse_b = jnp.broadcast_to(lse, (BLOCK, LANES))
            start = r + d * BLOCK * n
            if d == 1:
                rows = pl.ds(start if isinstance(start, int) else pl.multiple_of(start, BLOCK), BLOCK)
            else:
                rows = pl.ds(start, BLOCK, stride=d)
            o_scr[g, rows, :] = o
            lse_scr[g, rows, :] = lse_b

        def residue(r, carry, q_ref=q_ref, k_ref=k_ref, v_ref=v_ref, nb=nb, store=store):
            o, lse = _attn_block(q_ref[0, r, 0:BLOCK, :], k_ref[0, r, 0:BLOCK, :],
                                 v_ref[0, r, 0:BLOCK, :], bias_first)
            store(r, 0, o, lse)

            def block(n, c2):
                q = q_ref[0, r, pl.ds(pl.multiple_of(n * BLOCK, BLOCK), BLOCK), :]
                kv_rows = pl.ds(pl.multiple_of((n - 1) * BLOCK, BLOCK), 2 * BLOCK)
                o, lse = _attn_block(q, k_ref[0, r, kv_rows, :], v_ref[0, r, kv_rows, :], bias_both)
                store(r, n, o, lse)
                return c2

            if nb > 1:
                lax.fori_loop(1, nb, block, 0)
            return carry

        if d == 1:
            residue(0, 0)
        else:
            lax.fori_loop(0, d, residue, 0)

    rc = 256

    def combine(i, carry):
        rows = pl.ds(pl.multiple_of(i * rc, rc), rc)
        l0, l1, l2 = lse_scr[0, rows, :], lse_scr[1, rows, :], lse_scr[2, rows, :]
        m = jnp.maximum(jnp.maximum(l0, l1), l2)
        e0, e1, e2 = jnp.exp(l0 - m), jnp.exp(l1 - m), jnp.exp(l2 - m)
        o = (e0 * o_scr[0, rows, :] + e1 * o_scr[1, rows, :] + e2 * o_scr[2, rows, :]) / (e0 + e1 + e2)
        z = z0[0, 0, rows, :].astype(F32)
        y_ref[0, rows, :] = (o * _silu(z)).astype(BF16)
        return carry

    lax.fori_loop(0, seq // rc, combine, 0)


def _attention(p0, p1, p2):
    B, _, S, _ = p0.shape
    H = HEADS_PER_GROUP

    def spec(d, sec):
        return pl.BlockSpec((1, d, S // d, LANES), lambda b, h: (b, 0, 0, sec * H + h))

    in_specs = [spec(1, 0), spec(1, 1), spec(1, 2), spec(1, 3)]
    for d in (DIL_GROUPS[1][1], DIL_GROUPS[2][1]):
        in_specs += [spec(d, 0), spec(d, 1), spec(d, 2)]
    return pl.pallas_call(
        functools.partial(_attn_kernel, seq=S),
        out_shape=jax.ShapeDtypeStruct((B, S, ATTN_OUT), BF16),
        grid=(B, H),
        in_specs=in_specs,
        out_specs=pl.BlockSpec((1, S, LANES), lambda b, h: (b, 0, h)),
        scratch_shapes=[pltpu.VMEM((N_DIL, S, LANES), F32), pltpu.VMEM((N_DIL, S, LANES), F32)],
        compiler_params=_params(2),
        name="dilated_attention",
    )(p0, p0, p0, p0, p1, p1, p1, p2, p2, p2)


def _out_proj_kernel(y_ref, w_ref, x_ref, gate_ref, o_ref):
    acc = jnp.dot(y_ref[...], w_ref[...], preferred_element_type=F32)
    o_ref[...] = x_ref[...] + gate_ref[0] * acc


def _out_proj(y, w_bf16, x, gate, *, tm=512, tn=1024):
    B, S, D = x.shape
    M, K = y.shape
    tiles_m = S // tm
    out = pl.pallas_call(
        _out_proj_kernel,
        out_shape=jax.ShapeDtypeStruct((M, D), F32),
        grid=(M // tm, D // tn),
        in_specs=[pl.BlockSpec((tm, K), lambda i, j: (i, 0)),
                  pl.BlockSpec((K, tn), lambda i, j: (0, j)),
                  pl.BlockSpec((tm, tn), lambda i, j: (i, j)),
                  pl.BlockSpec((1, 1, tn), lambda i, j: (i // tiles_m, 0, j))],
        out_specs=pl.BlockSpec((tm, tn), lambda i, j: (i, j)),
        compiler_params=_params(2),
        name="out_proj",
    )(y, w_bf16, x.reshape(M, D), gate)
    return out.reshape(B, S, D)


def _sgu_proj_kernel(x_ref, sc_ref, sh_ref, g_ref, w_ref, o_ref, h_ref, *, tm, tn):
    j = pl.program_id(1)

    @pl.when(j == 0)
    def _():
        _norm_prologue(x_ref, sc_ref, sh_ref, g_ref, h_ref, None, d=1, tm=tm)

    acc = jnp.dot(h_ref[...], w_ref[...], preferred_element_type=F32)

    @pl.when(j < 2 * SGU_WIDTH // tn)
    def _():
        o_ref[...] = _gelu_tanh(acc).astype(BF16)

    @pl.when(j >= 2 * SGU_WIDTH // tn)
    def _():
        o_ref[...] = acc.astype(BF16)


def _sgu_proj(x, scale, shift, norm_g, w_bf16, *, tm=512, tn=1024):
    B, S, D = x.shape
    N = w_bf16.shape[1]
    tiles_m = S // tm
    return pl.pallas_call(
        functools.partial(_sgu_proj_kernel, tm=tm, tn=tn),
        out_shape=jax.ShapeDtypeStruct((B * S, N), BF16),
        grid=(B * tiles_m, N // tn),
        in_specs=[pl.BlockSpec((1, tm, D), lambda i, j: (i // tiles_m, i % tiles_m, 0)),
                  pl.BlockSpec((1, 1, D), lambda i, j: (i // tiles_m, 0, 0)),
                  pl.BlockSpec((1, 1, D), lambda i, j: (i // tiles_m, 0, 0)),
                  pl.BlockSpec((1, D), lambda i, j: (0, 0)),
                  pl.BlockSpec((D, tn), lambda i, j: (0, j))],
        out_specs=pl.BlockSpec((tm, tn), lambda i, j: (i, j)),
        scratch_shapes=[pltpu.VMEM((tm, D), BF16)],
        compiler_params=_params(2),
        name="sgu_proj",
    )(x, scale, shift, norm_g, w_bf16)


def _sgu_core_kernel(u_ref, v_ref, z_ref, lng_ref, lnb_ref, ws_ref, bst_ref, y_ref, wm_ref, vn_ref, *, tc):
    t_out = lax.broadcasted_iota(jnp.int32, (SGU_CHUNK, SGU_CHUNK), 0)
    s_in = lax.broadcasted_iota(jnp.int32, (SGU_CHUNK, SGU_CHUNK), 1)
    causal = s_in <= t_out
    for g in range(SGU_GROUPS):
        wm_ref[g] = jnp.where(causal, ws_ref[g], 0.0).astype(BF16)
    gw = SGU_WIDTH // SGU_GROUPS

    def chunk(ci, carry):
        rows = pl.ds(pl.multiple_of(ci * SGU_CHUNK, SGU_CHUNK), SGU_CHUNK)
        v = v_ref[rows, :].astype(F32)
        mu = jnp.mean(v, axis=-1, keepdims=True)
        vc = v - mu
        var = jnp.mean(vc * vc, axis=-1, keepdims=True)
        vn_ref[...] = (vc * lax.rsqrt(var + NORM_EPS) * lng_ref[...] + lnb_ref[...]).astype(BF16)
        for g in range(SGU_GROUPS):
            cols = slice(g * gw, (g + 1) * gw)
            sv = jnp.dot(wm_ref[g], vn_ref[:, cols], preferred_element_type=F32) + bst_ref[:, g:g + 1]
            z = z_ref[rows, cols].astype(F32)
            y_ref[rows, cols] = (u_ref[rows, cols].astype(F32) * sv * _silu(z)).astype(BF16)
        return carry

    lax.fori_loop(0, tc // SGU_CHUNK, chunk, 0)


def _sgu_core(proj, ln_g, ln_b, ws, bs, *, tc=256):
    M = proj.shape[0]
    E = SGU_WIDTH
    return pl.pallas_call(
        functools.partial(_sgu_core_kernel, tc=tc),
        out_shape=jax.ShapeDtypeStruct((M, E), BF16),
        grid=(M // tc,),
        in_specs=[pl.BlockSpec((tc, E), lambda i: (i, 0)),
                  pl.BlockSpec((tc, E), lambda i: (i, 1)),
                  pl.BlockSpec((tc, E), lambda i: (i, 2)),
                  pl.BlockSpec((1, E), lambda i: (0, 0)),
                  pl.BlockSpec((1, E), lambda i: (0, 0)),
                  pl.BlockSpec((SGU_GROUPS, SGU_CHUNK, SGU_CHUNK), lambda i: (0, 0, 0)),
                  pl.BlockSpec((SGU_CHUNK, SGU_GROUPS), lambda i: (0, 0))],
        out_specs=pl.BlockSpec((tc, E), lambda i: (i, 0)),
        scratch_shapes=[pltpu.VMEM((SGU_GROUPS, SGU_CHUNK, SGU_CHUNK), BF16),
                        pltpu.VMEM((SGU_CHUNK, E), BF16)],
        compiler_params=_params(1),
        name="sgu_core",
    )(proj, proj, proj, ln_g.reshape(1, E), ln_b.reshape(1, E), ws, bs.T)


def _conv_proj_kernel(x_ref, sc_ref, sh_ref, g_ref, wa_ref, wb_ref, o_ref, h_ref, *, tm, tn):
    j = pl.program_id(1)

    @pl.when(j == 0)
    def _():
        _norm_prologue(x_ref, sc_ref, sh_ref, g_ref, h_ref, None, d=1, tm=tm)

    @pl.when(j < CONV_WIDTH // tn)
    def _():
        a = jnp.dot(h_ref[...], wa_ref[...], preferred_element_type=F32)
        b = jnp.dot(h_ref[...], wb_ref[...], preferred_element_type=F32)
        o_ref[...] = (a * _sigmoid(b)).astype(BF16)

    @pl.when(j >= CONV_WIDTH // tn)
    def _():
        o_ref[...] = jnp.dot(h_ref[...], wa_ref[...], preferred_element_type=F32).astype(BF16)


def _conv_proj(x, scale, shift, norm_g, w_bf16, *, tm=512, tn=512):
    B, S, D = x.shape
    E = CONV_WIDTH
    nj = E // tn
    tiles_m = S // tm
    return pl.pallas_call(
        functools.partial(_conv_proj_kernel, tm=tm, tn=tn),
        out_shape=jax.ShapeDtypeStruct((B * S, 2 * E), BF16),
        grid=(B * tiles_m, 2 * nj),
        in_specs=[pl.BlockSpec((1, tm, D), lambda i, j: (i // tiles_m, i % tiles_m, 0)),
                  pl.BlockSpec((1, 1, D), lambda i, j: (i // tiles_m, 0, 0)),
                  pl.BlockSpec((1, 1, D), lambda i, j: (i // tiles_m, 0, 0)),
                  pl.BlockSpec((1, D), lambda i, j: (0, 0)),
                  pl.BlockSpec((D, tn), lambda i, j: (0, jnp.where(j < nj, j, nj + j))),
                  pl.BlockSpec((D, tn), lambda i, j: (0, jnp.minimum(nj + j, 2 * nj - 1)))],
        out_specs=pl.BlockSpec((tm, tn), lambda i, j: (i, j)),
        scratch_shapes=[pltpu.VMEM((tm, D), BF16)],
        compiler_params=_params(2),
        name="conv_proj",
    )(x, scale, shift, norm_g, w_bf16, w_bf16)


def _conv_core_kernel(g_ref, halo_ref, z_ref, w_ref, b_ref, lng_ref, lnb_ref, y_ref, ext_ref, cv_ref,
                      *, tc, tiles_per_seq):
    n_slabs = CONV_WIDTH // LANES
    first = (pl.program_id(0) % tiles_per_seq) == 0
    for c in range(n_slabs):
        cols = slice(c * LANES, (c + 1) * LANES)
        halo = halo_ref[:, cols].astype(F32)
        ext_ref[c, 0:HALO, :] = jnp.where(first, 0.0, halo)
        ext_ref[c, HALO:, :] = g_ref[:, cols].astype(F32)

    rc = 64
    lead = HALO - (CONV_K - 1)

    def slab(c, carry):
        for r0 in range(0, tc, rc):
            acc = jnp.broadcast_to(b_ref[c], (rc, LANES))
            for k in range(CONV_K):
                acc = acc + w_ref[c, k:k + 1, :] * ext_ref[c, r0 + lead + k:r0 + lead + k + rc, :]
            cv_ref[c, r0:r0 + rc, :] = acc
        return carry

    lax.fori_loop(0, n_slabs, slab, 0)

    inv_e = 1.0 / CONV_WIDTH
    for r0 in range(0, tc, rc):
        rows = slice(r0, r0 + rc)
        tot = cv_ref[0, rows, :]
        for c in range(1, n_slabs):
            tot = tot + cv_ref[c, rows, :]
        mu = jnp.sum(tot, axis=-1, keepdims=True) * inv_e
        sq = jnp.zeros((rc, LANES), F32)
        for c in range(n_slabs):
            dlt = cv_ref[c, rows, :] - mu
            sq = sq + dlt * dlt
        inv = lax.rsqrt(jnp.sum(sq, axis=-1, keepdims=True) * inv_e + NORM_EPS)
        for c in range(n_slabs):
            cols = slice(c * LANES, (c + 1) * LANES)
            t = (cv_ref[c, rows, :] - mu) * inv * lng_ref[:, cols] + lnb_ref[:, cols]
            z = z_ref[rows, cols].astype(F32)
            y_ref[rows, cols] = (_silu(t) * _silu(z)).astype(BF16)


def _conv_core(proj, seq, dw_w, dw_b, ln_g, ln_b, *, tc=256):
    M = proj.shape[0]
    E = CONV_WIDTH
    n_slabs = E // LANES
    per_halo = tc // HALO
    w_slabs = dw_w.reshape(CONV_K, n_slabs, LANES).transpose(1, 0, 2)
    return pl.pallas_call(
        functools.partial(_conv_core_kernel, tc=tc, tiles_per_seq=seq // tc),
        out_shape=jax.ShapeDtypeStruct((M, E), BF16),
        grid=(M // tc,),
        in_specs=[pl.BlockSpec((tc, E), lambda i: (i, 0)),
                  pl.BlockSpec((HALO, E), lambda i: (jnp.maximum(i * per_halo - 1, 0), 0)),
                  pl.BlockSpec((tc, E), lambda i: (i, 1)),
                  pl.BlockSpec((n_slabs, CONV_K, LANES), lambda i: (0, 0, 0)),
                  pl.BlockSpec((n_slabs, 1, LANES), lambda i: (0, 0, 0)),
                  pl.BlockSpec((1, E), lambda i: (0, 0)),
                  pl.BlockSpec((1, E), lambda i: (0, 0))],
        out_specs=pl.BlockSpec((tc, E), lambda i: (i, 0)),
        scratch_shapes=[pltpu.VMEM((n_slabs, tc + HALO, LANES), F32),
                        pltpu.VMEM((n_slabs, tc, LANES), F32)],
        compiler_params=_params(1),
        name="conv_core",
    )(proj, proj, proj, w_slabs, dw_b.reshape(n_slabs, 1, LANES), ln_g.reshape(1, E), ln_b.reshape(1, E))


def _rope_tables(positions):
    B, S = positions.shape
    half = ROPE_DIM // 2
    inv_freq = jnp.power(ROPE_THETA, -jnp.arange(0, ROPE_DIM, 2, dtype=F32) / ROPE_DIM)
    ang = positions.astype(F32)[..., None] * inv_freq
    cos, sin = jnp.cos(ang), jnp.sin(ang)
    rest = HEAD_DIM - ROPE_DIM
    cos_t = jnp.concatenate([cos, cos, jnp.ones((B, S, rest), F32)], axis=-1)
    sa = jnp.concatenate([-sin, jnp.zeros((B, S, half + rest), F32)], axis=-1)
    sb = jnp.concatenate([jnp.zeros((B, S, half), F32), sin, jnp.zeros((B, S, rest), F32)], axis=-1)
    out = []
    for _, d in DIL_GROUPS:
        out.append(tuple(t.reshape(B, S // d, d, HEAD_DIM).transpose(0, 2, 1, 3) for t in (cos_t, sa, sb)))
    return out


def kernel(x, c, positions, ada_w, ada_b, norm_g, attn_w_in, attn_q_gain, attn_k_gain, attn_w_out,
           sgu_w_in, sgu_ln_g, sgu_ln_b, sgu_ws, sgu_bs, sgu_w_out,
           conv_w_in, conv_dw_w, conv_dw_b, conv_ln_g, conv_ln_b, conv_w_out):
    B, S, D = x.shape
    mod = _ada_modulation(c, ada_w, ada_b)
    tables = _rope_tables(positions)
    for i in range(DEPTH):
        shift = mod[i, :, None, :D]
        scale = mod[i, :, None, D:2 * D]
        gate = mod[i, :, None, 2 * D:]
        g_row = norm_g[i].reshape(1, D)
        kind, l = i % N_MIXERS, i // N_MIXERS
        if kind == 0:
            w_in = attn_w_in[l].astype(BF16)
            gains = jnp.stack([attn_q_gain[l] * (HEAD_DIM ** -0.5), attn_k_gain[l]])
            proj = [_attn_proj(x, scale, shift, g_row, w_in, gains, tables[g], group=g, with_gate=(g == 0))
                    for g in range(N_DIL)]
            y = _attention(*proj).reshape(B * S, ATTN_OUT)
            x = _out_proj(y, attn_w_out[l].astype(BF16), x, gate)
        elif kind == 1:
            proj = _sgu_proj(x, scale, shift, g_row, sgu_w_in[l].astype(BF16))
            y = _sgu_core(proj, sgu_ln_g[l], sgu_ln_b[l], sgu_ws[l], sgu_bs[l])
            x = _out_proj(y, sgu_w_out[l].astype(BF16), x, gate)
        else:
            proj = _conv_proj(x, scale, shift, g_row, conv_w_in[l].astype(BF16))
            y = _conv_core(proj, S, conv_dw_w[l], conv_dw_b[l], conv_ln_g[l], conv_ln_b[l])
            x = _out_proj(y, conv_w_out[l].astype(BF16), x, gate)
    return x
```

```python
import functools

import jax
import jax.numpy as jnp
from jax import lax
from jax.experimental import pallas as pl
from jax.experimental.pallas import tpu as pltpu

D_MODEL = 2048
DEPTH = 4
N_MIXERS = 3
HEAD_DIM = 128
DIL_GROUPS = ((128, 1), (512, 4), (2048, 16))
N_DIL = len(DIL_GROUPS)
HEADS_PER_GROUP = D_MODEL // HEAD_DIM
ATTN_OUT = HEADS_PER_GROUP * HEAD_DIM
ATTN_QKV = N_DIL * ATTN_OUT
ROPE_DIM = HEAD_DIM // 4
ROPE_THETA = 500000.0
BLOCK = 128
SGU_WIDTH = 2 * D_MODEL
SGU_CHUNK = 128
SGU_GROUPS = 16
CONV_WIDTH = 2 * D_MODEL
CONV_K = 31
NORM_EPS = 1e-6

LANES = 128
MXU_N = 256
SUB = 4
HALO = 32
VMEM_LIMIT = 56 * 1024 * 1024
MASKED = -1e30

F32 = jnp.float32
BF16 = jnp.bfloat16


def _params(n_axes):
    return pltpu.CompilerParams(dimension_semantics=("arbitrary",) * n_axes,
                                vmem_limit_bytes=VMEM_LIMIT)


def _sigmoid(x):
    return 1.0 / (1.0 + jnp.exp(-x))


def _silu(x):
    return x * _sigmoid(x)


def _gelu_tanh(x):
    return x * (0.5 * (1.0 + jnp.tanh(0.7978845608028654 * (x + 0.044715 * (x * x * x)))))


def _ada_kernel(c_ref, w_ref, b_ref, o_ref):
    c = c_ref[...]
    o_ref[0] = jnp.dot(_silu(c), w_ref[0], preferred_element_type=F32,
                       precision=lax.Precision.HIGHEST) + b_ref[0]


def _ada_modulation(c, ada_w, ada_b):
    B, D = c.shape
    rows = 8
    tn = 1024
    c8 = jnp.pad(c, ((0, rows - B), (0, 0)))
    out = pl.pallas_call(
        _ada_kernel,
        out_shape=jax.ShapeDtypeStruct((DEPTH, rows, 3 * D), F32),
        grid=(DEPTH, 3 * D // tn),
        in_specs=[pl.BlockSpec((rows, D), lambda l, j: (0, 0)),
                  pl.BlockSpec((1, D, tn), lambda l, j: (l, 0, j)),
                  pl.BlockSpec((1, 1, tn), lambda l, j: (l, 0, j))],
        out_specs=pl.BlockSpec((1, rows, tn), lambda l, j: (l, 0, j)),
        compiler_params=_params(2),
        name="ada_modulation",
    )(c8, ada_w, ada_b.reshape(DEPTH, 1, 3 * D))
    return out[:, :B]


def _norm_prologue(x_ref, sc_ref, sh_ref, g_ref, h_ref, slab_ref, *, d, tm):
    rc = 128
    a = g_ref[...] * (1.0 + sc_ref[0])
    sh = sh_ref[0]
    n_slabs = D_MODEL // LANES

    def body(i, carry):
        r0 = pl.multiple_of(i * rc, rc)
        x = x_ref[0, pl.ds(r0, rc), :]
        ms = jnp.mean(x * x, axis=-1, keepdims=True)
        h = x * lax.rsqrt(ms + NORM_EPS) * a + sh
        if d == 1:
            h_ref[pl.ds(r0, rc), :] = h.astype(BF16)
        else:
            for c in range(n_slabs):
                slab_ref[c, pl.ds(r0, rc), :] = h[:, c * LANES:(c + 1) * LANES]
        return carry

    lax.fori_loop(0, tm // rc, body, 0)
    if d > 1:
        rows = tm // d
        for r in range(d):
            for c in range(n_slabs):
                h_ref[r * rows:(r + 1) * rows, c * LANES:(c + 1) * LANES] = (
                    slab_ref[c, pl.ds(r, rows, stride=d), :].astype(BF16))


def _attn_proj_kernel(x_ref, sc_ref, sh_ref, g_ref, w_ref, perm_ref, cos_ref, sin_ref,
                      o_ref, h_ref, acc_a, acc_b, part_ref, *slab, d, tm, tn, n_rope):
    j = pl.program_id(1)
    rows = tm // d
    rc = min(32, rows)
    per_res = rows // rc
    n_sub = tn // MXU_N

    @pl.when(j == 0)
    def _():
        _norm_prologue(x_ref, sc_ref, sh_ref, g_ref, h_ref, slab[0] if slab else None, d=d, tm=tm)

    def finish(acc_ref, c0, rope):
        if rope:
            part_ref[...] = jnp.dot(acc_ref[...].astype(BF16), perm_ref[...], preferred_element_type=F32)
        for ci in range(tm // rc):
            r, m0, p0 = ci // per_res, (ci % per_res) * rc, ci * rc
            if not rope:
                o_ref[0, r, m0:m0 + rc, c0:c0 + MXU_N] = acc_ref[p0:p0 + rc, :].astype(BF16)
                continue
            cos = cos_ref[0, 0, r, m0:m0 + rc, :]
            sin = sin_ref[0, 0, r, m0:m0 + rc, :]
            for hc in range(MXU_N // LANES):
                cols = slice(hc * LANES, (hc + 1) * LANES)
                a = acc_ref[p0:p0 + rc, cols]
                ms = jnp.sum(a * a, axis=-1, keepdims=True) * (1.0 / HEAD_DIM)
                rot = (a * cos + part_ref[p0:p0 + rc, cols] * sin) * lax.rsqrt(ms + NORM_EPS)
                o_ref[0, r, m0:m0 + rc, c0 + hc * LANES:c0 + (hc + 1) * LANES] = rot.astype(BF16)

    def tile(rope):
        bufs = (acc_a, acc_b)
        for c in range(n_sub + 1):
            if c < n_sub:
                bufs[c % 2][...] = jnp.dot(h_ref[...], w_ref[:, c * MXU_N:(c + 1) * MXU_N],
                                           preferred_element_type=F32)
            if c >= 1:
                finish(bufs[(c - 1) % 2], (c - 1) * MXU_N, rope)

    @pl.when(j < n_rope)
    def _():
        tile(True)

    @pl.when(j >= n_rope)
    def _():
        tile(False)


def _attn_proj(x, scale, shift, norm_g, w_bf16, tables, *, group, with_gate, tm=512, tn=1024):
    B, S, D = x.shape
    d = DIL_GROUPS[group][1]
    n_sec = 4 if with_gate else 3
    per_sec = ATTN_OUT // tn
    tiles_m = S // tm

    def w_index(i, j):
        sec = j // per_sec
        base = jnp.where(sec < 3, sec * (ATTN_QKV // tn) + group * per_sec, 3 * (ATTN_QKV // tn))
        return (0, base + j % per_sec)

    tab_spec = pl.BlockSpec((1, 1, d, tm // d, LANES),
                            lambda i, j: (jnp.minimum(j // per_sec, 1), i // tiles_m, 0, i % tiles_m, 0))
    scratch = [pltpu.VMEM((tm, D), BF16)] + [pltpu.VMEM((tm, MXU_N), F32)] * 3
    if d > 1:
        scratch.append(pltpu.VMEM((D // LANES, tm, LANES), F32))
    return pl.pallas_call(
        functools.partial(_attn_proj_kernel, d=d, tm=tm, tn=tn, n_rope=2 * per_sec),
        out_shape=jax.ShapeDtypeStruct((B, d, S // d, n_sec * ATTN_OUT), BF16),
        grid=(B * tiles_m, n_sec * per_sec),
        in_specs=[pl.BlockSpec((1, tm, D), lambda i, j: (i // tiles_m, i % tiles_m, 0)),
                  pl.BlockSpec((1, 1, D), lambda i, j: (i // tiles_m, 0, 0)),
                  pl.BlockSpec((1, 1, D), lambda i, j: (i // tiles_m, 0, 0)),
                  pl.BlockSpec((1, D), lambda i, j: (0, 0)),
                  pl.BlockSpec((D, tn), w_index),
                  pl.BlockSpec((MXU_N, MXU_N), lambda i, j: (0, 0)),
                  tab_spec, tab_spec],
        out_specs=pl.BlockSpec((1, d, tm // d, tn), lambda i, j: (i // tiles_m, 0, i % tiles_m, j)),
        scratch_shapes=scratch,
        compiler_params=_params(2),
        name=f"attn_proj_g{group}",
    )(x, scale, shift, norm_g, w_bf16, _rotate_half_matrix(), *tables)


def _attn_block(q, k, v, bias):
    s = lax.dot_general(q, k, (((1,), (1,)), ((), ())), preferred_element_type=F32) + bias
    m = jnp.max(s, axis=-1, keepdims=True)
    p = jnp.exp(s - m).astype(BF16)
    v_ext = jnp.concatenate([v, jnp.ones(v.shape, BF16)], axis=1)
    ol = jnp.dot(p, v_ext, preferred_element_type=F32)
    l = ol[:, HEAD_DIM:]
    return ol[:, :HEAD_DIM] * (1.0 / l), m + jnp.log(l)


def _attn_kernel(q0, k0, v0, z0, q1, k1, v1, q2, k2, v2, y_ref, o_scr, lse_scr, *, seq):
    row = lax.broadcasted_iota(jnp.int32, (BLOCK, 2 * BLOCK), 0)
    col = lax.broadcasted_iota(jnp.int32, (BLOCK, 2 * BLOCK), 1)
    dist = row + BLOCK - col
    bias_both = jnp.where(dist >= 0, jnp.where(dist <= BLOCK, 0.0, MASKED), MASKED).astype(F32)
    bias_first = bias_both[:, BLOCK:]

    groups = ((q0, k0, v0), (q1, k1, v1), (q2, k2, v2))
    for g, (q_ref, k_ref, v_ref) in enumerate(groups):
        d = DIL_GROUPS[g][1]
        nb = seq // d // BLOCK

        def store(r, n, o, lse, g=g, d=d):
            if d == 1:
                start = BLOCK * n
                rows = pl.ds(start if isinstance(start, int) else pl.multiple_of(start, BLOCK), BLOCK)
            elif d <= SUB:
                rows = pl.ds(r + d * BLOCK * n, BLOCK, stride=d)
            else:
                rows = pl.ds((r % SUB) * (seq // SUB) + r // SUB + (d // SUB) * BLOCK * n, BLOCK,
                             stride=d // SUB)
            slot = g if d <= SUB else N_DIL
            o_scr[slot, rows, :] = o
            lse_scr[slot, rows, :] = lse

        def first_block(r, q_ref=q_ref, k_ref=k_ref, v_ref=v_ref, store=store):
            o, lse = _attn_block(q_ref[0, r, 0:BLOCK, :], k_ref[0, r, 0:BLOCK, :],
                                 v_ref[0, r, 0:BLOCK, :], bias_first)
            store(r, 0, o, lse)

        def later_block(r, n, q_ref=q_ref, k_ref=k_ref, v_ref=v_ref, store=store):
            if isinstance(n, int):
                q_rows, kv_rows = pl.ds(n * BLOCK, BLOCK), pl.ds((n - 1) * BLOCK, 2 * BLOCK)
            else:
                q_rows = pl.ds(pl.multiple_of(n * BLOCK, BLOCK), BLOCK)
                kv_rows = pl.ds(pl.multiple_of((n - 1) * BLOCK, BLOCK), 2 * BLOCK)
            o, lse = _attn_block(q_ref[0, r, q_rows, :], k_ref[0, r, kv_rows, :], v_ref[0, r, kv_rows, :],
                                 bias_both)
            store(r, n, o, lse)

        if d == 1:
            first_block(0)

            def blocks(n, carry, later_block=later_block):
                later_block(0, n)
                return carry

            lax.fori_loop(1, nb, blocks, 0, unroll=5)
        else:
            def residue(r, carry, nb=nb, first_block=first_block, later_block=later_block):
                first_block(r)
                for n in range(1, nb):
                    later_block(r, n)
                return carry

            lax.fori_loop(0, d, residue, 0, unroll=(2 if nb > 1 else 8))

        if d > SUB:
            part = seq // SUB
            for r_lo in range(SUB):
                for scr in (o_scr, lse_scr):
                    scr[g, pl.ds(r_lo, part, stride=SUB), :] = scr[N_DIL, r_lo * part:(r_lo + 1) * part, :]

    rc = 256

    def combine(i, carry):
        rows = pl.ds(pl.multiple_of(i * rc, rc), rc)
        l0, l1, l2 = lse_scr[0, rows, :], lse_scr[1, rows, :], lse_scr[2, rows, :]
        m = jnp.maximum(jnp.maximum(l0, l1), l2)
        e0, e1, e2 = jnp.exp(l0 - m), jnp.exp(l1 - m), jnp.exp(l2 - m)
        o = (e0 * o_scr[0, rows, :] + e1 * o_scr[1, rows, :] + e2 * o_scr[2, rows, :]) / (e0 + e1 + e2)
        z = z0[0, 0, rows, :].astype(F32)
        y_ref[0, rows, :] = (o * _silu(z)).astype(BF16)
        return carry

    lax.fori_loop(0, seq // rc, combine, 0)


def _attention(p0, p1, p2):
    B, _, S, _ = p0.shape
    H = HEADS_PER_GROUP

    def spec(d, sec):
        return pl.BlockSpec((1, d, S // d, LANES), lambda b, h: (b, 0, 0, sec * H + h))

    in_specs = [spec(1, 0), spec(1, 1), spec(1, 2), spec(1, 3)]
    for d in (DIL_GROUPS[1][1], DIL_GROUPS[2][1]):
        in_specs += [spec(d, 0), spec(d, 1), spec(d, 2)]
    return pl.pallas_call(
        functools.partial(_attn_kernel, seq=S),
        out_shape=jax.ShapeDtypeStruct((B, S, ATTN_OUT), BF16),
        grid=(B, H),
        in_specs=in_specs,
        out_specs=pl.BlockSpec((1, S, LANES), lambda b, h: (b, 0, h)),
        scratch_shapes=[pltpu.VMEM((N_DIL + 1, S, LANES), F32), pltpu.VMEM((N_DIL + 1, S, LANES), F32)],
        compiler_params=_params(2),
        name="dilated_attention",
    )(p0, p0, p0, p0, p1, p1, p1, p2, p2, p2)


def _out_proj_kernel(y_ref, w_ref, x_ref, gate_ref, o_ref):
    acc = jnp.dot(y_ref[...], w_ref[...], preferred_element_type=F32)
    o_ref[...] = x_ref[...] + gate_ref[0] * acc


def _out_proj(y, w_bf16, x, gate, *, tm=512, tn=1024):
    B, S, D = x.shape
    M, K = y.shape
    tiles_m = S // tm
    out = pl.pallas_call(
        _out_proj_kernel,
        out_shape=jax.ShapeDtypeStruct((M, D), F32),
        grid=(M // tm, D // tn),
        in_specs=[pl.BlockSpec((tm, K), lambda i, j: (i, 0)),
                  pl.BlockSpec((K, tn), lambda i, j: (0, j)),
                  pl.BlockSpec((tm, tn), lambda i, j: (i, j)),
                  pl.BlockSpec((1, 1, tn), lambda i, j: (i // tiles_m, 0, j))],
        out_specs=pl.BlockSpec((tm, tn), lambda i, j: (i, j)),
        compiler_params=_params(2),
        name="out_proj",
    )(y, w_bf16, x.reshape(M, D), gate)
    return out.reshape(B, S, D)


def _sgu_proj_kernel(x_ref, sc_ref, sh_ref, g_ref, w_ref, o_ref, h_ref, *, tm, tn):
    j = pl.program_id(1)

    @pl.when(j == 0)
    def _():
        _norm_prologue(x_ref, sc_ref, sh_ref, g_ref, h_ref, None, d=1, tm=tm)

    acc = jnp.dot(h_ref[...], w_ref[...], preferred_element_type=F32)

    @pl.when(j < 2 * SGU_WIDTH // tn)
    def _():
        o_ref[...] = _gelu_tanh(acc).astype(BF16)

    @pl.when(j >= 2 * SGU_WIDTH // tn)
    def _():
        o_ref[...] = acc.astype(BF16)


def _sgu_proj(x, scale, shift, norm_g, w_bf16, *, tm=512, tn=1024):
    B, S, D = x.shape
    N = w_bf16.shape[1]
    tiles_m = S // tm
    return pl.pallas_call(
        functools.partial(_sgu_proj_kernel, tm=tm, tn=tn),
        out_shape=jax.ShapeDtypeStruct((B * S, N), BF16),
        grid=(B * tiles_m, N // tn),
        in_specs=[pl.BlockSpec((1, tm, D), lambda i, j: (i // tiles_m, i % tiles_m, 0)),
                  pl.BlockSpec((1, 1, D), lambda i, j: (i // tiles_m, 0, 0)),
                  pl.BlockSpec((1, 1, D), lambda i, j: (i // tiles_m, 0, 0)),
                  pl.BlockSpec((1, D), lambda i, j: (0, 0)),
                  pl.BlockSpec((D, tn), lambda i, j: (0, j))],
        out_specs=pl.BlockSpec((tm, tn), lambda i, j: (i, j)),
        scratch_shapes=[pltpu.VMEM((tm, D), BF16)],
        compiler_params=_params(2),
        name="sgu_proj",
    )(x, scale, shift, norm_g, w_bf16)


def _sgu_core_kernel(u_ref, v_ref, z_ref, lng_ref, lnb_ref, ws_ref, bst_ref, y_ref, wm_ref, vn_ref, *, tc):
    t_out = lax.broadcasted_iota(jnp.int32, (SGU_CHUNK, SGU_CHUNK), 0)
    s_in = lax.broadcasted_iota(jnp.int32, (SGU_CHUNK, SGU_CHUNK), 1)
    causal = s_in <= t_out
    for g in range(SGU_GROUPS):
        wm_ref[g] = jnp.where(causal, ws_ref[g], 0.0).astype(BF16)
    gw = SGU_WIDTH // SGU_GROUPS

    def chunk(ci, carry):
        rows = pl.ds(pl.multiple_of(ci * SGU_CHUNK, SGU_CHUNK), SGU_CHUNK)
        v = v_ref[rows, :].astype(F32)
        mu = jnp.mean(v, axis=-1, keepdims=True)
        vc = v - mu
        var = jnp.mean(vc * vc, axis=-1, keepdims=True)
        vn_ref[...] = (vc * lax.rsqrt(var + NORM_EPS) * lng_ref[...] + lnb_ref[...]).astype(BF16)
        for g in range(SGU_GROUPS):
            cols = slice(g * gw, (g + 1) * gw)
            sv = jnp.dot(wm_ref[g], vn_ref[:, cols], preferred_element_type=F32) + bst_ref[:, g:g + 1]
            z = z_ref[rows, cols].astype(F32)
            y_ref[rows, cols] = (u_ref[rows, cols].astype(F32) * sv * _silu(z)).astype(BF16)
        return carry

    lax.fori_loop(0, tc // SGU_CHUNK, chunk, 0)


def _sgu_core(proj, ln_g, ln_b, ws, bs, *, tc=256):
    M = proj.shape[0]
    E = SGU_WIDTH
    return pl.pallas_call(
        functools.partial(_sgu_core_kernel, tc=tc),
        out_shape=jax.ShapeDtypeStruct((M, E), BF16),
        grid=(M // tc,),
        in_specs=[pl.BlockSpec((tc, E), lambda i: (i, 0)),
                  pl.BlockSpec((tc, E), lambda i: (i, 1)),
                  pl.BlockSpec((tc, E), lambda i: (i, 2)),
                  pl.BlockSpec((1, E), lambda i: (0, 0)),
                  pl.BlockSpec((1, E), lambda i: (0, 0)),
                  pl.BlockSpec((SGU_GROUPS, SGU_CHUNK, SGU_CHUNK), lambda i: (0, 0, 0)),
                  pl.BlockSpec((SGU_CHUNK, SGU_GROUPS), lambda i: (0, 0))],
        out_specs=pl.BlockSpec((tc, E), lambda i: (i, 0)),
        scratch_shapes=[pltpu.VMEM((SGU_GROUPS, SGU_CHUNK, SGU_CHUNK), BF16),
                        pltpu.VMEM((SGU_CHUNK, E), BF16)],
        compiler_params=_params(1),
        name="sgu_core",
    )(proj, proj, proj, ln_g.reshape(1, E), ln_b.reshape(1, E), ws, bs.T)


def _conv_proj_kernel(x_ref, sc_ref, sh_ref, g_ref, wa_ref, wb_ref, o_ref, h_ref, *, tm, tn):
    j = pl.program_id(1)

    @pl.when(j == 0)
    def _():
        _norm_prologue(x_ref, sc_ref, sh_ref, g_ref, h_ref, None, d=1, tm=tm)

    @pl.when(j < CONV_WIDTH // tn)
    def _():
        a = jnp.dot(h_ref[...], wa_ref[...], preferred_element_type=F32)
        b = jnp.dot(h_ref[...], wb_ref[...], preferred_element_type=F32)
        o_ref[...] = (a * _sigmoid(b)).astype(BF16)

    @pl.when(j >= CONV_WIDTH // tn)
    def _():
        o_ref[...] = jnp.dot(h_ref[...], wa_ref[...], preferred_element_type=F32).astype(BF16)


def _conv_proj(x, scale, shift, norm_g, w_bf16, *, tm=512, tn=512):
    B, S, D = x.shape
    E = CONV_WIDTH
    nj = E // tn
    tiles_m = S // tm
    return pl.pallas_call(
        functools.partial(_conv_proj_kernel, tm=tm, tn=tn),
        out_shape=jax.ShapeDtypeStruct((B * S, 2 * E), BF16),
        grid=(B * tiles_m, 2 * nj),
        in_specs=[pl.BlockSpec((1, tm, D), lambda i, j: (i // tiles_m, i % tiles_m, 0)),
                  pl.BlockSpec((1, 1, D), lambda i, j: (i // tiles_m, 0, 0)),
                  pl.BlockSpec((1, 1, D), lambda i, j: (i // tiles_m, 0, 0)),
                  pl.BlockSpec((1, D), lambda i, j: (0, 0)),
                  pl.BlockSpec((D, tn), lambda i, j: (0, jnp.where(j < nj, j, nj + j))),
                  pl.BlockSpec((D, tn), lambda i, j: (0, jnp.minimum(nj + j, 2 * nj - 1)))],
        out_specs=pl.BlockSpec((tm, tn), lambda i, j: (i, j)),
        scratch_shapes=[pltpu.VMEM((tm, D), BF16)],
        compiler_params=_params(2),
        name="conv_proj",
    )(x, scale, shift, norm_g, w_bf16, w_bf16)


def _conv_core_kernel(g_ref, halo_ref, z_ref, w_ref, b_ref, lng_ref, lnb_ref, y_ref, ext_ref, cv_ref,
                      *, tc, tiles_per_seq):
    n_slabs = CONV_WIDTH // LANES
    first = (pl.program_id(0) % tiles_per_seq) == 0
    for c in range(n_slabs):
        cols = slice(c * LANES, (c + 1) * LANES)
        halo = halo_ref[:, cols].astype(F32)
        ext_ref[c, 0:HALO, :] = jnp.where(first, 0.0, halo)
        ext_ref[c, HALO:, :] = g_ref[:, cols].astype(F32)

    rc = 64
    lead = HALO - (CONV_K - 1)

    def slab(c, carry):
        for r0 in range(0, tc, rc):
            acc = jnp.broadcast_to(b_ref[c], (rc, LANES))
            for k in range(CONV_K):
                acc = acc + w_ref[c, k:k + 1, :] * ext_ref[c, r0 + lead + k:r0 + lead + k + rc, :]
            cv_ref[c, r0:r0 + rc, :] = acc
        return carry

    lax.fori_loop(0, n_slabs, slab, 0)

    inv_e = 1.0 / CONV_WIDTH
    for r0 in range(0, tc, rc):
        rows = slice(r0, r0 + rc)
        tot = cv_ref[0, rows, :]
        for c in range(1, n_slabs):
            tot = tot + cv_ref[c, rows, :]
        mu = jnp.sum(tot, axis=-1, keepdims=True) * inv_e
        sq = jnp.zeros((rc, LANES), F32)
        for c in range(n_slabs):
            dlt = cv_ref[c, rows, :] - mu
            sq = sq + dlt * dlt
        inv = lax.rsqrt(jnp.sum(sq, axis=-1, keepdims=True) * inv_e + NORM_EPS)
        for c in range(n_slabs):
            cols = slice(c * LANES, (c + 1) * LANES)
            t = (cv_ref[c, rows, :] - mu) * inv * lng_ref[:, cols] + lnb_ref[:, cols]
            z = z_ref[rows, cols].astype(F32)
            y_ref[rows, cols] = (_silu(t) * _silu(z)).astype(BF16)


def _conv_core(proj, seq, dw_w, dw_b, ln_g, ln_b, *, tc=256):
    M = proj.shape[0]
    E = CONV_WIDTH
    n_slabs = E // LANES
    per_halo = tc // HALO
    w_slabs = dw_w.reshape(CONV_K, n_slabs, LANES).transpose(1, 0, 2)
    return pl.pallas_call(
        functools.partial(_conv_core_kernel, tc=tc, tiles_per_seq=seq // tc),
        out_shape=jax.ShapeDtypeStruct((M, E), BF16),
        grid=(M // tc,),
        in_specs=[pl.BlockSpec((tc, E), lambda i: (i, 0)),
                  pl.BlockSpec((HALO, E), lambda i: (jnp.maximum(i * per_halo - 1, 0), 0)),
                  pl.BlockSpec((tc, E), lambda i: (i, 1)),
                  pl.BlockSpec((n_slabs, CONV_K, LANES), lambda i: (0, 0, 0)),
                  pl.BlockSpec((n_slabs, 1, LANES), lambda i: (0, 0, 0)),
                  pl.BlockSpec((1, E), lambda i: (0, 0)),
                  pl.BlockSpec((1, E), lambda i: (0, 0))],
        out_specs=pl.BlockSpec((tc, E), lambda i: (i, 0)),
        scratch_shapes=[pltpu.VMEM((n_slabs, tc + HALO, LANES), F32),
                        pltpu.VMEM((n_slabs, tc, LANES), F32)],
        compiler_params=_params(1),
        name="conv_core",
    )(proj, proj, proj, w_slabs, dw_b.reshape(n_slabs, 1, LANES), ln_g.reshape(1, E), ln_b.reshape(1, E))


def _rotate_half_matrix():
    half = ROPE_DIM // 2
    k = jnp.arange(HEAD_DIM)[:, None]
    j = jnp.arange(HEAD_DIM)[None, :]
    p = jnp.where((j < half) & (k == j + half), -1.0, 0.0) + jnp.where((j >= half) & (j < ROPE_DIM) & (k == j - half), 1.0, 0.0)
    return jnp.kron(jnp.eye(MXU_N // HEAD_DIM, dtype=F32), p).astype(BF16)


def _rope_tables(positions):
    B, S = positions.shape
    inv_freq = jnp.power(ROPE_THETA, -jnp.arange(0, ROPE_DIM, 2, dtype=F32) / ROPE_DIM)
    ang = positions.astype(F32)[..., None] * inv_freq
    cos, sin = jnp.cos(ang), jnp.sin(ang)
    rest = HEAD_DIM - ROPE_DIM
    cos_t = jnp.concatenate([cos, cos, jnp.ones((B, S, rest), F32)], axis=-1)
    sin_t = jnp.concatenate([sin, sin, jnp.zeros((B, S, rest), F32)], axis=-1)
    return cos_t, sin_t


def _gained_tables(rope, q_gain, k_gain):
    cos_t, sin_t = rope
    B, S, _ = cos_t.shape
    half = ROPE_DIM // 2
    gains = jnp.stack([q_gain, k_gain])
    lane = jnp.arange(HEAD_DIM)
    partner = jnp.where(lane < half, jnp.roll(gains, -half, axis=-1), jnp.roll(gains, half, axis=-1))
    stacked = (gains[:, None, None, :] * cos_t[None], partner[:, None, None, :] * sin_t[None])
    out = []
    for _, d in DIL_GROUPS:
        out.append(tuple(t.reshape(2, B, S // d, d, HEAD_DIM).transpose(0, 1, 3, 2, 4) for t in stacked))
    return out


def kernel(x, c, positions, ada_w, ada_b, norm_g, attn_w_in, attn_q_gain, attn_k_gain, attn_w_out,
           sgu_w_in, sgu_ln_g, sgu_ln_b, sgu_ws, sgu_bs, sgu_w_out,
           conv_w_in, conv_dw_w, conv_dw_b, conv_ln_g, conv_ln_b, conv_w_out):
    B, S, D = x.shape
    mod = _ada_modulation(c, ada_w, ada_b)
    rope = _rope_tables(positions)
    for i in range(DEPTH):
        shift = mod[i, :, None, :D]
        scale = mod[i, :, None, D:2 * D]
        gate = mod[i, :, None, 2 * D:]
        g_row = norm_g[i].reshape(1, D)
        kind, l = i % N_MIXERS, i // N_MIXERS
        if kind == 0:
            w_in = attn_w_in[l].astype(BF16)
            tables = _gained_tables(rope, attn_q_gain[l] * (HEAD_DIM ** -0.5), attn_k_gain[l])
            proj = [_attn_proj(x, scale, shift, g_row, w_in, tables[g], group=g, with_gate=(g == 0))
                    for g in range(N_DIL)]
            y = _attention(*proj).reshape(B * S, ATTN_OUT)
            x = _out_proj(y, attn_w_out[l].astype(BF16), x, gate)
        elif kind == 1:
            proj = _sgu_proj(x, scale, shift, g_row, sgu_w_in[l].astype(BF16))
            y = _sgu_core(proj, sgu_ln_g[l], sgu_ln_b[l], sgu_ws[l], sgu_bs[l])
            x = _out_proj(y, sgu_w_out[l].astype(BF16), x, gate)
        else:
            proj = _conv_proj(x, scale, shift, g_row, conv_w_in[l].astype(BF16))
            y = _conv_core(proj, S, conv_dw_w[l], conv_dw_b[l], conv_ln_g[l], conv_ln_b[l])
            x = _out_proj(y, conv_w_out[l].astype(BF16), x, gate)
    return x
```

```python
import functools

import jax
import jax.numpy as jnp
from jax import lax
from jax.experimental import pallas as pl
from jax.experimental.pallas import tpu as pltpu

D_MODEL = 2048
DEPTH = 4
N_MIXERS = 3
HEAD_DIM = 128
DIL_GROUPS = ((128, 1), (512, 4), (2048, 16))
N_DIL = len(DIL_GROUPS)
HEADS_PER_GROUP = D_MODEL // HEAD_DIM
ATTN_OUT = HEADS_PER_GROUP * HEAD_DIM
ATTN_QKV = N_DIL * ATTN_OUT
ROPE_DIM = HEAD_DIM // 4
ROPE_THETA = 500000.0
BLOCK = 128
SGU_WIDTH = 2 * D_MODEL
SGU_CHUNK = 128
SGU_GROUPS = 16
CONV_WIDTH = 2 * D_MODEL
CONV_K = 31
NORM_EPS = 1e-6

LANES = 128
MXU_N = 256
SUB = 4
HALO = 32
VMEM_LIMIT = 56 * 1024 * 1024
MASKED = -1e30

F32 = jnp.float32
BF16 = jnp.bfloat16


def _params(n_axes):
    return pltpu.CompilerParams(dimension_semantics=("arbitrary",) * n_axes,
                                vmem_limit_bytes=VMEM_LIMIT)


def _sigmoid(x):
    return 1.0 / (1.0 + jnp.exp(-x))


def _silu(x):
    return x * _sigmoid(x)


def _gelu_tanh(x):
    return x * (0.5 * (1.0 + jnp.tanh(0.7978845608028654 * (x + 0.044715 * (x * x * x)))))


def _pipelined_subtiles(n_sub, matmul, finish):
    for c in range(n_sub + 1):
        if c < n_sub:
            matmul(c, c % 2)
        if c >= 1:
            finish(c - 1, (c - 1) % 2)


def _ada_kernel(c_ref, w_ref, b_ref, o_ref):
    c = c_ref[...]
    o_ref[0] = jnp.dot(_silu(c), w_ref[0], preferred_element_type=F32,
                       precision=lax.Precision.HIGHEST) + b_ref[0]


def _ada_modulation(c, ada_w, ada_b):
    B, D = c.shape
    rows = 8
    tn = 1024
    c8 = jnp.pad(c, ((0, rows - B), (0, 0)))
    out = pl.pallas_call(
        _ada_kernel,
        out_shape=jax.ShapeDtypeStruct((DEPTH, rows, 3 * D), F32),
        grid=(DEPTH, 3 * D // tn),
        in_specs=[pl.BlockSpec((rows, D), lambda l, j: (0, 0)),
                  pl.BlockSpec((1, D, tn), lambda l, j: (l, 0, j)),
                  pl.BlockSpec((1, 1, tn), lambda l, j: (l, 0, j))],
        out_specs=pl.BlockSpec((1, rows, tn), lambda l, j: (l, 0, j)),
        compiler_params=_params(2),
        name="ada_modulation",
    )(c8, ada_w, ada_b.reshape(DEPTH, 1, 3 * D))
    return out[:, :B]


def _norm_prologue(x_ref, sc_ref, sh_ref, g_ref, h_ref, slab_ref, tmp_ref, *, d, tm):
    rc = 128
    a = g_ref[...] * (1.0 + sc_ref[0])
    sh = sh_ref[0]
    n_slabs = D_MODEL // LANES

    def body(i, carry):
        r0 = pl.multiple_of(i * rc, rc)
        x = x_ref[0, pl.ds(r0, rc), :]
        ms = jnp.mean(x * x, axis=-1, keepdims=True)
        h = x * lax.rsqrt(ms + NORM_EPS) * a + sh
        if d == 1:
            h_ref[pl.ds(r0, rc), :] = h.astype(BF16)
        else:
            for c in range(n_slabs):
                slab_ref[c, pl.ds(r0, rc), :] = h[:, c * LANES:(c + 1) * LANES]
        return carry

    lax.fori_loop(0, tm // rc, body, 0)
    if d == 1:
        return
    rows = tm // d
    for c in range(n_slabs):
        cols = slice(c * LANES, (c + 1) * LANES)
        if d <= SUB:
            for r in range(d):
                h_ref[r * rows:(r + 1) * rows, cols] = slab_ref[c, pl.ds(r, rows, stride=d), :].astype(BF16)
            continue
        part = tm // SUB
        for r_lo in range(SUB):
            tmp_ref[r_lo * part:(r_lo + 1) * part, :] = slab_ref[c, pl.ds(r_lo, part, stride=SUB), :]
        for r in range(d):
            src = pl.ds((r % SUB) * part + r // SUB, rows, stride=d // SUB)
            h_ref[r * rows:(r + 1) * rows, cols] = tmp_ref[src, :].astype(BF16)


def _attn_proj_kernel(x_ref, sc_ref, sh_ref, g_ref, w_ref, perm_ref, cos_ref, sin_ref,
                      o_ref, h_ref, acc_a, acc_b, part_ref, *extra, d, tm, tn, n_rope):
    j = pl.program_id(1)
    rows = tm // d
    rc = min(32, rows)
    per_res = rows // rc
    bufs = (acc_a, acc_b)

    @pl.when(j == 0)
    def _():
        slab_ref = extra[0] if d > 1 else None
        tmp_ref = extra[1] if d > SUB else None
        _norm_prologue(x_ref, sc_ref, sh_ref, g_ref, h_ref, slab_ref, tmp_ref, d=d, tm=tm)

    def matmul(c, slot):
        bufs[slot][...] = jnp.dot(h_ref[...], w_ref[:, c * MXU_N:(c + 1) * MXU_N],
                                  preferred_element_type=F32)

    def finish(c, slot, rope):
        acc_ref, c0 = bufs[slot], c * MXU_N
        if rope:
            part_ref[...] = jnp.dot(acc_ref[...].astype(BF16), perm_ref[...], preferred_element_type=F32)
        for ci in range(tm // rc):
            r, m0, p0 = ci // per_res, (ci % per_res) * rc, ci * rc
            if not rope:
                o_ref[0, r, m0:m0 + rc, c0:c0 + MXU_N] = acc_ref[p0:p0 + rc, :].astype(BF16)
                continue
            cos = cos_ref[0, 0, r, m0:m0 + rc, :]
            sin = sin_ref[0, 0, r, m0:m0 + rc, :]
            for hc in range(MXU_N // LANES):
                cols = slice(hc * LANES, (hc + 1) * LANES)
                a = acc_ref[p0:p0 + rc, cols]
                ms = jnp.sum(a * a, axis=-1, keepdims=True) * (1.0 / HEAD_DIM)
                rot = (a * cos + part_ref[p0:p0 + rc, cols] * sin) * lax.rsqrt(ms + NORM_EPS)
                o_ref[0, r, m0:m0 + rc, c0 + hc * LANES:c0 + (hc + 1) * LANES] = rot.astype(BF16)

    @pl.when(j < n_rope)
    def _():
        _pipelined_subtiles(tn // MXU_N, matmul, functools.partial(finish, rope=True))

    @pl.when(j >= n_rope)
    def _():
        _pipelined_subtiles(tn // MXU_N, matmul, functools.partial(finish, rope=False))


def _attn_proj(x, scale, shift, norm_g, w_bf16, tables, *, group, with_gate, tm=1024, tn=1024):
    B, S, D = x.shape
    d = DIL_GROUPS[group][1]
    n_sec = 4 if with_gate else 3
    per_sec = ATTN_OUT // tn
    tiles_m = S // tm

    def w_index(i, j):
        sec = j // per_sec
        base = jnp.where(sec < 3, sec * (ATTN_QKV // tn) + group * per_sec, 3 * (ATTN_QKV // tn))
        return (0, base + j % per_sec)

    tab_spec = pl.BlockSpec((1, 1, d, tm // d, LANES),
                            lambda i, j: (jnp.minimum(j // per_sec, 1), i // tiles_m, 0, i % tiles_m, 0))
    scratch = [pltpu.VMEM((tm, D), BF16)] + [pltpu.VMEM((tm, MXU_N), F32)] * 3
    if d > 1:
        scratch.append(pltpu.VMEM((D // LANES, tm, LANES), F32))
    if d > SUB:
        scratch.append(pltpu.VMEM((tm, LANES), F32))
    return pl.pallas_call(
        functools.partial(_attn_proj_kernel, d=d, tm=tm, tn=tn, n_rope=2 * per_sec),
        out_shape=jax.ShapeDtypeStruct((B, d, S // d, n_sec * ATTN_OUT), BF16),
        grid=(B * tiles_m, n_sec * per_sec),
        in_specs=[pl.BlockSpec((1, tm, D), lambda i, j: (i // tiles_m, i % tiles_m, 0)),
                  pl.BlockSpec((1, 1, D), lambda i, j: (i // tiles_m, 0, 0)),
                  pl.BlockSpec((1, 1, D), lambda i, j: (i // tiles_m, 0, 0)),
                  pl.BlockSpec((1, D), lambda i, j: (0, 0)),
                  pl.BlockSpec((D, tn), w_index),
                  pl.BlockSpec((MXU_N, MXU_N), lambda i, j: (0, 0)),
                  tab_spec, tab_spec],
        out_specs=pl.BlockSpec((1, d, tm // d, tn), lambda i, j: (i // tiles_m, 0, i % tiles_m, j)),
        scratch_shapes=scratch,
        compiler_params=_params(2),
        name=f"attn_proj_g{group}",
    )(x, scale, shift, norm_g, w_bf16, _rotate_half_matrix(), *tables)


def _attn_block(q, k, v, bias):
    s = lax.dot_general(q, k, (((1,), (1,)), ((), ())), preferred_element_type=F32) + bias
    m = jnp.max(s, axis=-1, keepdims=True)
    p = jnp.exp(s - m).astype(BF16)
    v_ext = jnp.concatenate([v, jnp.ones(v.shape, BF16)], axis=1)
    ol = jnp.dot(p, v_ext, preferred_element_type=F32)
    l = ol[:, HEAD_DIM:]
    return ol[:, :HEAD_DIM] * (1.0 / l), m + jnp.log(l)


def _attn_kernel(q0, k0, v0, z0, q1, k1, v1, q2, k2, v2, y_ref, o_scr, lse_scr, *, seq):
    dist = (lax.broadcasted_iota(jnp.int32, (BLOCK, 2 * BLOCK), 0) + BLOCK
            - lax.broadcasted_iota(jnp.int32, (BLOCK, 2 * BLOCK), 1))
    bias_both = jnp.where(dist >= 0, jnp.where(dist <= BLOCK, 0.0, MASKED), MASKED).astype(F32)
    bias_first = bias_both[:, BLOCK:]
    row = lax.broadcasted_iota(jnp.int32, (2 * BLOCK, 2 * BLOCK), 0)
    col = lax.broadcasted_iota(jnp.int32, (2 * BLOCK, 2 * BLOCK), 1)
    same_seq = (row // BLOCK) == (col // BLOCK)
    bias_pair = jnp.where(same_seq, jnp.where(col <= row, 0.0, MASKED), MASKED).astype(F32)

    groups = ((q0, k0, v0), (q1, k1, v1), (q2, k2, v2))
    for g, (q_ref, k_ref, v_ref) in enumerate(groups):
        d = DIL_GROUPS[g][1]
        nb = seq // d // BLOCK

        def store(r, n, o, lse, g=g, d=d):
            if d == 1:
                start = BLOCK * n
                rows = pl.ds(start if isinstance(start, int) else pl.multiple_of(start, BLOCK), BLOCK)
            elif d <= SUB:
                rows = pl.ds(r + d * BLOCK * n, BLOCK, stride=d)
            else:
                rows = pl.ds((r % SUB) * (seq // SUB) + r // SUB + (d // SUB) * BLOCK * n, BLOCK,
                             stride=d // SUB)
            slot = g if d <= SUB else N_DIL
            o_scr[slot, rows, :] = o
            lse_scr[slot, rows, :] = lse

        def first_block(r, q_ref=q_ref, k_ref=k_ref, v_ref=v_ref, store=store):
            o, lse = _attn_block(q_ref[0, r, 0:BLOCK, :], k_ref[0, r, 0:BLOCK, :],
                                 v_ref[0, r, 0:BLOCK, :], bias_first)
            store(r, 0, o, lse)

        def later_block(r, n, q_ref=q_ref, k_ref=k_ref, v_ref=v_ref, store=store):
            if isinstance(n, int):
                q_rows, kv_rows = pl.ds(n * BLOCK, BLOCK), pl.ds((n - 1) * BLOCK, 2 * BLOCK)
            else:
                q_rows = pl.ds(pl.multiple_of(n * BLOCK, BLOCK), BLOCK)
                kv_rows = pl.ds(pl.multiple_of((n - 1) * BLOCK, BLOCK), 2 * BLOCK)
            o, lse = _attn_block(q_ref[0, r, q_rows, :], k_ref[0, r, kv_rows, :], v_ref[0, r, kv_rows, :],
                                 bias_both)
            store(r, n, o, lse)

        def block_pair(i, carry, q_ref=q_ref, k_ref=k_ref, v_ref=v_ref, store=store):
            def both(ref):
                return jnp.concatenate([ref[0, 2 * i], ref[0, 2 * i + 1]], axis=0)

            o, lse = _attn_block(both(q_ref), both(k_ref), both(v_ref), bias_pair)
            store(2 * i, 0, o[:BLOCK], lse[:BLOCK])
            store(2 * i + 1, 0, o[BLOCK:], lse[BLOCK:])
            return carry

        if d == 1:
            first_block(0)

            def blocks(n, carry, later_block=later_block):
                later_block(0, n)
                return carry

            lax.fori_loop(1, nb, blocks, 0, unroll=5)
        elif nb > 1:
            def residue(r, carry, nb=nb, first_block=first_block, later_block=later_block):
                first_block(r)
                for n in range(1, nb):
                    later_block(r, n)
                return carry

            lax.fori_loop(0, d, residue, 0, unroll=2)
        else:
            lax.fori_loop(0, d // 2, block_pair, 0, unroll=4)

        if d > SUB:
            part = seq // SUB
            for r_lo in range(SUB):
                for scr in (o_scr, lse_scr):
                    scr[g, pl.ds(r_lo, part, stride=SUB), :] = scr[N_DIL, r_lo * part:(r_lo + 1) * part, :]

    rc = 256

    def combine(i, carry):
        rows = pl.ds(pl.multiple_of(i * rc, rc), rc)
        l0, l1, l2 = lse_scr[0, rows, :], lse_scr[1, rows, :], lse_scr[2, rows, :]
        m = jnp.maximum(jnp.maximum(l0, l1), l2)
        e0, e1, e2 = jnp.exp(l0 - m), jnp.exp(l1 - m), jnp.exp(l2 - m)
        o = (e0 * o_scr[0, rows, :] + e1 * o_scr[1, rows, :] + e2 * o_scr[2, rows, :]) / (e0 + e1 + e2)
        z = z0[0, 0, rows, :].astype(F32)
        y_ref[0, rows, :] = (o * _silu(z)).astype(BF16)
        return carry

    lax.fori_loop(0, seq // rc, combine, 0)


def _attention(p0, p1, p2):
    B, _, S, _ = p0.shape
    H = HEADS_PER_GROUP

    def spec(d, sec):
        return pl.BlockSpec((1, d, S // d, LANES), lambda b, h: (b, 0, 0, sec * H + h))

    in_specs = [spec(1, 0), spec(1, 1), spec(1, 2), spec(1, 3)]
    for d in (DIL_GROUPS[1][1], DIL_GROUPS[2][1]):
        in_specs += [spec(d, 0), spec(d, 1), spec(d, 2)]
    return pl.pallas_call(
        functools.partial(_attn_kernel, seq=S),
        out_shape=jax.ShapeDtypeStruct((B, S, ATTN_OUT), BF16),
        grid=(B, H),
        in_specs=in_specs,
        out_specs=pl.BlockSpec((1, S, LANES), lambda b, h: (b, 0, h)),
        scratch_shapes=[pltpu.VMEM((N_DIL + 1, S, LANES), F32), pltpu.VMEM((N_DIL + 1, S, LANES), F32)],
        compiler_params=_params(2),
        name="dilated_attention",
    )(p0, p0, p0, p0, p1, p1, p1, p2, p2, p2)


def _out_proj_kernel(y_ref, w_ref, x_ref, gate_ref, o_ref, acc_a, acc_b):
    bufs = (acc_a, acc_b)

    def matmul(c, slot):
        bufs[slot][...] = jnp.dot(y_ref[...], w_ref[:, c * MXU_N:(c + 1) * MXU_N],
                                  preferred_element_type=F32)

    def finish(c, slot):
        cols = slice(c * MXU_N, (c + 1) * MXU_N)
        o_ref[:, cols] = x_ref[:, cols] + gate_ref[0][:, cols] * bufs[slot][...]

    _pipelined_subtiles(o_ref.shape[1] // MXU_N, matmul, finish)


def _out_proj(y, w_bf16, x, gate, *, tm=512):
    B, S, D = x.shape
    M, K = y.shape
    tiles_m = S // tm
    out = pl.pallas_call(
        _out_proj_kernel,
        out_shape=jax.ShapeDtypeStruct((M, D), F32),
        grid=(M // tm,),
        in_specs=[pl.BlockSpec((tm, K), lambda i: (i, 0)),
                  pl.BlockSpec((K, D), lambda i: (0, 0), pipeline_mode=pl.Buffered(1)),
                  pl.BlockSpec((tm, D), lambda i: (i, 0)),
                  pl.BlockSpec((1, 1, D), lambda i: (i // tiles_m, 0, 0))],
        out_specs=pl.BlockSpec((tm, D), lambda i: (i, 0)),
        scratch_shapes=[pltpu.VMEM((tm, MXU_N), F32)] * 2,
        compiler_params=_params(1),
        name="out_proj",
    )(y, w_bf16, x.reshape(M, D), gate)
    return out.reshape(B, S, D)


def _sgu_proj_kernel(x_ref, sc_ref, sh_ref, g_ref, w_ref, o_ref, h_ref, acc_a, acc_b, *, tm, tn, n_gelu):
    j = pl.program_id(1)
    bufs = (acc_a, acc_b)

    @pl.when(j == 0)
    def _():
        _norm_prologue(x_ref, sc_ref, sh_ref, g_ref, h_ref, None, None, d=1, tm=tm)

    def matmul(c, slot):
        bufs[slot][...] = jnp.dot(h_ref[...], w_ref[:, c * MXU_N:(c + 1) * MXU_N],
                                  preferred_element_type=F32)

    def finish(c, slot, gelu):
        a = bufs[slot][...]
        o_ref[:, c * MXU_N:(c + 1) * MXU_N] = (_gelu_tanh(a) if gelu else a).astype(BF16)

    @pl.when(j < n_gelu)
    def _():
        _pipelined_subtiles(tn // MXU_N, matmul, functools.partial(finish, gelu=True))

    @pl.when(j >= n_gelu)
    def _():
        _pipelined_subtiles(tn // MXU_N, matmul, functools.partial(finish, gelu=False))


def _sgu_proj(x, scale, shift, norm_g, w_bf16, *, tm=1024, tn=1024):
    B, S, D = x.shape
    N = w_bf16.shape[1]
    tiles_m = S // tm
    return pl.pallas_call(
        functools.partial(_sgu_proj_kernel, tm=tm, tn=tn, n_gelu=2 * SGU_WIDTH // tn),
        out_shape=jax.ShapeDtypeStruct((B * S, N), BF16),
        grid=(B * tiles_m, N // tn),
        in_specs=[pl.BlockSpec((1, tm, D), lambda i, j: (i // tiles_m, i % tiles_m, 0)),
                  pl.BlockSpec((1, 1, D), lambda i, j: (i // tiles_m, 0, 0)),
                  pl.BlockSpec((1, 1, D), lambda i, j: (i // tiles_m, 0, 0)),
                  pl.BlockSpec((1, D), lambda i, j: (0, 0)),
                  pl.BlockSpec((D, tn), lambda i, j: (0, j))],
        out_specs=pl.BlockSpec((tm, tn), lambda i, j: (i, j)),
        scratch_shapes=[pltpu.VMEM((tm, D), BF16)] + [pltpu.VMEM((tm, MXU_N), F32)] * 2,
        compiler_params=_params(2),
        name="sgu_proj",
    )(x, scale, shift, norm_g, w_bf16)


def _sgu_core_kernel(u_ref, v_ref, z_ref, lng_ref, lnb_ref, ws_ref, bst_ref, y_ref, wm_ref, vn_ref, *, tc):
    t_out = lax.broadcasted_iota(jnp.int32, (SGU_CHUNK, SGU_CHUNK), 0)
    s_in = lax.broadcasted_iota(jnp.int32, (SGU_CHUNK, SGU_CHUNK), 1)
    causal = s_in <= t_out
    for g in range(SGU_GROUPS):
        wm_ref[g] = jnp.where(causal, ws_ref[g], 0.0).astype(BF16)
    gw = SGU_WIDTH // SGU_GROUPS

    def chunk(ci, carry):
        rows = pl.ds(pl.multiple_of(ci * SGU_CHUNK, SGU_CHUNK), SGU_CHUNK)
        v = v_ref[rows, :].astype(F32)
        mu = jnp.mean(v, axis=-1, keepdims=True)
        vc = v - mu
        var = jnp.mean(vc * vc, axis=-1, keepdims=True)
        vn_ref[...] = (vc * lax.rsqrt(var + NORM_EPS) * lng_ref[...] + lnb_ref[...]).astype(BF16)
        for g in range(SGU_GROUPS):
            cols = slice(g * gw, (g + 1) * gw)
            sv = jnp.dot(wm_ref[g], vn_ref[:, cols], preferred_element_type=F32) + bst_ref[:, g:g + 1]
            z = z_ref[rows, cols].astype(F32)
            y_ref[rows, cols] = (u_ref[rows, cols].astype(F32) * sv * _silu(z)).astype(BF16)
        return carry

    lax.fori_loop(0, tc // SGU_CHUNK, chunk, 0)


def _sgu_core(proj, ln_g, ln_b, ws, bs, *, tc=256):
    M = proj.shape[0]
    E = SGU_WIDTH
    return pl.pallas_call(
        functools.partial(_sgu_core_kernel, tc=tc),
        out_shape=jax.ShapeDtypeStruct((M, E), BF16),
        grid=(M // tc,),
        in_specs=[pl.BlockSpec((tc, E), lambda i: (i, 0)),
                  pl.BlockSpec((tc, E), lambda i: (i, 1)),
                  pl.BlockSpec((tc, E), lambda i: (i, 2)),
                  pl.BlockSpec((1, E), lambda i: (0, 0)),
                  pl.BlockSpec((1, E), lambda i: (0, 0)),
                  pl.BlockSpec((SGU_GROUPS, SGU_CHUNK, SGU_CHUNK), lambda i: (0, 0, 0)),
                  pl.BlockSpec((SGU_CHUNK, SGU_GROUPS), lambda i: (0, 0))],
        out_specs=pl.BlockSpec((tc, E), lambda i: (i, 0)),
        scratch_shapes=[pltpu.VMEM((SGU_GROUPS, SGU_CHUNK, SGU_CHUNK), BF16),
                        pltpu.VMEM((SGU_CHUNK, E), BF16)],
        compiler_params=_params(1),
        name="sgu_core",
    )(proj, proj, proj, ln_g.reshape(1, E), ln_b.reshape(1, E), ws, bs.T)


def _conv_proj_kernel(x_ref, sc_ref, sh_ref, g_ref, wa_ref, wb_ref, o_ref, h_ref,
                      acc_a, acc_b, gate_a, gate_b, *, tm, tn, n_glu):
    j = pl.program_id(1)
    accs, gates = (acc_a, acc_b), (gate_a, gate_b)

    @pl.when(j == 0)
    def _():
        _norm_prologue(x_ref, sc_ref, sh_ref, g_ref, h_ref, None, None, d=1, tm=tm)

    def matmul(c, slot, glu):
        cols = slice(c * MXU_N, (c + 1) * MXU_N)
        accs[slot][...] = jnp.dot(h_ref[...], wa_ref[:, cols], preferred_element_type=F32)
        if glu:
            gates[slot][...] = jnp.dot(h_ref[...], wb_ref[:, cols], preferred_element_type=F32)

    def finish(c, slot, glu):
        a = accs[slot][...]
        if glu:
            a = a * _sigmoid(gates[slot][...])
        o_ref[:, c * MXU_N:(c + 1) * MXU_N] = a.astype(BF16)

    @pl.when(j < n_glu)
    def _():
        _pipelined_subtiles(tn // MXU_N, functools.partial(matmul, glu=True), functools.partial(finish, glu=True))

    @pl.when(j >= n_glu)
    def _():
        _pipelined_subtiles(tn // MXU_N, functools.partial(matmul, glu=False), functools.partial(finish, glu=False))


def _conv_proj(x, scale, shift, norm_g, w_bf16, *, tm=1024, tn=1024):
    B, S, D = x.shape
    E = CONV_WIDTH
    nj = E // tn
    tiles_m = S // tm
    return pl.pallas_call(
        functools.partial(_conv_proj_kernel, tm=tm, tn=tn, n_glu=nj),
        out_shape=jax.ShapeDtypeStruct((B * S, 2 * E), BF16),
        grid=(B * tiles_m, 2 * nj),
        in_specs=[pl.BlockSpec((1, tm, D), lambda i, j: (i // tiles_m, i % tiles_m, 0)),
                  pl.BlockSpec((1, 1, D), lambda i, j: (i // tiles_m, 0, 0)),
                  pl.BlockSpec((1, 1, D), lambda i, j: (i // tiles_m, 0, 0)),
                  pl.BlockSpec((1, D), lambda i, j: (0, 0)),
                  pl.BlockSpec((D, tn), lambda i, j: (0, jnp.where(j < nj, j, nj + j))),
                  pl.BlockSpec((D, tn), lambda i, j: (0, jnp.minimum(nj + j, 2 * nj - 1)))],
        out_specs=pl.BlockSpec((tm, tn), lambda i, j: (i, j)),
        scratch_shapes=[pltpu.VMEM((tm, D), BF16)] + [pltpu.VMEM((tm, MXU_N), F32)] * 4,
        compiler_params=_params(2),
        name="conv_proj",
    )(x, scale, shift, norm_g, w_bf16, w_bf16)


def _conv_core_kernel(g_ref, halo_ref, z_ref, w_ref, b_ref, lng_ref, lnb_ref, y_ref, ext_ref, cv_ref,
                      *, tc, tiles_per_seq):
    n_slabs = CONV_WIDTH // LANES
    first = (pl.program_id(0) % tiles_per_seq) == 0
    for c in range(n_slabs):
        cols = slice(c * LANES, (c + 1) * LANES)
        halo = halo_ref[:, cols].astype(F32)
        ext_ref[c, 0:HALO, :] = jnp.where(first, 0.0, halo)
        ext_ref[c, HALO:, :] = g_ref[:, cols].astype(F32)

    rc = 64
    lead = HALO - (CONV_K - 1)

    def slab(c, carry):
        for r0 in range(0, tc, rc):
            acc = jnp.broadcast_to(b_ref[c], (rc, LANES))
            for k in range(CONV_K):
                acc = acc + w_ref[c, k:k + 1, :] * ext_ref[c, r0 + lead + k:r0 + lead + k + rc, :]
            cv_ref[c, r0:r0 + rc, :] = acc
        return carry

    lax.fori_loop(0, n_slabs, slab, 0)

    inv_e = 1.0 / CONV_WIDTH
    for r0 in range(0, tc, rc):
        rows = slice(r0, r0 + rc)
        tot = cv_ref[0, rows, :]
        for c in range(1, n_slabs):
            tot = tot + cv_ref[c, rows, :]
        mu = jnp.sum(tot, axis=-1, keepdims=True) * inv_e
        sq = jnp.zeros((rc, LANES), F32)
        for c in range(n_slabs):
            dlt = cv_ref[c, rows, :] - mu
            sq = sq + dlt * dlt
        inv = lax.rsqrt(jnp.sum(sq, axis=-1, keepdims=True) * inv_e + NORM_EPS)
        for c in range(n_slabs):
            cols = slice(c * LANES, (c + 1) * LANES)
            t = (cv_ref[c, rows, :] - mu) * inv * lng_ref[:, cols] + lnb_ref[:, cols]
            z = z_ref[rows, cols].astype(F32)
            y_ref[rows, cols] = (_silu(t) * _silu(z)).astype(BF16)


def _conv_core(proj, seq, dw_w, dw_b, ln_g, ln_b, *, tc=256):
    M = proj.shape[0]
    E = CONV_WIDTH
    n_slabs = E // LANES
    per_halo = tc // HALO
    w_slabs = dw_w.reshape(CONV_K, n_slabs, LANES).transpose(1, 0, 2)
    return pl.pallas_call(
        functools.partial(_conv_core_kernel, tc=tc, tiles_per_seq=seq // tc),
        out_shape=jax.ShapeDtypeStruct((M, E), BF16),
        grid=(M // tc,),
        in_specs=[pl.BlockSpec((tc, E), lambda i: (i, 0)),
                  pl.BlockSpec((HALO, E), lambda i: (jnp.maximum(i * per_halo - 1, 0), 0)),
                  pl.BlockSpec((tc, E), lambda i: (i, 1)),
                  pl.BlockSpec((n_slabs, CONV_K, LANES), lambda i: (0, 0, 0)),
                  pl.BlockSpec((n_slabs, 1, LANES), lambda i: (0, 0, 0)),
                  pl.BlockSpec((1, E), lambda i: (0, 0)),
                  pl.BlockSpec((1, E), lambda i: (0, 0))],
        out_specs=pl.BlockSpec((tc, E), lambda i: (i, 0)),
        scratch_shapes=[pltpu.VMEM((n_slabs, tc + HALO, LANES), F32),
                        pltpu.VMEM((n_slabs, tc, LANES), F32)],
        compiler_params=_params(1),
        name="conv_core",
    )(proj, proj, proj, w_slabs, dw_b.reshape(n_slabs, 1, LANES), ln_g.reshape(1, E), ln_b.reshape(1, E))


def _rotate_half_matrix():
    half = ROPE_DIM // 2
    k = jnp.arange(HEAD_DIM)[:, None]
    j = jnp.arange(HEAD_DIM)[None, :]
    p = (jnp.where((j < half) & (k == j + half), -1.0, 0.0)
         + jnp.where((j >= half) & (j < ROPE_DIM) & (k == j - half), 1.0, 0.0))
    return jnp.kron(jnp.eye(MXU_N // HEAD_DIM, dtype=F32), p).astype(BF16)


def _rope_tables(positions):
    B, S = positions.shape
    inv_freq = jnp.power(ROPE_THETA, -jnp.arange(0, ROPE_DIM, 2, dtype=F32) / ROPE_DIM)
    ang = positions.astype(F32)[..., None] * inv_freq
    cos, sin = jnp.cos(ang), jnp.sin(ang)
    rest = HEAD_DIM - ROPE_DIM
    cos_t = jnp.concatenate([cos, cos, jnp.ones((B, S, rest), F32)], axis=-1)
    sin_t = jnp.concatenate([sin, sin, jnp.zeros((B, S, rest), F32)], axis=-1)
    return cos_t, sin_t


def _gained_tables(rope, q_gain, k_gain):
    cos_t, sin_t = rope
    B, S, _ = cos_t.shape
    half = ROPE_DIM // 2
    gains = jnp.stack([q_gain, k_gain])
    lane = jnp.arange(HEAD_DIM)
    partner = jnp.where(lane < half, jnp.roll(gains, -half, axis=-1), jnp.roll(gains, half, axis=-1))
    stacked = (gains[:, None, None, :] * cos_t[None], partner[:, None, None, :] * sin_t[None])
    out = []
    for _, d in DIL_GROUPS:
        out.append(tuple(t.reshape(2, B, S // d, d, HEAD_DIM).transpose(0, 1, 3, 2, 4) for t in stacked))
    return out


def kernel(x, c, positions, ada_w, ada_b, norm_g, attn_w_in, attn_q_gain, attn_k_gain, attn_w_out,
           sgu_w_in, sgu_ln_g, sgu_ln_b, sgu_ws, sgu_bs, sgu_w_out,
           conv_w_in, conv_dw_w, conv_dw_b, conv_ln_g, conv_ln_b, conv_w_out):
    B, S, D = x.shape
    mod = _ada_modulation(c, ada_w, ada_b)
    rope = _rope_tables(positions)
    for i in range(DEPTH):
        shift = mod[i, :, None, :D]
        scale = mod[i, :, None, D:2 * D]
        gate = mod[i, :, None, 2 * D:]
        g_row = norm_g[i].reshape(1, D)
        kind, l = i % N_MIXERS, i // N_MIXERS
        if kind == 0:
            w_in = attn_w_in[l].astype(BF16)
            tables = _gained_tables(rope, attn_q_gain[l] * (HEAD_DIM ** -0.5), attn_k_gain[l])
            proj = [_attn_proj(x, scale, shift, g_row, w_in, tables[g], group=g, with_gate=(g == 0))
                    for g in range(N_DIL)]
            y = _attention(*proj).reshape(B * S, ATTN_OUT)
            x = _out_proj(y, attn_w_out[l].astype(BF16), x, gate)
        elif kind == 1:
            proj = _sgu_proj(x, scale, shift, g_row, sgu_w_in[l].astype(BF16))
            y = _sgu_core(proj, sgu_ln_g[l], sgu_ln_b[l], sgu_ws[l], sgu_bs[l])
            x = _out_proj(y, sgu_w_out[l].astype(BF16), x, gate)
        else:
            proj = _conv_proj(x, scale, shift, g_row, conv_w_in[l].astype(BF16))
            y = _conv_core(proj, S, conv_dw_w[l], conv_dw_b[l], conv_ln_g[l], conv_ln_b[l])
            x = _out_proj(y, conv_w_out[l].astype(BF16), x, gate)
    return x
```

```python
import functools

import jax
import jax.numpy as jnp
from jax import lax
from jax.experimental import pallas as pl
from jax.experimental.pallas import tpu as pltpu

D_MODEL = 2048
DEPTH = 4
N_MIXERS = 3
HEAD_DIM = 128
DIL_GROUPS = ((128, 1), (512, 4), (2048, 16))
N_DIL = len(DIL_GROUPS)
HEADS_PER_GROUP = D_MODEL // HEAD_DIM
ATTN_OUT = HEADS_PER_GROUP * HEAD_DIM
ATTN_QKV = N_DIL * ATTN_OUT
ROPE_DIM = HEAD_DIM // 4
ROPE_THETA = 500000.0
BLOCK = 128
SGU_WIDTH = 2 * D_MODEL
SGU_CHUNK = 128
SGU_GROUPS = 16
CONV_WIDTH = 2 * D_MODEL
CONV_K = 31
NORM_EPS = 1e-6

LANES = 128
MXU_N = 256
SUB = 4
HALO = 32
VMEM_LIMIT = 56 * 1024 * 1024
MASKED = -1e30

F32 = jnp.float32
BF16 = jnp.bfloat16


def _params(n_axes):
    return pltpu.CompilerParams(dimension_semantics=("arbitrary",) * n_axes,
                                vmem_limit_bytes=VMEM_LIMIT)


def _sigmoid(x):
    return 1.0 / (1.0 + jnp.exp(-x))


def _silu(x):
    return x * _sigmoid(x)


def _gelu_tanh(x):
    return x * (0.5 * (1.0 + jnp.tanh(0.7978845608028654 * (x + 0.044715 * (x * x * x)))))


def _pipelined_subtiles(n_sub, matmul, finish):
    for c in range(n_sub + 1):
        if c < n_sub:
            matmul(c, c % 2)
        if c >= 1:
            finish(c - 1, (c - 1) % 2)


def _split_bf16(v):
    hi = v.astype(BF16)
    return hi, (v - hi.astype(F32)).astype(BF16)


def _ada_kernel(c_ref, w_ref, b_ref, o_ref):
    c_hi, c_lo = _split_bf16(_silu(c_ref[...]))
    w_hi, w_lo = _split_bf16(w_ref[0])
    rows = c_hi.shape[0]
    both = jnp.dot(jnp.concatenate([c_hi, c_lo], axis=0), w_hi, preferred_element_type=F32)
    o_ref[0] = (both[:rows] + both[rows:] + jnp.dot(c_hi, w_lo, preferred_element_type=F32)
                + b_ref[0])


def _ada_modulation(c, ada_w, ada_b):
    B, D = c.shape
    rows = 16
    tn = 1024
    c8 = jnp.pad(c, ((0, rows - B), (0, 0)))
    out = pl.pallas_call(
        _ada_kernel,
        out_shape=jax.ShapeDtypeStruct((DEPTH, rows, 3 * D), F32),
        grid=(DEPTH, 3 * D // tn),
        in_specs=[pl.BlockSpec((rows, D), lambda l, j: (0, 0)),
                  pl.BlockSpec((1, D, tn), lambda l, j: (l, 0, j)),
                  pl.BlockSpec((1, 1, tn), lambda l, j: (l, 0, j))],
        out_specs=pl.BlockSpec((1, rows, tn), lambda l, j: (l, 0, j)),
        compiler_params=_params(2),
        name="ada_modulation",
    )(c8, ada_w, ada_b.reshape(DEPTH, 1, 3 * D))
    return out[:, :B]


def _norm_prologue(x_ref, sc_ref, sh_ref, g_ref, h_ref, work_ref, *, d, tm):
    rc = 128
    a = g_ref[...] * (1.0 + sc_ref[0])
    sh = sh_ref[0]

    def chunk_rows(i):
        return pl.ds(pl.multiple_of(i * rc, rc), rc)

    def inv_rms(x):
        return lax.rsqrt(jnp.mean(x * x, axis=-1, keepdims=True) + NORM_EPS)

    if d == 1:
        def body(i, carry):
            x = x_ref[0, chunk_rows(i), :]
            h_ref[chunk_rows(i), :] = (x * inv_rms(x) * a + sh).astype(BF16)
            return carry

        lax.fori_loop(0, tm // rc, body, 0)
        return

    INV, NAT, TMP = 0, 1, 2

    def stats(i, carry):
        work_ref[INV, chunk_rows(i), :] = jnp.broadcast_to(inv_rms(x_ref[0, chunk_rows(i), :]), (rc, LANES))
        return carry

    lax.fori_loop(0, tm // rc, stats, 0)
    rows = tm // d
    for c in range(D_MODEL // LANES):
        cols = slice(c * LANES, (c + 1) * LANES)

        work_ref[NAT] = x_ref[0, :, cols] * work_ref[INV] * a[:, cols] + sh[:, cols]
        if d <= SUB:
            for r in range(d):
                h_ref[r * rows:(r + 1) * rows, cols] = work_ref[NAT, pl.ds(r, rows, stride=d), :].astype(BF16)
            continue
        part = tm // SUB
        for r_lo in range(SUB):
            work_ref[TMP, r_lo * part:(r_lo + 1) * part, :] = work_ref[NAT, pl.ds(r_lo, part, stride=SUB), :]
        for r in range(d):
            src = pl.ds((r % SUB) * part + r // SUB, rows, stride=d // SUB)
            h_ref[r * rows:(r + 1) * rows, cols] = work_ref[TMP, src, :].astype(BF16)


def _attn_proj_kernel(x_ref, sc_ref, sh_ref, g_ref, w_ref, perm_ref, cos_ref, sin_ref,
                      o_ref, h_ref, acc_a, acc_b, part_ref, *work, d, tm, tn, n_rope):
    j = pl.program_id(1)
    rows = tm // d
    rc = min(32, rows)
    per_res = rows // rc
    bufs = (acc_a, acc_b)

    @pl.when(j == 0)
    def _():
        _norm_prologue(x_ref, sc_ref, sh_ref, g_ref, h_ref, work[0] if work else None, d=d, tm=tm)

    def matmul(c, slot):
        bufs[slot][...] = jnp.dot(h_ref[...], w_ref[:, c * MXU_N:(c + 1) * MXU_N].astype(BF16),
                                  preferred_element_type=F32)

    def finish(c, slot, rope):
        acc_ref, c0 = bufs[slot], c * MXU_N
        if rope:
            part_ref[...] = jnp.dot(acc_ref[...].astype(BF16), perm_ref[...], preferred_element_type=F32)
        for ci in range(tm // rc):
            r, m0, p0 = ci // per_res, (ci % per_res) * rc, ci * rc
            if not rope:
                o_ref[0, r, m0:m0 + rc, c0:c0 + MXU_N] = acc_ref[p0:p0 + rc, :].astype(BF16)
                continue
            cos = cos_ref[0, 0, r, m0:m0 + rc, :]
            sin = sin_ref[0, 0, r, m0:m0 + rc, :]
            for hc in range(MXU_N // LANES):
                cols = slice(hc * LANES, (hc + 1) * LANES)
                a = acc_ref[p0:p0 + rc, cols]
                ms = jnp.sum(a * a, axis=-1, keepdims=True) * (1.0 / HEAD_DIM)
                rot = (a * cos + part_ref[p0:p0 + rc, cols] * sin) * lax.rsqrt(ms + NORM_EPS)
                o_ref[0, r, m0:m0 + rc, c0 + hc * LANES:c0 + (hc + 1) * LANES] = rot.astype(BF16)

    @pl.when(j < n_rope)
    def _():
        _pipelined_subtiles(tn // MXU_N, matmul, functools.partial(finish, rope=True))

    @pl.when(j >= n_rope)
    def _():
        _pipelined_subtiles(tn // MXU_N, matmul, functools.partial(finish, rope=False))


def _attn_proj(x, scale, shift, norm_g, w_in, layer, tables, *, group, with_gate, tm=1024, tn=1024):
    B, S, D = x.shape
    d = DIL_GROUPS[group][1]
    n_sec = 4 if with_gate else 3
    per_sec = ATTN_OUT // tn
    tiles_m = S // tm

    def w_index(i, j):
        sec = j // per_sec
        base = jnp.where(sec < 3, sec * (ATTN_QKV // tn) + group * per_sec, 3 * (ATTN_QKV // tn))
        return (layer, 0, base + j % per_sec)

    tab_spec = pl.BlockSpec((1, 1, d, tm // d, LANES),
                            lambda i, j: (jnp.minimum(j // per_sec, 1), i // tiles_m, 0, i % tiles_m, 0))
    scratch = [pltpu.VMEM((tm, D), BF16)] + [pltpu.VMEM((tm, MXU_N), F32)] * 3
    if d > 1:
        scratch.append(pltpu.VMEM((3, tm, LANES), F32))
    return pl.pallas_call(
        functools.partial(_attn_proj_kernel, d=d, tm=tm, tn=tn, n_rope=2 * per_sec),
        out_shape=jax.ShapeDtypeStruct((B, d, S // d, n_sec * ATTN_OUT), BF16),
        grid=(B * tiles_m, n_sec * per_sec),
        in_specs=[pl.BlockSpec((1, tm, D), lambda i, j: (i // tiles_m, i % tiles_m, 0)),
                  pl.BlockSpec((1, 1, D), lambda i, j: (i // tiles_m, 0, 0)),
                  pl.BlockSpec((1, 1, D), lambda i, j: (i // tiles_m, 0, 0)),
                  pl.BlockSpec((1, D), lambda i, j: (0, 0)),
                  pl.BlockSpec((None, D, tn), w_index),
                  pl.BlockSpec((MXU_N, MXU_N), lambda i, j: (0, 0)),
                  tab_spec, tab_spec],
        out_specs=pl.BlockSpec((1, d, tm // d, tn), lambda i, j: (i // tiles_m, 0, i % tiles_m, j)),
        scratch_shapes=scratch,
        compiler_params=_params(2),
        name=f"attn_proj_g{group}",
    )(x, scale, shift, norm_g, w_in, _rotate_half_matrix(), *tables)


def _attn_block(q, k, v, bias):
    s = lax.dot_general(q, k, (((1,), (1,)), ((), ())), preferred_element_type=F32) + bias
    m = jnp.max(s, axis=-1, keepdims=True)
    p = jnp.exp(s - m).astype(BF16)
    v_ext = jnp.concatenate([v, jnp.ones(v.shape, BF16)], axis=1)
    ol = jnp.dot(p, v_ext, preferred_element_type=F32)
    l = ol[:, HEAD_DIM:]
    return ol[:, :HEAD_DIM] * (1.0 / l), m + jnp.log(l)


def _attn_kernel(q0, k0, v0, z0, q1, k1, v1, q2, k2, v2, y_ref, o_scr, lse_scr, *, seq):
    dist = (lax.broadcasted_iota(jnp.int32, (BLOCK, 2 * BLOCK), 0) + BLOCK
            - lax.broadcasted_iota(jnp.int32, (BLOCK, 2 * BLOCK), 1))
    bias_both = jnp.where(dist >= 0, jnp.where(dist <= BLOCK, 0.0, MASKED), MASKED).astype(F32)
    bias_first = bias_both[:, BLOCK:]
    row = lax.broadcasted_iota(jnp.int32, (2 * BLOCK, 2 * BLOCK), 0)
    col = lax.broadcasted_iota(jnp.int32, (2 * BLOCK, 2 * BLOCK), 1)
    same_seq = (row // BLOCK) == (col // BLOCK)
    bias_pair = jnp.where(same_seq, jnp.where(col <= row, 0.0, MASKED), MASKED).astype(F32)

    groups = ((q0, k0, v0), (q1, k1, v1), (q2, k2, v2))
    for g, (q_ref, k_ref, v_ref) in enumerate(groups):
        d = DIL_GROUPS[g][1]
        nb = seq // d // BLOCK

        def store(r, n, o, lse, g=g, d=d):
            if d == 1:
                start = BLOCK * n
                rows = pl.ds(start if isinstance(start, int) else pl.multiple_of(start, BLOCK), BLOCK)
            elif d <= SUB:
                rows = pl.ds(r + d * BLOCK * n, BLOCK, stride=d)
            else:
                rows = pl.ds((r % SUB) * (seq // SUB) + r // SUB + (d // SUB) * BLOCK * n, BLOCK,
                             stride=d // SUB)
            slot = g if d <= SUB else N_DIL
            o_scr[slot, rows, :] = o
            lse_scr[slot, rows, :] = lse

        def first_block(r, q_ref=q_ref, k_ref=k_ref, v_ref=v_ref, store=store):
            o, lse = _attn_block(q_ref[0, r, 0:BLOCK, :], k_ref[0, r, 0:BLOCK, :],
                                 v_ref[0, r, 0:BLOCK, :], bias_first)
            store(r, 0, o, lse)

        def later_block(r, n, q_ref=q_ref, k_ref=k_ref, v_ref=v_ref, store=store):
            if isinstance(n, int):
                q_rows, kv_rows = pl.ds(n * BLOCK, BLOCK), pl.ds((n - 1) * BLOCK, 2 * BLOCK)
            else:
                q_rows = pl.ds(pl.multiple_of(n * BLOCK, BLOCK), BLOCK)
                kv_rows = pl.ds(pl.multiple_of((n - 1) * BLOCK, BLOCK), 2 * BLOCK)
            o, lse = _attn_block(q_ref[0, r, q_rows, :], k_ref[0, r, kv_rows, :], v_ref[0, r, kv_rows, :],
                                 bias_both)
            store(r, n, o, lse)

        def block_pair(i, carry, q_ref=q_ref, k_ref=k_ref, v_ref=v_ref, store=store):
            def both(ref):
                return jnp.concatenate([ref[0, 2 * i], ref[0, 2 * i + 1]], axis=0)

            o, lse = _attn_block(both(q_ref), both(k_ref), both(v_ref), bias_pair)
            store(2 * i, 0, o[:BLOCK], lse[:BLOCK])
            store(2 * i + 1, 0, o[BLOCK:], lse[BLOCK:])
            return carry

        if nb > 1:
            for r in range(d):
                first_block(r)
                for n in range(1, nb):
                    later_block(r, n)
        else:
            for i in range(d // 2):
                block_pair(i, 0)

        if d > SUB:
            part = seq // SUB
            for r_lo in range(SUB):
                for scr in (o_scr, lse_scr):
                    scr[g, pl.ds(r_lo, part, stride=SUB), :] = scr[N_DIL, r_lo * part:(r_lo + 1) * part, :]

    rc = 256

    def combine(i, carry):
        rows = pl.ds(pl.multiple_of(i * rc, rc), rc)
        l0, l1, l2 = lse_scr[0, rows, :], lse_scr[1, rows, :], lse_scr[2, rows, :]
        m = jnp.maximum(jnp.maximum(l0, l1), l2)
        e0, e1, e2 = jnp.exp(l0 - m), jnp.exp(l1 - m), jnp.exp(l2 - m)
        o = (e0 * o_scr[0, rows, :] + e1 * o_scr[1, rows, :] + e2 * o_scr[2, rows, :]) / (e0 + e1 + e2)
        z = z0[0, 0, rows, :].astype(F32)
        y_ref[0, rows, :] = (o * _silu(z)).astype(BF16)
        return carry

    lax.fori_loop(0, seq // rc, combine, 0)


def _attention(p0, p1, p2):
    B, _, S, _ = p0.shape
    H = HEADS_PER_GROUP

    def spec(d, sec):
        return pl.BlockSpec((1, d, S // d, LANES), lambda b, h: (b, 0, 0, sec * H + h))

    in_specs = [spec(1, 0), spec(1, 1), spec(1, 2), spec(1, 3)]
    for d in (DIL_GROUPS[1][1], DIL_GROUPS[2][1]):
        in_specs += [spec(d, 0), spec(d, 1), spec(d, 2)]
    return pl.pallas_call(
        functools.partial(_attn_kernel, seq=S),
        out_shape=jax.ShapeDtypeStruct((B, S, ATTN_OUT), BF16),
        grid=(B, H),
        in_specs=in_specs,
        out_specs=pl.BlockSpec((1, S, LANES), lambda b, h: (b, 0, h)),
        scratch_shapes=[pltpu.VMEM((N_DIL + 1, S, LANES), F32), pltpu.VMEM((N_DIL + 1, S, LANES), F32)],
        compiler_params=_params(2),
        name="dilated_attention",
    )(p0, p0, p0, p0, p1, p1, p1, p2, p2, p2)


def _out_proj_kernel(y_ref, w_ref, x_ref, gate_ref, o_ref, acc_a, acc_b):
    bufs = (acc_a, acc_b)

    def matmul(c, slot):
        bufs[slot][...] = jnp.dot(y_ref[...], w_ref[:, c * MXU_N:(c + 1) * MXU_N],
                                  preferred_element_type=F32)

    def finish(c, slot):
        cols = slice(c * MXU_N, (c + 1) * MXU_N)
        o_ref[:, cols] = x_ref[:, cols] + gate_ref[0][:, cols] * bufs[slot][...]

    _pipelined_subtiles(o_ref.shape[1] // MXU_N, matmul, finish)


def _out_proj(y, w_bf16, x, gate, *, tm=512):
    B, S, D = x.shape
    M, K = y.shape
    tiles_m = S // tm
    out = pl.pallas_call(
        _out_proj_kernel,
        out_shape=jax.ShapeDtypeStruct((M, D), F32),
        grid=(M // tm,),
        in_specs=[pl.BlockSpec((tm, K), lambda i: (i, 0)),
                  pl.BlockSpec((K, D), lambda i: (0, 0), pipeline_mode=pl.Buffered(1)),
                  pl.BlockSpec((tm, D), lambda i: (i, 0)),
                  pl.BlockSpec((1, 1, D), lambda i: (i // tiles_m, 0, 0))],
        out_specs=pl.BlockSpec((tm, D), lambda i: (i, 0)),
        scratch_shapes=[pltpu.VMEM((tm, MXU_N), F32)] * 2,
        compiler_params=_params(1),
        name="out_proj",
    )(y, w_bf16, x.reshape(M, D), gate)
    return out.reshape(B, S, D)


def _sgu_proj_kernel(x_ref, sc_ref, sh_ref, g_ref, w_ref, o_ref, h_ref, acc_a, acc_b, *, tm, tn, n_gelu):
    j = pl.program_id(1)
    bufs = (acc_a, acc_b)

    @pl.when(j == 0)
    def _():
        _norm_prologue(x_ref, sc_ref, sh_ref, g_ref, h_ref, None, d=1, tm=tm)

    def matmul(c, slot):
        bufs[slot][...] = jnp.dot(h_ref[...], w_ref[:, c * MXU_N:(c + 1) * MXU_N].astype(BF16),
                                  preferred_element_type=F32)

    def finish(c, slot, gelu):
        a = bufs[slot][...]
        o_ref[:, c * MXU_N:(c + 1) * MXU_N] = (_gelu_tanh(a) if gelu else a).astype(BF16)

    @pl.when(j < n_gelu)
    def _():
        _pipelined_subtiles(tn // MXU_N, matmul, functools.partial(finish, gelu=True))

    @pl.when(j >= n_gelu)
    def _():
        _pipelined_subtiles(tn // MXU_N, matmul, functools.partial(finish, gelu=False))


def _sgu_proj(x, scale, shift, norm_g, w_in, layer, *, tm=1024, tn=1024):
    B, S, D = x.shape
    N = w_in.shape[2]
    tiles_m = S // tm
    return pl.pallas_call(
        functools.partial(_sgu_proj_kernel, tm=tm, tn=tn, n_gelu=2 * SGU_WIDTH // tn),
        out_shape=jax.ShapeDtypeStruct((B * S, N), BF16),
        grid=(B * tiles_m, N // tn),
        in_specs=[pl.BlockSpec((1, tm, D), lambda i, j: (i // tiles_m, i % tiles_m, 0)),
                  pl.BlockSpec((1, 1, D), lambda i, j: (i // tiles_m, 0, 0)),
                  pl.BlockSpec((1, 1, D), lambda i, j: (i // tiles_m, 0, 0)),
                  pl.BlockSpec((1, D), lambda i, j: (0, 0)),
                  pl.BlockSpec((None, D, tn), lambda i, j: (layer, 0, j))],
        out_specs=pl.BlockSpec((tm, tn), lambda i, j: (i, j)),
        scratch_shapes=[pltpu.VMEM((tm, D), BF16)] + [pltpu.VMEM((tm, MXU_N), F32)] * 2,
        compiler_params=_params(2),
        name="sgu_proj",
    )(x, scale, shift, norm_g, w_in)


def _sgu_core_kernel(u_ref, v_ref, z_ref, lng_ref, lnb_ref, ws_ref, bst_ref, y_ref, wm_ref, vn_ref, *, tc):
    t_out = lax.broadcasted_iota(jnp.int32, (SGU_CHUNK, SGU_CHUNK), 0)
    s_in = lax.broadcasted_iota(jnp.int32, (SGU_CHUNK, SGU_CHUNK), 1)
    causal = s_in <= t_out
    for g in range(SGU_GROUPS):
        wm_ref[g] = jnp.where(causal, ws_ref[g], 0.0).astype(BF16)
    gw = SGU_WIDTH // SGU_GROUPS

    def chunk(ci, carry):
        rows = pl.ds(pl.multiple_of(ci * SGU_CHUNK, SGU_CHUNK), SGU_CHUNK)
        v = v_ref[rows, :].astype(F32)
        mu = jnp.mean(v, axis=-1, keepdims=True)
        vc = v - mu
        var = jnp.mean(vc * vc, axis=-1, keepdims=True)
        vn_ref[...] = (vc * lax.rsqrt(var + NORM_EPS) * lng_ref[...] + lnb_ref[...]).astype(BF16)
        for g in range(SGU_GROUPS):
            cols = slice(g * gw, (g + 1) * gw)
            sv = jnp.dot(wm_ref[g], vn_ref[:, cols], preferred_element_type=F32) + bst_ref[:, g:g + 1]
            z = z_ref[rows, cols].astype(F32)
            y_ref[rows, cols] = (u_ref[rows, cols].astype(F32) * sv * _silu(z)).astype(BF16)
        return carry

    lax.fori_loop(0, tc // SGU_CHUNK, chunk, 0)


def _sgu_core(proj, ln_g, ln_b, ws, bs, *, tc=256):
    M = proj.shape[0]
    E = SGU_WIDTH
    return pl.pallas_call(
        functools.partial(_sgu_core_kernel, tc=tc),
        out_shape=jax.ShapeDtypeStruct((M, E), BF16),
        grid=(M // tc,),
        in_specs=[pl.BlockSpec((tc, E), lambda i: (i, 0)),
                  pl.BlockSpec((tc, E), lambda i: (i, 1)),
                  pl.BlockSpec((tc, E), lambda i: (i, 2)),
                  pl.BlockSpec((1, E), lambda i: (0, 0)),
                  pl.BlockSpec((1, E), lambda i: (0, 0)),
                  pl.BlockSpec((SGU_GROUPS, SGU_CHUNK, SGU_CHUNK), lambda i: (0, 0, 0)),
                  pl.BlockSpec((SGU_CHUNK, SGU_GROUPS), lambda i: (0, 0))],
        out_specs=pl.BlockSpec((tc, E), lambda i: (i, 0)),
        scratch_shapes=[pltpu.VMEM((SGU_GROUPS, SGU_CHUNK, SGU_CHUNK), BF16),
                        pltpu.VMEM((SGU_CHUNK, E), BF16)],
        compiler_params=_params(1),
        name="sgu_core",
    )(proj, proj, proj, ln_g.reshape(1, E), ln_b.reshape(1, E), ws, bs.T)


def _conv_proj_kernel(x_ref, sc_ref, sh_ref, g_ref, wa_ref, wb_ref, o_ref, h_ref,
                      acc_a, acc_b, gate_a, gate_b, *, tm, tn, n_glu):
    j = pl.program_id(1)
    accs, gates = (acc_a, acc_b), (gate_a, gate_b)

    @pl.when(j == 0)
    def _():
        _norm_prologue(x_ref, sc_ref, sh_ref, g_ref, h_ref, None, d=1, tm=tm)

    def matmul(c, slot, glu):
        cols = slice(c * MXU_N, (c + 1) * MXU_N)
        accs[slot][...] = jnp.dot(h_ref[...], wa_ref[:, cols], preferred_element_type=F32)
        if glu:
            gates[slot][...] = jnp.dot(h_ref[...], wb_ref[:, cols], preferred_element_type=F32)

    def finish(c, slot, glu):
        a = accs[slot][...]
        if glu:
            a = a * _sigmoid(gates[slot][...])
        o_ref[:, c * MXU_N:(c + 1) * MXU_N] = a.astype(BF16)

    @pl.when(j < n_glu)
    def _():
        _pipelined_subtiles(tn // MXU_N, functools.partial(matmul, glu=True), functools.partial(finish, glu=True))

    @pl.when(j >= n_glu)
    def _():
        _pipelined_subtiles(tn // MXU_N, functools.partial(matmul, glu=False), functools.partial(finish, glu=False))


def _conv_proj(x, scale, shift, norm_g, w_bf16, *, tm=1024, tn=1024):
    B, S, D = x.shape
    E = CONV_WIDTH
    nj = E // tn
    tiles_m = S // tm
    return pl.pallas_call(
        functools.partial(_conv_proj_kernel, tm=tm, tn=tn, n_glu=nj),
        out_shape=jax.ShapeDtypeStruct((B * S, 2 * E), BF16),
        grid=(B * tiles_m, 2 * nj),
        in_specs=[pl.BlockSpec((1, tm, D), lambda i, j: (i // tiles_m, i % tiles_m, 0)),
                  pl.BlockSpec((1, 1, D), lambda i, j: (i // tiles_m, 0, 0)),
                  pl.BlockSpec((1, 1, D), lambda i, j: (i // tiles_m, 0, 0)),
                  pl.BlockSpec((1, D), lambda i, j: (0, 0)),
                  pl.BlockSpec((D, tn), lambda i, j: (0, jnp.where(j < nj, j, nj + j))),
                  pl.BlockSpec((D, tn), lambda i, j: (0, jnp.minimum(nj + j, 2 * nj - 1)))],
        out_specs=pl.BlockSpec((tm, tn), lambda i, j: (i, j)),
        scratch_shapes=[pltpu.VMEM((tm, D), BF16)] + [pltpu.VMEM((tm, MXU_N), F32)] * 4,
        compiler_params=_params(2),
        name="conv_proj",
    )(x, scale, shift, norm_g, w_bf16, w_bf16)


def _conv_core_kernel(g_ref, halo_ref, z_ref, w_ref, b_ref, lng_ref, lnb_ref, y_ref, ext_ref, cv_ref,
                      *, tc, tiles_per_seq):
    n_slabs = CONV_WIDTH // LANES
    first = (pl.program_id(0) % tiles_per_seq) == 0
    for c in range(n_slabs):
        cols = slice(c * LANES, (c + 1) * LANES)
        halo = halo_ref[:, cols].astype(F32)
        ext_ref[c, 0:HALO, :] = jnp.where(first, 0.0, halo)
        ext_ref[c, HALO:, :] = g_ref[:, cols].astype(F32)

    rc = 64
    lead = HALO - (CONV_K - 1)

    def slab(c, carry):
        for r0 in range(0, tc, rc):
            acc = jnp.broadcast_to(b_ref[c], (rc, LANES))
            for k in range(CONV_K):
                acc = acc + w_ref[c, k:k + 1, :] * ext_ref[c, r0 + lead + k:r0 + lead + k + rc, :]
            cv_ref[c, r0:r0 + rc, :] = acc
        return carry

    lax.fori_loop(0, n_slabs, slab, 0)

    inv_e = 1.0 / CONV_WIDTH
    for r0 in range(0, tc, rc):
        rows = slice(r0, r0 + rc)
        tot = cv_ref[0, rows, :]
        for c in range(1, n_slabs):
            tot = tot + cv_ref[c, rows, :]
        mu = jnp.sum(tot, axis=-1, keepdims=True) * inv_e
        sq = jnp.zeros((rc, LANES), F32)
        for c in range(n_slabs):
            dlt = cv_ref[c, rows, :] - mu
            sq = sq + dlt * dlt
        inv = lax.rsqrt(jnp.sum(sq, axis=-1, keepdims=True) * inv_e + NORM_EPS)
        for c in range(n_slabs):
            cols = slice(c * LANES, (c + 1) * LANES)
            t = (cv_ref[c, rows, :] - mu) * inv * lng_ref[:, cols] + lnb_ref[:, cols]
            z = z_ref[rows, cols].astype(F32)
            y_ref[rows, cols] = (_silu(t) * _silu(z)).astype(BF16)


def _conv_core(proj, seq, dw_w, dw_b, ln_g, ln_b, *, tc=256):
    M = proj.shape[0]
    E = CONV_WIDTH
    n_slabs = E // LANES
    per_halo = tc // HALO
    w_slabs = dw_w.reshape(CONV_K, n_slabs, LANES).transpose(1, 0, 2)
    return pl.pallas_call(
        functools.partial(_conv_core_kernel, tc=tc, tiles_per_seq=seq // tc),
        out_shape=jax.ShapeDtypeStruct((M, E), BF16),
        grid=(M // tc,),
        in_specs=[pl.BlockSpec((tc, E), lambda i: (i, 0)),
                  pl.BlockSpec((HALO, E), lambda i: (jnp.maximum(i * per_halo - 1, 0), 0)),
                  pl.BlockSpec((tc, E), lambda i: (i, 1)),
                  pl.BlockSpec((n_slabs, CONV_K, LANES), lambda i: (0, 0, 0)),
                  pl.BlockSpec((n_slabs, 1, LANES), lambda i: (0, 0, 0)),
                  pl.BlockSpec((1, E), lambda i: (0, 0)),
                  pl.BlockSpec((1, E), lambda i: (0, 0))],
        out_specs=pl.BlockSpec((tc, E), lambda i: (i, 0)),
        scratch_shapes=[pltpu.VMEM((n_slabs, tc + HALO, LANES), F32),
                        pltpu.VMEM((n_slabs, tc, LANES), F32)],
        compiler_params=_params(1),
        name="conv_core",
    )(proj, proj, proj, w_slabs, dw_b.reshape(n_slabs, 1, LANES), ln_g.reshape(1, E), ln_b.reshape(1, E))


def _rotate_half_matrix():
    half = ROPE_DIM // 2
    k = jnp.arange(HEAD_DIM)[:, None]
    j = jnp.arange(HEAD_DIM)[None, :]
    p = (jnp.where((j < half) & (k == j + half), -1.0, 0.0)
         + jnp.where((j >= half) & (j < ROPE_DIM) & (k == j - half), 1.0, 0.0))
    return jnp.kron(jnp.eye(MXU_N // HEAD_DIM, dtype=F32), p).astype(BF16)


def _rope_tables(positions):
    B, S = positions.shape
    inv_freq = jnp.power(ROPE_THETA, -jnp.arange(0, ROPE_DIM, 2, dtype=F32) / ROPE_DIM)
    ang = positions.astype(F32)[..., None] * inv_freq
    cos, sin = jnp.cos(ang), jnp.sin(ang)
    rest = HEAD_DIM - ROPE_DIM
    cos_t = jnp.concatenate([cos, cos, jnp.ones((B, S, rest), F32)], axis=-1)
    sin_t = jnp.concatenate([sin, sin, jnp.zeros((B, S, rest), F32)], axis=-1)
    return cos_t, sin_t


def _gained_tables(rope, q_gain, k_gain):
    cos_t, sin_t = rope
    B, S, _ = cos_t.shape
    half = ROPE_DIM // 2
    gains = jnp.stack([q_gain, k_gain])
    lane = jnp.arange(HEAD_DIM)
    partner = jnp.where(lane < half, jnp.roll(gains, -half, axis=-1), jnp.roll(gains, half, axis=-1))
    stacked = (gains[:, None, None, :] * cos_t[None], partner[:, None, None, :] * sin_t[None])
    out = []
    for _, d in DIL_GROUPS:
        out.append(tuple(t.reshape(2, B, S // d, d, HEAD_DIM).transpose(0, 1, 3, 2, 4) for t in stacked))
    return out


def kernel(x, c, positions, ada_w, ada_b, norm_g, attn_w_in, attn_q_gain, attn_k_gain, attn_w_out,
           sgu_w_in, sgu_ln_g, sgu_ln_b, sgu_ws, sgu_bs, sgu_w_out,
           conv_w_in, conv_dw_w, conv_dw_b, conv_ln_g, conv_ln_b, conv_w_out):
    B, S, D = x.shape
    mod = _ada_modulation(c, ada_w, ada_b)
    rope = _rope_tables(positions)
    for i in range(DEPTH):
        shift = mod[i, :, None, :D]
        scale = mod[i, :, None, D:2 * D]
        gate = mod[i, :, None, 2 * D:]
        g_row = norm_g[i].reshape(1, D)
        kind, l = i % N_MIXERS, i // N_MIXERS
        if kind == 0:
            tables = _gained_tables(rope, attn_q_gain[l] * (HEAD_DIM ** -0.5), attn_k_gain[l])
            proj = [_attn_proj(x, scale, shift, g_row, attn_w_in, l, tables[g], group=g, with_gate=(g == 0))
                    for g in range(N_DIL)]
            y = _attention(*proj).reshape(B * S, ATTN_OUT)
            x = _out_proj(y, attn_w_out[l].astype(BF16), x, gate)
        elif kind == 1:
            proj = _sgu_proj(x, scale, shift, g_row, sgu_w_in, l)
            y = _sgu_core(proj, sgu_ln_g[l], sgu_ln_b[l], sgu_ws[l], sgu_bs[l])
            x = _out_proj(y, sgu_w_out[l].astype(BF16), x, gate)
        else:
            proj = _conv_proj(x, scale, shift, g_row, conv_w_in[l].astype(BF16))
            y = _conv_core(proj, S, conv_dw_w[l], conv_dw_b[l], conv_ln_g[l], conv_ln_b[l])
            x = _out_proj(y, conv_w_out[l].astype(BF16), x, gate)
    return x
```

```python
import functools

import jax
import jax.numpy as jnp
from jax import lax
from jax.experimental import pallas as pl
from jax.experimental.pallas import tpu as pltpu

D_MODEL = 2048
DEPTH = 4
N_MIXERS = 3
HEAD_DIM = 128
DIL_GROUPS = ((128, 1), (512, 4), (2048, 16))
N_DIL = len(DIL_GROUPS)
HEADS_PER_GROUP = D_MODEL // HEAD_DIM
ATTN_OUT = HEADS_PER_GROUP * HEAD_DIM
ATTN_QKV = N_DIL * ATTN_OUT
ROPE_DIM = HEAD_DIM // 4
ROPE_THETA = 500000.0
BLOCK = 128
SGU_WIDTH = 2 * D_MODEL
SGU_CHUNK = 128
SGU_GROUPS = 16
CONV_WIDTH = 2 * D_MODEL
CONV_K = 31
NORM_EPS = 1e-6

LANES = 128
MXU_N = 256
SUB = 4
HALO = 32
VMEM_LIMIT = 56 * 1024 * 1024
MASKED = -1e30

F32 = jnp.float32
BF16 = jnp.bfloat16


def _params(n_axes):
    return pltpu.CompilerParams(dimension_semantics=("arbitrary",) * n_axes,
                                vmem_limit_bytes=VMEM_LIMIT)


def _sigmoid(x):
    return 1.0 / (1.0 + jnp.exp(-x))


def _silu(x):
    return x * _sigmoid(x)


def _gelu_tanh(x):
    return x * (0.5 * (1.0 + jnp.tanh(0.7978845608028654 * (x + 0.044715 * (x * x * x)))))


def _chained_subtiles(n_sub, matmul, finish):
    assert n_sub % 2 == 0
    for c in range(n_sub):
        matmul(c + 1, (c + 1) % 2)
        finish(c, c % 2)


def _split_bf16(v):
    hi = v.astype(BF16)
    return hi, (v - hi.astype(F32)).astype(BF16)


def _ada_kernel(c_ref, w_ref, b_ref, o_ref):
    c_hi, c_lo = _split_bf16(_silu(c_ref[...]))
    w_hi, w_lo = _split_bf16(w_ref[0])
    rows = c_hi.shape[0]
    both = jnp.dot(jnp.concatenate([c_hi, c_lo], axis=0), w_hi, preferred_element_type=F32)
    o_ref[0] = (both[:rows] + both[rows:] + jnp.dot(c_hi, w_lo, preferred_element_type=F32)
                + b_ref[0])


def _ada_modulation(c, ada_w, ada_b):
    B, D = c.shape
    rows = 16
    tn = 1024
    c8 = jnp.pad(c, ((0, rows - B), (0, 0)))
    out = pl.pallas_call(
        _ada_kernel,
        out_shape=jax.ShapeDtypeStruct((DEPTH, rows, 3 * D), F32),
        grid=(DEPTH, 3 * D // tn),
        in_specs=[pl.BlockSpec((rows, D), lambda l, j: (0, 0)),
                  pl.BlockSpec((1, D, tn), lambda l, j: (l, 0, j)),
                  pl.BlockSpec((1, 1, tn), lambda l, j: (l, 0, j))],
        out_specs=pl.BlockSpec((1, rows, tn), lambda l, j: (l, 0, j)),
        compiler_params=_params(2),
        name="ada_modulation",
    )(c8, ada_w, ada_b.reshape(DEPTH, 1, 3 * D))
    return out[:, :B]


def _norm_prologue(x_ref, sc_ref, sh_ref, g_ref, h_ref, work_ref, *, d, tm):
    rc = 128
    a = g_ref[...] * (1.0 + sc_ref[0])
    sh = sh_ref[0]

    def chunk_rows(i):
        return pl.ds(pl.multiple_of(i * rc, rc), rc)

    def inv_rms(x):
        return lax.rsqrt(jnp.mean(x * x, axis=-1, keepdims=True) + NORM_EPS)

    if d == 1:
        def body(i, carry):
            x = x_ref[0, chunk_rows(i), :]
            h_ref[chunk_rows(i), :] = (x * inv_rms(x) * a + sh).astype(BF16)
            return carry

        lax.fori_loop(0, tm // rc, body, 0)
        return

    INV, NAT, TMP = 0, 1, 2

    def stats(i, carry):
        work_ref[INV, chunk_rows(i), :] = jnp.broadcast_to(inv_rms(x_ref[0, chunk_rows(i), :]), (rc, LANES))
        return carry

    lax.fori_loop(0, tm // rc, stats, 0)
    rows = tm // d
    for c in range(D_MODEL // LANES):
        cols = slice(c * LANES, (c + 1) * LANES)

        work_ref[NAT] = x_ref[0, :, cols] * work_ref[INV] * a[:, cols] + sh[:, cols]
        if d <= SUB:
            for r in range(d):
                h_ref[r * rows:(r + 1) * rows, cols] = work_ref[NAT, pl.ds(r, rows, stride=d), :].astype(BF16)
            continue
        part = tm // SUB
        for r_lo in range(SUB):
            work_ref[TMP, r_lo * part:(r_lo + 1) * part, :] = work_ref[NAT, pl.ds(r_lo, part, stride=SUB), :]
        for r in range(d):
            src = pl.ds((r % SUB) * part + r // SUB, rows, stride=d // SUB)
            h_ref[r * rows:(r + 1) * rows, cols] = work_ref[TMP, src, :].astype(BF16)


def _attn_proj_kernel(x_ref, sc_ref, sh_ref, g_ref, w_ref, wn_ref, perm_ref, cos_ref, sin_ref,
                      o_ref, h_ref, acc_a, acc_b, part_ref, *work, d, tm, tn, n_rope):
    j = pl.program_id(1)
    rows = tm // d
    rc = min(32, rows)
    per_res = rows // rc
    n_sub = tn // MXU_N
    bufs = (acc_a, acc_b)

    def matmul(c, slot):
        w = wn_ref[...] if c == n_sub else w_ref[:, c * MXU_N:(c + 1) * MXU_N]
        bufs[slot][...] = jnp.dot(h_ref[...], w.astype(BF16), preferred_element_type=F32)

    @pl.when(j == 0)
    def _():
        _norm_prologue(x_ref, sc_ref, sh_ref, g_ref, h_ref, work[0] if work else None, d=d, tm=tm)
        matmul(0, 0)

    def finish(c, slot, rope):
        acc_ref, c0 = bufs[slot], c * MXU_N
        if rope:
            part_ref[...] = jnp.dot(acc_ref[...].astype(BF16), perm_ref[...], preferred_element_type=F32)
        for ci in range(tm // rc):
            r, m0, p0 = ci // per_res, (ci % per_res) * rc, ci * rc
            if not rope:
                o_ref[0, r, m0:m0 + rc, c0:c0 + MXU_N] = acc_ref[p0:p0 + rc, :].astype(BF16)
                continue
            cos = cos_ref[0, 0, r, m0:m0 + rc, :]
            sin = sin_ref[0, 0, r, m0:m0 + rc, :]
            for hc in range(MXU_N // LANES):
                cols = slice(hc * LANES, (hc + 1) * LANES)
                a = acc_ref[p0:p0 + rc, cols]
                ms = jnp.sum(a * a, axis=-1, keepdims=True) * (1.0 / HEAD_DIM)
                rot = (a * cos + part_ref[p0:p0 + rc, cols] * sin) * lax.rsqrt(ms + NORM_EPS)
                o_ref[0, r, m0:m0 + rc, c0 + hc * LANES:c0 + (hc + 1) * LANES] = rot.astype(BF16)

    @pl.when(j < n_rope)
    def _():
        _chained_subtiles(n_sub, matmul, functools.partial(finish, rope=True))

    @pl.when(j >= n_rope)
    def _():
        _chained_subtiles(n_sub, matmul, functools.partial(finish, rope=False))


def _attn_proj(x, scale, shift, norm_g, w_in, layer, tables, *, group, with_gate, tm=1024, tn=1024):
    B, S, D = x.shape
    d = DIL_GROUPS[group][1]
    n_sec = 4 if with_gate else 3
    per_sec = ATTN_OUT // tn
    n_tiles = n_sec * per_sec
    tiles_m = S // tm

    def col_block(t):
        sec = t // per_sec
        base = jnp.where(sec < 3, sec * (ATTN_QKV // tn) + group * per_sec, 3 * (ATTN_QKV // tn))
        return base + t % per_sec

    def w_index(i, j):
        return (layer, 0, col_block(j))

    def w_next_index(i, j):
        return (layer, 0, col_block(jnp.minimum(j + 1, n_tiles - 1)) * (tn // MXU_N))

    tab_spec = pl.BlockSpec((1, 1, d, tm // d, LANES),
                            lambda i, j: (jnp.minimum(j // per_sec, 1), i // tiles_m, 0, i % tiles_m, 0))
    scratch = [pltpu.VMEM((tm, D), BF16)] + [pltpu.VMEM((tm, MXU_N), F32)] * 3
    if d > 1:
        scratch.append(pltpu.VMEM((3, tm, LANES), F32))
    return pl.pallas_call(
        functools.partial(_attn_proj_kernel, d=d, tm=tm, tn=tn, n_rope=2 * per_sec),
        out_shape=jax.ShapeDtypeStruct((B, d, S // d, n_sec * ATTN_OUT), BF16),
        grid=(B * tiles_m, n_tiles),
        in_specs=[pl.BlockSpec((1, tm, D), lambda i, j: (i // tiles_m, i % tiles_m, 0)),
                  pl.BlockSpec((1, 1, D), lambda i, j: (i // tiles_m, 0, 0)),
                  pl.BlockSpec((1, 1, D), lambda i, j: (i // tiles_m, 0, 0)),
                  pl.BlockSpec((1, D), lambda i, j: (0, 0)),
                  pl.BlockSpec((None, D, tn), w_index),
                  pl.BlockSpec((None, D, MXU_N), w_next_index),
                  pl.BlockSpec((MXU_N, MXU_N), lambda i, j: (0, 0)),
                  tab_spec, tab_spec],
        out_specs=pl.BlockSpec((1, d, tm // d, tn), lambda i, j: (i // tiles_m, 0, i % tiles_m, j)),
        scratch_shapes=scratch,
        compiler_params=_params(2),
        name=f"attn_proj_g{group}",
    )(x, scale, shift, norm_g, w_in, w_in, _rotate_half_matrix(), *tables)


def _attn_block(q, k, v, bias):
    s = lax.dot_general(q, k, (((1,), (1,)), ((), ())), preferred_element_type=F32) + bias
    m = jnp.max(s, axis=-1, keepdims=True)
    p = jnp.exp(s - m).astype(BF16)
    v_ext = jnp.concatenate([v, jnp.ones(v.shape, BF16)], axis=1)
    ol = jnp.dot(p, v_ext, preferred_element_type=F32)
    l = ol[:, HEAD_DIM:]
    return ol[:, :HEAD_DIM] * (1.0 / l), m + jnp.log(l)


def _attn_kernel(q0, k0, v0, z0, q1, k1, v1, q2, k2, v2, y_ref, o_scr, lse_scr, *, seq):
    dist = (lax.broadcasted_iota(jnp.int32, (BLOCK, 2 * BLOCK), 0) + BLOCK
            - lax.broadcasted_iota(jnp.int32, (BLOCK, 2 * BLOCK), 1))
    bias_both = jnp.where(dist >= 0, jnp.where(dist <= BLOCK, 0.0, MASKED), MASKED).astype(F32)
    bias_first = bias_both[:, BLOCK:]
    row = lax.broadcasted_iota(jnp.int32, (2 * BLOCK, 2 * BLOCK), 0)
    col = lax.broadcasted_iota(jnp.int32, (2 * BLOCK, 2 * BLOCK), 1)
    same_seq = (row // BLOCK) == (col // BLOCK)
    bias_pair = jnp.where(same_seq, jnp.where(col <= row, 0.0, MASKED), MASKED).astype(F32)

    groups = ((q0, k0, v0), (q1, k1, v1), (q2, k2, v2))
    for g, (q_ref, k_ref, v_ref) in enumerate(groups):
        d = DIL_GROUPS[g][1]
        nb = seq // d // BLOCK

        def store(r, n, o, lse, g=g, d=d):
            if d == 1:
                start = BLOCK * n
                rows = pl.ds(start if isinstance(start, int) else pl.multiple_of(start, BLOCK), BLOCK)
            elif d <= SUB:
                rows = pl.ds(r + d * BLOCK * n, BLOCK, stride=d)
            else:
                rows = pl.ds((r % SUB) * (seq // SUB) + r // SUB + (d // SUB) * BLOCK * n, BLOCK,
                             stride=d // SUB)
            slot = g if d <= SUB else N_DIL
            o_scr[slot, rows, :] = o
            lse_scr[slot, rows, :] = lse

        def first_block(r, q_ref=q_ref, k_ref=k_ref, v_ref=v_ref, store=store):
            o, lse = _attn_block(q_ref[0, r, 0:BLOCK, :], k_ref[0, r, 0:BLOCK, :],
                                 v_ref[0, r, 0:BLOCK, :], bias_first)
            store(r, 0, o, lse)

        def later_block(r, n, q_ref=q_ref, k_ref=k_ref, v_ref=v_ref, store=store):
            if isinstance(n, int):
                q_rows, kv_rows = pl.ds(n * BLOCK, BLOCK), pl.ds((n - 1) * BLOCK, 2 * BLOCK)
            else:
                q_rows = pl.ds(pl.multiple_of(n * BLOCK, BLOCK), BLOCK)
                kv_rows = pl.ds(pl.multiple_of((n - 1) * BLOCK, BLOCK), 2 * BLOCK)
            o, lse = _attn_block(q_ref[0, r, q_rows, :], k_ref[0, r, kv_rows, :], v_ref[0, r, kv_rows, :],
                                 bias_both)
            store(r, n, o, lse)

        def block_pair(i, carry, q_ref=q_ref, k_ref=k_ref, v_ref=v_ref, store=store):
            def both(ref):
                return jnp.concatenate([ref[0, 2 * i], ref[0, 2 * i + 1]], axis=0)

            o, lse = _attn_block(both(q_ref), both(k_ref), both(v_ref), bias_pair)
            store(2 * i, 0, o[:BLOCK], lse[:BLOCK])
            store(2 * i + 1, 0, o[BLOCK:], lse[BLOCK:])
            return carry

        if nb > 1:
            for r in range(d):
                first_block(r)
                for n in range(1, nb):
                    later_block(r, n)
        else:
            for i in range(d // 2):
                block_pair(i, 0)

        if d > SUB:
            part = seq // SUB
            for r_lo in range(SUB):
                for scr in (o_scr, lse_scr):
                    scr[g, pl.ds(r_lo, part, stride=SUB), :] = scr[N_DIL, r_lo * part:(r_lo + 1) * part, :]

    rc = 256

    def combine(i, carry):
        rows = pl.ds(pl.multiple_of(i * rc, rc), rc)
        l0, l1, l2 = lse_scr[0, rows, :], lse_scr[1, rows, :], lse_scr[2, rows, :]
        m = jnp.maximum(jnp.maximum(l0, l1), l2)
        e0, e1, e2 = jnp.exp(l0 - m), jnp.exp(l1 - m), jnp.exp(l2 - m)
        o = (e0 * o_scr[0, rows, :] + e1 * o_scr[1, rows, :] + e2 * o_scr[2, rows, :]) / (e0 + e1 + e2)
        z = z0[0, 0, rows, :].astype(F32)
        y_ref[0, rows, :] = (o * _silu(z)).astype(BF16)
        return carry

    lax.fori_loop(0, seq // rc, combine, 0)


def _attention(p0, p1, p2):
    B, _, S, _ = p0.shape
    H = HEADS_PER_GROUP

    def spec(d, sec):
        return pl.BlockSpec((1, d, S // d, LANES), lambda b, h: (b, 0, 0, sec * H + h))

    in_specs = [spec(1, 0), spec(1, 1), spec(1, 2), spec(1, 3)]
    for d in (DIL_GROUPS[1][1], DIL_GROUPS[2][1]):
        in_specs += [spec(d, 0), spec(d, 1), spec(d, 2)]
    return pl.pallas_call(
        functools.partial(_attn_kernel, seq=S),
        out_shape=jax.ShapeDtypeStruct((B, S, ATTN_OUT), BF16),
        grid=(B, H),
        in_specs=in_specs,
        out_specs=pl.BlockSpec((1, S, LANES), lambda b, h: (b, 0, h)),
        scratch_shapes=[pltpu.VMEM((N_DIL + 1, S, LANES), F32), pltpu.VMEM((N_DIL + 1, S, LANES), F32)],
        compiler_params=_params(2),
        name="dilated_attention",
    )(p0, p0, p0, p0, p1, p1, p1, p2, p2, p2)


def _out_proj_kernel(y_ref, yn_ref, w_ref, x_ref, gate_ref, o_ref, acc_a, acc_b):
    bufs = (acc_a, acc_b)
    n_sub = o_ref.shape[1] // MXU_N

    def matmul(c, slot):
        y = yn_ref if c == n_sub else y_ref
        c = c % n_sub
        bufs[slot][...] = jnp.dot(y[...], w_ref[:, c * MXU_N:(c + 1) * MXU_N], preferred_element_type=F32)

    def finish(c, slot):
        cols = slice(c * MXU_N, (c + 1) * MXU_N)
        o_ref[:, cols] = x_ref[:, cols] + gate_ref[0][:, cols] * bufs[slot][...]

    @pl.when(pl.program_id(0) == 0)
    def _():
        matmul(0, 0)

    _chained_subtiles(n_sub, matmul, finish)


def _out_proj(y, w_bf16, x, gate, *, tm=512):
    B, S, D = x.shape
    M, K = y.shape
    tiles_m = S // tm
    n_steps = M // tm
    out = pl.pallas_call(
        _out_proj_kernel,
        out_shape=jax.ShapeDtypeStruct((M, D), F32),
        grid=(n_steps,),
        in_specs=[pl.BlockSpec((tm, K), lambda i: (i, 0)),
                  pl.BlockSpec((tm, K), lambda i: (jnp.minimum(i + 1, n_steps - 1), 0)),
                  pl.BlockSpec((K, D), lambda i: (0, 0), pipeline_mode=pl.Buffered(1)),
                  pl.BlockSpec((tm, D), lambda i: (i, 0)),
                  pl.BlockSpec((1, 1, D), lambda i: (i // tiles_m, 0, 0))],
        out_specs=pl.BlockSpec((tm, D), lambda i: (i, 0)),
        scratch_shapes=[pltpu.VMEM((tm, MXU_N), F32)] * 2,
        compiler_params=_params(1),
        name="out_proj",
    )(y, y, w_bf16, x.reshape(M, D), gate)
    return out.reshape(B, S, D)


def _sgu_proj_kernel(x_ref, sc_ref, sh_ref, g_ref, w_ref, wn_ref, o_ref, h_ref, acc_a, acc_b, *, tm, tn, n_gelu):
    j = pl.program_id(1)
    n_sub = tn // MXU_N
    bufs = (acc_a, acc_b)

    def matmul(c, slot):
        w = wn_ref[...] if c == n_sub else w_ref[:, c * MXU_N:(c + 1) * MXU_N]
        bufs[slot][...] = jnp.dot(h_ref[...], w.astype(BF16), preferred_element_type=F32)

    @pl.when(j == 0)
    def _():
        _norm_prologue(x_ref, sc_ref, sh_ref, g_ref, h_ref, None, d=1, tm=tm)
        matmul(0, 0)

    def finish(c, slot, gelu):
        a = bufs[slot][...]
        o_ref[:, c * MXU_N:(c + 1) * MXU_N] = (_gelu_tanh(a) if gelu else a).astype(BF16)

    @pl.when(j < n_gelu)
    def _():
        _chained_subtiles(n_sub, matmul, functools.partial(finish, gelu=True))

    @pl.when(j >= n_gelu)
    def _():
        _chained_subtiles(n_sub, matmul, functools.partial(finish, gelu=False))


def _sgu_proj(x, scale, shift, norm_g, w_in, layer, *, tm=1024, tn=1024):
    B, S, D = x.shape
    N = w_in.shape[2]
    tiles_m = S // tm
    return pl.pallas_call(
        functools.partial(_sgu_proj_kernel, tm=tm, tn=tn, n_gelu=2 * SGU_WIDTH // tn),
        out_shape=jax.ShapeDtypeStruct((B * S, N), BF16),
        grid=(B * tiles_m, N // tn),
        in_specs=[pl.BlockSpec((1, tm, D), lambda i, j: (i // tiles_m, i % tiles_m, 0)),
                  pl.BlockSpec((1, 1, D), lambda i, j: (i // tiles_m, 0, 0)),
                  pl.BlockSpec((1, 1, D), lambda i, j: (i // tiles_m, 0, 0)),
                  pl.BlockSpec((1, D), lambda i, j: (0, 0)),
                  pl.BlockSpec((None, D, tn), lambda i, j: (layer, 0, j)),
                  pl.BlockSpec((None, D, MXU_N),
                               lambda i, j: (layer, 0, jnp.minimum(j + 1, N // tn - 1) * (tn // MXU_N)))],
        out_specs=pl.BlockSpec((tm, tn), lambda i, j: (i, j)),
        scratch_shapes=[pltpu.VMEM((tm, D), BF16)] + [pltpu.VMEM((tm, MXU_N), F32)] * 2,
        compiler_params=_params(2),
        name="sgu_proj",
    )(x, scale, shift, norm_g, w_in, w_in)


def _sgu_core_kernel(u_ref, v_ref, z_ref, lng_ref, lnb_ref, ws_ref, bst_ref, y_ref, wm_ref, vn_ref, *, tc):
    t_out = lax.broadcasted_iota(jnp.int32, (SGU_CHUNK, SGU_CHUNK), 0)
    s_in = lax.broadcasted_iota(jnp.int32, (SGU_CHUNK, SGU_CHUNK), 1)
    causal = s_in <= t_out
    for g in range(SGU_GROUPS):
        wm_ref[g] = jnp.where(causal, ws_ref[g], 0.0).astype(BF16)
    gw = SGU_WIDTH // SGU_GROUPS

    def chunk(ci, carry):
        rows = pl.ds(pl.multiple_of(ci * SGU_CHUNK, SGU_CHUNK), SGU_CHUNK)
        v = v_ref[rows, :].astype(F32)
        mu = jnp.mean(v, axis=-1, keepdims=True)
        vc = v - mu
        var = jnp.mean(vc * vc, axis=-1, keepdims=True)
        vn_ref[...] = (vc * lax.rsqrt(var + NORM_EPS) * lng_ref[...] + lnb_ref[...]).astype(BF16)
        for g in range(SGU_GROUPS):
            cols = slice(g * gw, (g + 1) * gw)
            sv = jnp.dot(wm_ref[g], vn_ref[:, cols], preferred_element_type=F32) + bst_ref[:, g:g + 1]
            z = z_ref[rows, cols].astype(F32)
            y_ref[rows, cols] = (u_ref[rows, cols].astype(F32) * sv * _silu(z)).astype(BF16)
        return carry

    lax.fori_loop(0, tc // SGU_CHUNK, chunk, 0)


def _sgu_core(proj, ln_g, ln_b, ws, bs, *, tc=256):
    M = proj.shape[0]
    E = SGU_WIDTH
    return pl.pallas_call(
        functools.partial(_sgu_core_kernel, tc=tc),
        out_shape=jax.ShapeDtypeStruct((M, E), BF16),
        grid=(M // tc,),
        in_specs=[pl.BlockSpec((tc, E), lambda i: (i, 0)),
                  pl.BlockSpec((tc, E), lambda i: (i, 1)),
                  pl.BlockSpec((tc, E), lambda i: (i, 2)),
                  pl.BlockSpec((1, E), lambda i: (0, 0)),
                  pl.BlockSpec((1, E), lambda i: (0, 0)),
                  pl.BlockSpec((SGU_GROUPS, SGU_CHUNK, SGU_CHUNK), lambda i: (0, 0, 0)),
                  pl.BlockSpec((SGU_CHUNK, SGU_GROUPS), lambda i: (0, 0))],
        out_specs=pl.BlockSpec((tc, E), lambda i: (i, 0)),
        scratch_shapes=[pltpu.VMEM((SGU_GROUPS, SGU_CHUNK, SGU_CHUNK), BF16),
                        pltpu.VMEM((SGU_CHUNK, E), BF16)],
        compiler_params=_params(1),
        name="sgu_core",
    )(proj, proj, proj, ln_g.reshape(1, E), ln_b.reshape(1, E), ws, bs.T)


def _conv_proj_kernel(x_ref, sc_ref, sh_ref, g_ref, wa_ref, wan_ref, wb_ref, wbn_ref, o_ref, h_ref,
                      acc_a, acc_b, gate_a, gate_b, *, tm, tn, n_glu):
    j = pl.program_id(1)
    n_sub = tn // MXU_N
    accs, gates = (acc_a, acc_b), (gate_a, gate_b)

    def matmul(c, slot, glu, glu_next):
        if c == n_sub:
            wa, wb, glu = wan_ref[...], wbn_ref[...], glu_next
        else:
            cols = slice(c * MXU_N, (c + 1) * MXU_N)
            wa, wb = wa_ref[:, cols], wb_ref[:, cols]
        accs[slot][...] = jnp.dot(h_ref[...], wa, preferred_element_type=F32)
        if glu:
            gates[slot][...] = jnp.dot(h_ref[...], wb, preferred_element_type=F32)

    def finish(c, slot, glu):
        a = accs[slot][...]
        if glu:
            a = a * _sigmoid(gates[slot][...])
        o_ref[:, c * MXU_N:(c + 1) * MXU_N] = a.astype(BF16)

    @pl.when(j == 0)
    def _():
        _norm_prologue(x_ref, sc_ref, sh_ref, g_ref, h_ref, None, d=1, tm=tm)
        matmul(0, 0, True, True)

    def step(glu, glu_next):
        _chained_subtiles(n_sub, functools.partial(matmul, glu=glu, glu_next=glu_next),
                          functools.partial(finish, glu=glu))

    @pl.when(j < n_glu - 1)
    def _():
        step(True, True)

    @pl.when(j == n_glu - 1)
    def _():
        step(True, False)

    @pl.when(j >= n_glu)
    def _():
        step(False, False)


def _conv_proj(x, scale, shift, norm_g, w_bf16, *, tm=1024, tn=1024):
    B, S, D = x.shape
    E = CONV_WIDTH
    nj = E // tn
    per = tn // MXU_N
    tiles_m = S // tm

    def a_block(j):
        return jnp.where(j < nj, j, nj + j)

    def b_block(j):
        return jnp.minimum(nj + j, 2 * nj - 1)

    def nxt(j):
        return jnp.minimum(j + 1, 2 * nj - 1)

    return pl.pallas_call(
        functools.partial(_conv_proj_kernel, tm=tm, tn=tn, n_glu=nj),
        out_shape=jax.ShapeDtypeStruct((B * S, 2 * E), BF16),
        grid=(B * tiles_m, 2 * nj),
        in_specs=[pl.BlockSpec((1, tm, D), lambda i, j: (i // tiles_m, i % tiles_m, 0)),
                  pl.BlockSpec((1, 1, D), lambda i, j: (i // tiles_m, 0, 0)),
                  pl.BlockSpec((1, 1, D), lambda i, j: (i // tiles_m, 0, 0)),
                  pl.BlockSpec((1, D), lambda i, j: (0, 0)),
                  pl.BlockSpec((D, tn), lambda i, j: (0, a_block(j))),
                  pl.BlockSpec((D, MXU_N), lambda i, j: (0, a_block(nxt(j)) * per)),
                  pl.BlockSpec((D, tn), lambda i, j: (0, b_block(j))),
                  pl.BlockSpec((D, MXU_N), lambda i, j: (0, b_block(nxt(j)) * per))],
        out_specs=pl.BlockSpec((tm, tn), lambda i, j: (i, j)),
        scratch_shapes=[pltpu.VMEM((tm, D), BF16)] + [pltpu.VMEM((tm, MXU_N), F32)] * 4,
        compiler_params=_params(2),
        name="conv_proj",
    )(x, scale, shift, norm_g, w_bf16, w_bf16, w_bf16, w_bf16)


def _conv_core_kernel(g_ref, halo_ref, z_ref, w_ref, b_ref, lng_ref, lnb_ref, y_ref, ext_ref, cv_ref,
                      *, tc, tiles_per_seq):
    n_slabs = CONV_WIDTH // LANES
    first = (pl.program_id(0) % tiles_per_seq) == 0
    for c in range(n_slabs):
        cols = slice(c * LANES, (c + 1) * LANES)
        halo = halo_ref[:, cols].astype(F32)
        ext_ref[c, 0:HALO, :] = jnp.where(first, 0.0, halo)
        ext_ref[c, HALO:, :] = g_ref[:, cols].astype(F32)

    rc = 64
    lead = HALO - (CONV_K - 1)

    def slab(c, carry):
        for r0 in range(0, tc, rc):
            acc = jnp.broadcast_to(b_ref[c], (rc, LANES))
            for k in range(CONV_K):
                acc = acc + w_ref[c, k:k + 1, :] * ext_ref[c, r0 + lead + k:r0 + lead + k + rc, :]
            cv_ref[c, r0:r0 + rc, :] = acc
        return carry

    lax.fori_loop(0, n_slabs, slab, 0)

    inv_e = 1.0 / CONV_WIDTH
    for r0 in range(0, tc, rc):
        rows = slice(r0, r0 + rc)
        tot = cv_ref[0, rows, :]
        for c in range(1, n_slabs):
            tot = tot + cv_ref[c, rows, :]
        mu = jnp.sum(tot, axis=-1, keepdims=True) * inv_e
        sq = jnp.zeros((rc, LANES), F32)
        for c in range(n_slabs):
            dlt = cv_ref[c, rows, :] - mu
            sq = sq + dlt * dlt
        inv = lax.rsqrt(jnp.sum(sq, axis=-1, keepdims=True) * inv_e + NORM_EPS)
        for c in range(n_slabs):
            cols = slice(c * LANES, (c + 1) * LANES)
            t = (cv_ref[c, rows, :] - mu) * inv * lng_ref[:, cols] + lnb_ref[:, cols]
            z = z_ref[rows, cols].astype(F32)
            y_ref[rows, cols] = (_silu(t) * _silu(z)).astype(BF16)


def _conv_core(proj, seq, dw_w, dw_b, ln_g, ln_b, *, tc=256):
    M = proj.shape[0]
    E = CONV_WIDTH
    n_slabs = E // LANES
    per_halo = tc // HALO
    w_slabs = dw_w.reshape(CONV_K, n_slabs, LANES).transpose(1, 0, 2)
    return pl.pallas_call(
        functools.partial(_conv_core_kernel, tc=tc, tiles_per_seq=seq // tc),
        out_shape=jax.ShapeDtypeStruct((M, E), BF16),
        grid=(M // tc,),
        in_specs=[pl.BlockSpec((tc, E), lambda i: (i, 0)),
                  pl.BlockSpec((HALO, E), lambda i: (jnp.maximum(i * per_halo - 1, 0), 0)),
                  pl.BlockSpec((tc, E), lambda i: (i, 1)),
                  pl.BlockSpec((n_slabs, CONV_K, LANES), lambda i: (0, 0, 0)),
                  pl.BlockSpec((n_slabs, 1, LANES), lambda i: (0, 0, 0)),
                  pl.BlockSpec((1, E), lambda i: (0, 0)),
                  pl.BlockSpec((1, E), lambda i: (0, 0))],
        out_specs=pl.BlockSpec((tc, E), lambda i: (i, 0)),
        scratch_shapes=[pltpu.VMEM((n_slabs, tc + HALO, LANES), F32),
                        pltpu.VMEM((n_slabs, tc, LANES), F32)],
        compiler_params=_params(1),
        name="conv_core",
    )(proj, proj, proj, w_slabs, dw_b.reshape(n_slabs, 1, LANES), ln_g.reshape(1, E), ln_b.reshape(1, E))


def _rotate_half_matrix():
    half = ROPE_DIM // 2
    k = jnp.arange(HEAD_DIM)[:, None]
    j = jnp.arange(HEAD_DIM)[None, :]
    p = (jnp.where((j < half) & (k == j + half), -1.0, 0.0)
         + jnp.where((j >= half) & (j < ROPE_DIM) & (k == j - half), 1.0, 0.0))
    return jnp.kron(jnp.eye(MXU_N // HEAD_DIM, dtype=F32), p).astype(BF16)


def _rope_tables(positions):
    B, S = positions.shape
    inv_freq = jnp.power(ROPE_THETA, -jnp.arange(0, ROPE_DIM, 2, dtype=F32) / ROPE_DIM)
    ang = positions.astype(F32)[..., None] * inv_freq
    cos, sin = jnp.cos(ang), jnp.sin(ang)
    rest = HEAD_DIM - ROPE_DIM
    cos_t = jnp.concatenate([cos, cos, jnp.ones((B, S, rest), F32)], axis=-1)
    sin_t = jnp.concatenate([sin, sin, jnp.zeros((B, S, rest), F32)], axis=-1)
    return cos_t, sin_t


def _gained_tables(rope, q_gain, k_gain):
    cos_t, sin_t = rope
    B, S, _ = cos_t.shape
    half = ROPE_DIM // 2
    gains = jnp.stack([q_gain, k_gain])
    lane = jnp.arange(HEAD_DIM)
    partner = jnp.where(lane < half, jnp.roll(gains, -half, axis=-1), jnp.roll(gains, half, axis=-1))
    stacked = (gains[:, None, None, :] * cos_t[None], partner[:, None, None, :] * sin_t[None])
    out = []
    for _, d in DIL_GROUPS:
        out.append(tuple(t.reshape(2, B, S // d, d, HEAD_DIM).transpose(0, 1, 3, 2, 4) for t in stacked))
    return out


def kernel(x, c, positions, ada_w, ada_b, norm_g, attn_w_in, attn_q_gain, attn_k_gain, attn_w_out,
           sgu_w_in, sgu_ln_g, sgu_ln_b, sgu_ws, sgu_bs, sgu_w_out,
           conv_w_in, conv_dw_w, conv_dw_b, conv_ln_g, conv_ln_b, conv_w_out):
    B, S, D = x.shape
    mod = _ada_modulation(c, ada_w, ada_b)
    rope = _rope_tables(positions)
    for i in range(DEPTH):
        shift = mod[i, :, None, :D]
        scale = mod[i, :, None, D:2 * D]
        gate = mod[i, :, None, 2 * D:]
        g_row = norm_g[i].reshape(1, D)
        kind, l = i % N_MIXERS, i // N_MIXERS
        if kind == 0:
            tables = _gained_tables(rope, attn_q_gain[l] * (HEAD_DIM ** -0.5), attn_k_gain[l])
            proj = [_attn_proj(x, scale, shift, g_row, attn_w_in, l, tables[g], group=g, with_gate=(g == 0))
                    for g in range(N_DIL)]
            y = _attention(*proj).reshape(B * S, ATTN_OUT)
            x = _out_proj(y, attn_w_out[l].astype(BF16), x, gate)
        elif kind == 1:
            proj = _sgu_proj(x, scale, shift, g_row, sgu_w_in, l)
            y = _sgu_core(proj, sgu_ln_g[l], sgu_ln_b[l], sgu_ws[l], sgu_bs[l])
            x = _out_proj(y, sgu_w_out[l].astype(BF16), x, gate)
        else:
            proj = _conv_proj(x, scale, shift, g_row, conv_w_in[l].astype(BF16))
            y = _conv_core(proj, S, conv_dw_w[l], conv_dw_b[l], conv_ln_g[l], conv_ln_b[l])
            x = _out_proj(y, conv_w_out[l].astype(BF16), x, gate)
    return x
```

```python
import functools

import jax
import jax.numpy as jnp
from jax import lax
from jax.experimental import pallas as pl
from jax.experimental.pallas import tpu as pltpu

D_MODEL = 2048
DEPTH = 4
N_MIXERS = 3
HEAD_DIM = 128
DIL_GROUPS = ((128, 1), (512, 4), (2048, 16))
N_DIL = len(DIL_GROUPS)
HEADS_PER_GROUP = D_MODEL // HEAD_DIM
ATTN_OUT = HEADS_PER_GROUP * HEAD_DIM
ATTN_QKV = N_DIL * ATTN_OUT
ROPE_DIM = HEAD_DIM // 4
ROPE_THETA = 500000.0
BLOCK = 128
SGU_WIDTH = 2 * D_MODEL
SGU_CHUNK = 128
SGU_GROUPS = 16
CONV_WIDTH = 2 * D_MODEL
CONV_K = 31
NORM_EPS = 1e-6

LANES = 128
MXU_N = 256
SUB = 4
HALO = 32
VMEM_LIMIT = 56 * 1024 * 1024
MASKED = -1e30

F32 = jnp.float32
BF16 = jnp.bfloat16


def _params(n_axes):
    return pltpu.CompilerParams(dimension_semantics=("arbitrary",) * n_axes,
                                vmem_limit_bytes=VMEM_LIMIT)


def _sigmoid(x):
    return 1.0 / (1.0 + jnp.exp(-x))


def _silu(x):
    return x * _sigmoid(x)


def _gelu_tanh(x):
    return x * (0.5 * (1.0 + jnp.tanh(0.7978845608028654 * (x + 0.044715 * (x * x * x)))))


def _pipelined_subtiles(n_sub, matmul, finish):
    for c in range(n_sub + 1):
        if c < n_sub:
            matmul(c, c % 2)
        if c >= 1:
            finish(c - 1, (c - 1) % 2)


def _split_bf16(v):
    hi = v.astype(BF16)
    return hi, (v - hi.astype(F32)).astype(BF16)


def _ada_kernel(c_ref, w_ref, b_ref, o_ref):
    c_hi, c_lo = _split_bf16(_silu(c_ref[...]))
    w_hi, w_lo = _split_bf16(w_ref[0])
    rows = c_hi.shape[0]
    both = jnp.dot(jnp.concatenate([c_hi, c_lo], axis=0), w_hi, preferred_element_type=F32)
    o_ref[0] = (both[:rows] + both[rows:] + jnp.dot(c_hi, w_lo, preferred_element_type=F32)
                + b_ref[0])


def _ada_modulation(c, ada_w, ada_b):
    B, D = c.shape
    rows = 16
    tn = 1024
    c8 = jnp.pad(c, ((0, rows - B), (0, 0)))
    out = pl.pallas_call(
        _ada_kernel,
        out_shape=jax.ShapeDtypeStruct((DEPTH, rows, 3 * D), F32),
        grid=(DEPTH, 3 * D // tn),
        in_specs=[pl.BlockSpec((rows, D), lambda l, j: (0, 0)),
                  pl.BlockSpec((1, D, tn), lambda l, j: (l, 0, j)),
                  pl.BlockSpec((1, 1, tn), lambda l, j: (l, 0, j))],
        out_specs=pl.BlockSpec((1, rows, tn), lambda l, j: (l, 0, j)),
        compiler_params=_params(2),
        name="ada_modulation",
    )(c8, ada_w, ada_b.reshape(DEPTH, 1, 3 * D))
    return out[:, :B]


def _norm_prologue(x_ref, sc_ref, sh_ref, g_ref, h_ref, work_ref, *, d, tm):
    rc = 128
    a = g_ref[...] * (1.0 + sc_ref[0])
    sh = sh_ref[0]

    def chunk_rows(i):
        return pl.ds(pl.multiple_of(i * rc, rc), rc)

    def inv_rms(x):
        return lax.rsqrt(jnp.mean(x * x, axis=-1, keepdims=True) + NORM_EPS)

    if d == 1:
        def body(i, carry):
            x = x_ref[0, chunk_rows(i), :]
            h_ref[chunk_rows(i), :] = (x * inv_rms(x) * a + sh).astype(BF16)
            return carry

        lax.fori_loop(0, tm // rc, body, 0)
        return

    INV, NAT, TMP = 0, 1, 2

    def stats(i, carry):
        work_ref[INV, chunk_rows(i), :] = jnp.broadcast_to(inv_rms(x_ref[0, chunk_rows(i), :]), (rc, LANES))
        return carry

    lax.fori_loop(0, tm // rc, stats, 0)
    rows = tm // d
    for c in range(D_MODEL // LANES):
        cols = slice(c * LANES, (c + 1) * LANES)
        work_ref[NAT] = x_ref[0, :, cols] * work_ref[INV] * a[:, cols] + sh[:, cols]
        if d <= SUB:
            for r in range(d):
                h_ref[r * rows:(r + 1) * rows, cols] = work_ref[NAT, pl.ds(r, rows, stride=d), :].astype(BF16)
            continue
        part = tm // SUB
        for r_lo in range(SUB):
            work_ref[TMP, r_lo * part:(r_lo + 1) * part, :] = work_ref[NAT, pl.ds(r_lo, part, stride=SUB), :]
        for r in range(d):
            src = pl.ds((r % SUB) * part + r // SUB, rows, stride=d // SUB)
            h_ref[r * rows:(r + 1) * rows, cols] = work_ref[TMP, src, :].astype(BF16)


def _attn_proj_kernel(x_ref, sc_ref, sh_ref, g_ref, w_ref, perm_ref, cos_ref, sin_ref,
                      o_ref, h_ref, acc_a, acc_b, part_ref, *work, d, tm, tn, n_rope):
    j = pl.program_id(1)
    rows = tm // d
    rc = min(32, rows)
    per_res = rows // rc
    bufs = (acc_a, acc_b)

    @pl.when(j == 0)
    def _():
        _norm_prologue(x_ref, sc_ref, sh_ref, g_ref, h_ref, work[0] if work else None, d=d, tm=tm)

    def matmul(c, slot):
        bufs[slot][...] = jnp.dot(h_ref[...], w_ref[:, c * MXU_N:(c + 1) * MXU_N].astype(BF16),
                                  preferred_element_type=F32)

    def finish(c, slot, rope):
        acc_ref, c0 = bufs[slot], c * MXU_N
        if rope:
            part_ref[...] = jnp.dot(acc_ref[...].astype(BF16), perm_ref[...], preferred_element_type=F32)
        for ci in range(tm // rc):
            r, m0, p0 = ci // per_res, (ci % per_res) * rc, ci * rc
            if not rope:
                o_ref[0, r, m0:m0 + rc, c0:c0 + MXU_N] = acc_ref[p0:p0 + rc, :].astype(BF16)
                continue
            cos = cos_ref[0, 0, r, m0:m0 + rc, :]
            sin = sin_ref[0, 0, r, m0:m0 + rc, :]
            for hc in range(MXU_N // LANES):
                cols = slice(hc * LANES, (hc + 1) * LANES)
                a = acc_ref[p0:p0 + rc, cols]
                ms = jnp.sum(a * a, axis=-1, keepdims=True) * (1.0 / HEAD_DIM)
                rot = (a * cos + part_ref[p0:p0 + rc, cols] * sin) * lax.rsqrt(ms + NORM_EPS)
                o_ref[0, r, m0:m0 + rc, c0 + hc * LANES:c0 + (hc + 1) * LANES] = rot.astype(BF16)

    @pl.when(j < n_rope)
    def _():
        _pipelined_subtiles(tn // MXU_N, matmul, functools.partial(finish, rope=True))

    @pl.when(j >= n_rope)
    def _():
        _pipelined_subtiles(tn // MXU_N, matmul, functools.partial(finish, rope=False))


def _attn_proj(x, scale, shift, norm_g, w_in, layer, tables, *, group, with_gate, tm=1024, tn=1024):
    B, S, D = x.shape
    d = DIL_GROUPS[group][1]
    n_sec = 4 if with_gate else 3
    per_sec = ATTN_OUT // tn
    tiles_m = S // tm

    def w_index(i, j):
        sec = j // per_sec
        base = jnp.where(sec < 3, sec * (ATTN_QKV // tn) + group * per_sec, 3 * (ATTN_QKV // tn))
        return (layer, 0, base + j % per_sec)

    tab_spec = pl.BlockSpec((1, 1, d, tm // d, LANES),
                            lambda i, j: (jnp.minimum(j // per_sec, 1), i // tiles_m, 0, i % tiles_m, 0))
    scratch = [pltpu.VMEM((tm, D), BF16)] + [pltpu.VMEM((tm, MXU_N), F32)] * 3
    if d > 1:
        scratch.append(pltpu.VMEM((3, tm, LANES), F32))
    return pl.pallas_call(
        functools.partial(_attn_proj_kernel, d=d, tm=tm, tn=tn, n_rope=2 * per_sec),
        out_shape=jax.ShapeDtypeStruct((B, d, S // d, n_sec * ATTN_OUT), BF16),
        grid=(B * tiles_m, n_sec * per_sec),
        in_specs=[pl.BlockSpec((1, tm, D), lambda i, j: (i // tiles_m, i % tiles_m, 0)),
                  pl.BlockSpec((1, 1, D), lambda i, j: (i // tiles_m, 0, 0)),
                  pl.BlockSpec((1, 1, D), lambda i, j: (i // tiles_m, 0, 0)),
                  pl.BlockSpec((1, D), lambda i, j: (0, 0)),
                  pl.BlockSpec((None, D, tn), w_index),
                  pl.BlockSpec((MXU_N, MXU_N), lambda i, j: (0, 0)),
                  tab_spec, tab_spec],
        out_specs=pl.BlockSpec((1, d, tm // d, tn), lambda i, j: (i // tiles_m, 0, i % tiles_m, j)),
        scratch_shapes=scratch,
        compiler_params=_params(2),
        name=f"attn_proj_g{group}",
    )(x, scale, shift, norm_g, w_in, _rotate_half_matrix(), *tables)


def _attn_block(q, k, v, bias):
    s = lax.dot_general(q, k, (((1,), (1,)), ((), ())), preferred_element_type=F32) + bias
    m = jnp.max(s, axis=-1, keepdims=True)
    p = jnp.exp(s - m).astype(BF16)
    v_ext = jnp.concatenate([v, jnp.ones(v.shape, BF16)], axis=1)
    ol = jnp.dot(p, v_ext, preferred_element_type=F32)
    l = ol[:, HEAD_DIM:]
    return ol[:, :HEAD_DIM] * (1.0 / l), m + jnp.log(l)


def _attn_kernel(q0, k0, v0, z0, q1, k1, v1, q2, k2, v2, y_ref, o_scr, lse_scr, *, seq):
    dist = (lax.broadcasted_iota(jnp.int32, (BLOCK, 2 * BLOCK), 0) + BLOCK
            - lax.broadcasted_iota(jnp.int32, (BLOCK, 2 * BLOCK), 1))
    bias_both = jnp.where(dist >= 0, jnp.where(dist <= BLOCK, 0.0, MASKED), MASKED).astype(F32)
    bias_first = bias_both[:, BLOCK:]
    row = lax.broadcasted_iota(jnp.int32, (2 * BLOCK, 2 * BLOCK), 0)
    col = lax.broadcasted_iota(jnp.int32, (2 * BLOCK, 2 * BLOCK), 1)
    same_seq = (row // BLOCK) == (col // BLOCK)
    bias_pair = jnp.where(same_seq, jnp.where(col <= row, 0.0, MASKED), MASKED).astype(F32)

    groups = ((q0, k0, v0), (q1, k1, v1), (q2, k2, v2))
    for g, (q_ref, k_ref, v_ref) in enumerate(groups):
        d = DIL_GROUPS[g][1]
        nb = seq // d // BLOCK

        def store(r, n, o, lse, g=g, d=d):
            if d == 1:
                rows = pl.ds(BLOCK * n, BLOCK)
            elif d <= SUB:
                rows = pl.ds(r + d * BLOCK * n, BLOCK, stride=d)
            else:
                rows = pl.ds((r % SUB) * (seq // SUB) + r // SUB + (d // SUB) * BLOCK * n, BLOCK,
                             stride=d // SUB)
            slot = g if d <= SUB else N_DIL
            o_scr[slot, rows, :] = o
            lse_scr[slot, rows, :] = lse

        def first_block(r, q_ref=q_ref, k_ref=k_ref, v_ref=v_ref, store=store):
            o, lse = _attn_block(q_ref[0, r, 0:BLOCK, :], k_ref[0, r, 0:BLOCK, :],
                                 v_ref[0, r, 0:BLOCK, :], bias_first)
            store(r, 0, o, lse)

        def later_block(r, n, q_ref=q_ref, k_ref=k_ref, v_ref=v_ref, store=store):
            q_rows, kv_rows = pl.ds(n * BLOCK, BLOCK), pl.ds((n - 1) * BLOCK, 2 * BLOCK)
            o, lse = _attn_block(q_ref[0, r, q_rows, :], k_ref[0, r, kv_rows, :], v_ref[0, r, kv_rows, :],
                                 bias_both)
            store(r, n, o, lse)

        def block_pair(i, q_ref=q_ref, k_ref=k_ref, v_ref=v_ref, store=store):
            def both(ref):
                return jnp.concatenate([ref[0, 2 * i], ref[0, 2 * i + 1]], axis=0)

            o, lse = _attn_block(both(q_ref), both(k_ref), both(v_ref), bias_pair)
            store(2 * i, 0, o[:BLOCK], lse[:BLOCK])
            store(2 * i + 1, 0, o[BLOCK:], lse[BLOCK:])

        if nb > 1:
            for r in range(d):
                first_block(r)
                for n in range(1, nb):
                    later_block(r, n)
        else:
            for i in range(d // 2):
                block_pair(i)

        if d > SUB:
            part = seq // SUB
            for r_lo in range(SUB):
                for scr in (o_scr, lse_scr):
                    scr[g, pl.ds(r_lo, part, stride=SUB), :] = scr[N_DIL, r_lo * part:(r_lo + 1) * part, :]

    rc = 256

    def combine(i, carry):
        rows = pl.ds(pl.multiple_of(i * rc, rc), rc)
        l0, l1, l2 = lse_scr[0, rows, :], lse_scr[1, rows, :], lse_scr[2, rows, :]
        m = jnp.maximum(jnp.maximum(l0, l1), l2)
        e0, e1, e2 = jnp.exp(l0 - m), jnp.exp(l1 - m), jnp.exp(l2 - m)
        o = (e0 * o_scr[0, rows, :] + e1 * o_scr[1, rows, :] + e2 * o_scr[2, rows, :]) / (e0 + e1 + e2)
        z = z0[0, 0, rows, :].astype(F32)
        y_ref[0, rows, :] = (o * _silu(z)).astype(BF16)
        return carry

    lax.fori_loop(0, seq // rc, combine, 0)


def _attention(p0, p1, p2):
    B, _, S, _ = p0.shape
    H = HEADS_PER_GROUP

    def spec(d, sec):
        return pl.BlockSpec((1, d, S // d, LANES), lambda b, h: (b, 0, 0, sec * H + h))

    in_specs = [spec(1, 0), spec(1, 1), spec(1, 2), spec(1, 3)]
    for d in (DIL_GROUPS[1][1], DIL_GROUPS[2][1]):
        in_specs += [spec(d, 0), spec(d, 1), spec(d, 2)]
    return pl.pallas_call(
        functools.partial(_attn_kernel, seq=S),
        out_shape=jax.ShapeDtypeStruct((B, S, ATTN_OUT), BF16),
        grid=(B, H),
        in_specs=in_specs,
        out_specs=pl.BlockSpec((1, S, LANES), lambda b, h: (b, 0, h)),
        scratch_shapes=[pltpu.VMEM((N_DIL + 1, S, LANES), F32), pltpu.VMEM((N_DIL + 1, S, LANES), F32)],
        compiler_params=_params(2),
        name="dilated_attention",
    )(p0, p0, p0, p0, p1, p1, p1, p2, p2, p2)


def _out_proj_kernel(y_ref, w_ref, x_ref, gate_ref, o_ref, acc_a, acc_b):
    bufs = (acc_a, acc_b)

    def matmul(c, slot):
        bufs[slot][...] = jnp.dot(y_ref[...], w_ref[:, c * MXU_N:(c + 1) * MXU_N],
                                  preferred_element_type=F32)

    def finish(c, slot):
        cols = slice(c * MXU_N, (c + 1) * MXU_N)
        o_ref[:, cols] = x_ref[:, cols] + gate_ref[0][:, cols] * bufs[slot][...]

    _pipelined_subtiles(o_ref.shape[1] // MXU_N, matmul, finish)


def _out_proj(y, w_bf16, x, gate, *, tm=512):
    B, S, D = x.shape
    M, K = y.shape
    tiles_m = S // tm
    out = pl.pallas_call(
        _out_proj_kernel,
        out_shape=jax.ShapeDtypeStruct((M, D), F32),
        grid=(M // tm,),
        in_specs=[pl.BlockSpec((tm, K), lambda i: (i, 0)),
                  pl.BlockSpec((K, D), lambda i: (0, 0), pipeline_mode=pl.Buffered(1)),
                  pl.BlockSpec((tm, D), lambda i: (i, 0)),
                  pl.BlockSpec((1, 1, D), lambda i: (i // tiles_m, 0, 0))],
        out_specs=pl.BlockSpec((tm, D), lambda i: (i, 0)),
        scratch_shapes=[pltpu.VMEM((tm, MXU_N), F32)] * 2,
        compiler_params=_params(1),
        name="out_proj",
    )(y, w_bf16, x.reshape(M, D), gate)
    return out.reshape(B, S, D)


def _sgu_proj_kernel(x_ref, sc_ref, sh_ref, g_ref, w_ref, o_ref, h_ref, acc_a, acc_b, *, tm, tn, n_gelu):
    j = pl.program_id(1)
    bufs = (acc_a, acc_b)

    @pl.when(j == 0)
    def _():
        _norm_prologue(x_ref, sc_ref, sh_ref, g_ref, h_ref, None, d=1, tm=tm)

    def matmul(c, slot):
        bufs[slot][...] = jnp.dot(h_ref[...], w_ref[:, c * MXU_N:(c + 1) * MXU_N].astype(BF16),
                                  preferred_element_type=F32)

    def finish(c, slot, gelu):
        a = bufs[slot][...]
        o_ref[:, c * MXU_N:(c + 1) * MXU_N] = (_gelu_tanh(a) if gelu else a).astype(BF16)

    @pl.when(j < n_gelu)
    def _():
        _pipelined_subtiles(tn // MXU_N, matmul, functools.partial(finish, gelu=True))

    @pl.when(j >= n_gelu)
    def _():
        _pipelined_subtiles(tn // MXU_N, matmul, functools.partial(finish, gelu=False))


def _sgu_proj(x, scale, shift, norm_g, w_in, layer, *, tm=1024, tn=1024):
    B, S, D = x.shape
    N = w_in.shape[2]
    tiles_m = S // tm
    return pl.pallas_call(
        functools.partial(_sgu_proj_kernel, tm=tm, tn=tn, n_gelu=2 * SGU_WIDTH // tn),
        out_shape=jax.ShapeDtypeStruct((B * S, N), BF16),
        grid=(B * tiles_m, N // tn),
        in_specs=[pl.BlockSpec((1, tm, D), lambda i, j: (i // tiles_m, i % tiles_m, 0)),
                  pl.BlockSpec((1, 1, D), lambda i, j: (i // tiles_m, 0, 0)),
                  pl.BlockSpec((1, 1, D), lambda i, j: (i // tiles_m, 0, 0)),
                  pl.BlockSpec((1, D), lambda i, j: (0, 0)),
                  pl.BlockSpec((None, D, tn), lambda i, j: (layer, 0, j))],
        out_specs=pl.BlockSpec((tm, tn), lambda i, j: (i, j)),
        scratch_shapes=[pltpu.VMEM((tm, D), BF16)] + [pltpu.VMEM((tm, MXU_N), F32)] * 2,
        compiler_params=_params(2),
        name="sgu_proj",
    )(x, scale, shift, norm_g, w_in)


def _sgu_core_kernel(u_ref, v_ref, z_ref, lng_ref, lnb_ref, ws_ref, bst_ref, y_ref, wm_ref, vn_ref, *, tc):
    t_out = lax.broadcasted_iota(jnp.int32, (SGU_CHUNK, SGU_CHUNK), 0)
    s_in = lax.broadcasted_iota(jnp.int32, (SGU_CHUNK, SGU_CHUNK), 1)
    causal = s_in <= t_out
    for g in range(SGU_GROUPS):
        wm_ref[g] = jnp.where(causal, ws_ref[g], 0.0).astype(BF16)
    gw = SGU_WIDTH // SGU_GROUPS

    def chunk(ci, carry):
        rows = pl.ds(pl.multiple_of(ci * SGU_CHUNK, SGU_CHUNK), SGU_CHUNK)
        v = v_ref[rows, :].astype(F32)
        mu = jnp.mean(v, axis=-1, keepdims=True)
        vc = v - mu
        var = jnp.mean(vc * vc, axis=-1, keepdims=True)
        vn_ref[...] = (vc * lax.rsqrt(var + NORM_EPS) * lng_ref[...] + lnb_ref[...]).astype(BF16)
        for g in range(SGU_GROUPS):
            cols = slice(g * gw, (g + 1) * gw)
            sv = jnp.dot(wm_ref[g], vn_ref[:, cols], preferred_element_type=F32) + bst_ref[:, g:g + 1]
            z = z_ref[rows, cols].astype(F32)
            y_ref[rows, cols] = (u_ref[rows, cols].astype(F32) * sv * _silu(z)).astype(BF16)
        return carry

    lax.fori_loop(0, tc // SGU_CHUNK, chunk, 0)


def _sgu_core(proj, ln_g, ln_b, ws, bs, *, tc=256):
    M = proj.shape[0]
    E = SGU_WIDTH
    return pl.pallas_call(
        functools.partial(_sgu_core_kernel, tc=tc),
        out_shape=jax.ShapeDtypeStruct((M, E), BF16),
        grid=(M // tc,),
        in_specs=[pl.BlockSpec((tc, E), lambda i: (i, 0)),
                  pl.BlockSpec((tc, E), lambda i: (i, 1)),
                  pl.BlockSpec((tc, E), lambda i: (i, 2)),
                  pl.BlockSpec((1, E), lambda i: (0, 0)),
                  pl.BlockSpec((1, E), lambda i: (0, 0)),
                  pl.BlockSpec((SGU_GROUPS, SGU_CHUNK, SGU_CHUNK), lambda i: (0, 0, 0)),
                  pl.BlockSpec((SGU_CHUNK, SGU_GROUPS), lambda i: (0, 0))],
        out_specs=pl.BlockSpec((tc, E), lambda i: (i, 0)),
        scratch_shapes=[pltpu.VMEM((SGU_GROUPS, SGU_CHUNK, SGU_CHUNK), BF16),
                        pltpu.VMEM((SGU_CHUNK, E), BF16)],
        compiler_params=_params(1),
        name="sgu_core",
    )(proj, proj, proj, ln_g.reshape(1, E), ln_b.reshape(1, E), ws, bs.T)


def _conv_mix_kernel(x_ref, sc_ref, sh_ref, g_ref, wa_ref, wb_ref, dw_ref, db_ref, lng_ref, lnb_ref,
                     y_ref, h_ref, acc_a, acc_b, gate_a, gate_b, ext_ref, halo_ref, cv_ref, s1_ref, s2_ref,
                     *, tm, tn, n_glu, tiles_per_seq):
    i, j = pl.program_id(0), pl.program_id(1)
    per_sub = MXU_N // LANES
    per_step = tn // LANES
    accs, gates = (acc_a, acc_b), (gate_a, gate_b)
    first_of_seq = (i % tiles_per_seq) == 0
    lead = HALO - (CONV_K - 1)
    rc = 64

    @pl.when(j == 0)
    def _():
        _norm_prologue(x_ref, sc_ref, sh_ref, g_ref, h_ref, None, d=1, tm=tm)
        s1_ref[...] = jnp.zeros_like(s1_ref)
        s2_ref[...] = jnp.zeros_like(s2_ref)

    def matmul(c, slot, glu):
        cols = slice(c * MXU_N, (c + 1) * MXU_N)
        accs[slot][...] = jnp.dot(h_ref[...], wa_ref[:, cols], preferred_element_type=F32)
        if glu:
            gates[slot][...] = jnp.dot(h_ref[...], wb_ref[:, cols], preferred_element_type=F32)

    def finish_glu(c, slot):
        for s in range(per_sub):
            lanes = slice(s * LANES, (s + 1) * LANES)
            slab = j * per_step + c * per_sub + s
            stage = (c % 2) * per_sub + s
            ext_ref[stage, 0:HALO, :] = jnp.where(first_of_seq, 0.0, halo_ref[slab])
            for r0 in range(0, tm, rc):
                ext_ref[stage, HALO + r0:HALO + r0 + rc, :] = (
                    accs[slot][r0:r0 + rc, lanes] * _sigmoid(gates[slot][r0:r0 + rc, lanes]))
            halo_ref[slab] = ext_ref[stage, tm:tm + HALO, :]
            for r0 in range(0, tm, rc):
                acc = jnp.broadcast_to(db_ref[slab], (rc, LANES))
                for k in range(CONV_K):
                    acc = acc + dw_ref[slab, k:k + 1, :] * ext_ref[stage, r0 + lead + k:r0 + lead + k + rc, :]
                cv_ref[slab, r0:r0 + rc, :] = acc.astype(BF16)
                s1_ref[r0:r0 + rc, :] += acc
                s2_ref[r0:r0 + rc, :] += acc * acc

    def finish_z(c, slot):
        mu, inv = s1_ref[...], s2_ref[...]
        for s in range(per_sub):
            lanes = slice(s * LANES, (s + 1) * LANES)
            slab = (j - n_glu) * per_step + c * per_sub + s
            t = (cv_ref[slab].astype(F32) - mu) * inv * lng_ref[slab] + lnb_ref[slab]
            y_ref[:, c * MXU_N + s * LANES:c * MXU_N + (s + 1) * LANES] = (
                _silu(t) * _silu(accs[slot][:, lanes])).astype(BF16)

    @pl.when(j < n_glu)
    def _():
        _pipelined_subtiles(tn // MXU_N, functools.partial(matmul, glu=True), finish_glu)

    @pl.when(j == n_glu)
    def _():
        inv_e = 1.0 / CONV_WIDTH
        mu = jnp.sum(s1_ref[...], axis=-1, keepdims=True) * inv_e
        var = jnp.maximum(jnp.sum(s2_ref[...], axis=-1, keepdims=True) * inv_e - mu * mu, 0.0)
        s1_ref[...] = jnp.broadcast_to(mu, s1_ref.shape)
        s2_ref[...] = jnp.broadcast_to(lax.rsqrt(var + NORM_EPS), s2_ref.shape)

    @pl.when(j >= n_glu)
    def _():
        _pipelined_subtiles(tn // MXU_N, functools.partial(matmul, glu=False), finish_z)


def _conv_mix(x, scale, shift, norm_g, w_bf16, dw_w, dw_b, ln_g, ln_b, *, tm=512, tn=1024):
    B, S, D = x.shape
    E = CONV_WIDTH
    nj = E // tn
    n_slabs = E // LANES
    tiles_m = S // tm
    w_slabs = dw_w.reshape(CONV_K, n_slabs, LANES).transpose(1, 0, 2)

    def slabs(v):
        return v.reshape(n_slabs, 1, LANES)

    def whole(shape):
        return pl.BlockSpec(shape, lambda i, j: (0,) * len(shape))

    return pl.pallas_call(
        functools.partial(_conv_mix_kernel, tm=tm, tn=tn, n_glu=nj, tiles_per_seq=tiles_m),
        out_shape=jax.ShapeDtypeStruct((B * S, E), BF16),
        grid=(B * tiles_m, 2 * nj),
        in_specs=[pl.BlockSpec((1, tm, D), lambda i, j: (i // tiles_m, i % tiles_m, 0)),
                  pl.BlockSpec((1, 1, D), lambda i, j: (i // tiles_m, 0, 0)),
                  pl.BlockSpec((1, 1, D), lambda i, j: (i // tiles_m, 0, 0)),
                  pl.BlockSpec((1, D), lambda i, j: (0, 0)),
                  pl.BlockSpec((D, tn), lambda i, j: (0, jnp.where(j < nj, j, nj + j))),
                  pl.BlockSpec((D, tn), lambda i, j: (0, jnp.minimum(nj + j, 2 * nj - 1))),
                  whole((n_slabs, CONV_K, LANES)), whole((n_slabs, 1, LANES)),
                  whole((n_slabs, 1, LANES)), whole((n_slabs, 1, LANES))],
        out_specs=pl.BlockSpec((tm, tn), lambda i, j: (i, jnp.maximum(j - nj, 0))),
        scratch_shapes=[pltpu.VMEM((tm, D), BF16)] + [pltpu.VMEM((tm, MXU_N), F32)] * 4
        + [pltpu.VMEM((2 * MXU_N // LANES, tm + HALO, LANES), F32),
           pltpu.VMEM((n_slabs, HALO, LANES), F32),
           pltpu.VMEM((n_slabs, tm, LANES), BF16),
           pltpu.VMEM((tm, LANES), F32), pltpu.VMEM((tm, LANES), F32)],
        compiler_params=_params(2),
        name="conv_mix",
    )(x, scale, shift, norm_g, w_bf16, w_bf16, w_slabs, slabs(dw_b), slabs(ln_g), slabs(ln_b))


def _rotate_half_matrix():
    half = ROPE_DIM // 2
    k = jnp.arange(HEAD_DIM)[:, None]
    j = jnp.arange(HEAD_DIM)[None, :]
    p = (jnp.where((j < half) & (k == j + half), -1.0, 0.0)
         + jnp.where((j >= half) & (j < ROPE_DIM) & (k == j - half), 1.0, 0.0))
    return jnp.kron(jnp.eye(MXU_N // HEAD_DIM, dtype=F32), p).astype(BF16)


def _rope_tables(positions):
    B, S = positions.shape
    inv_freq = jnp.power(ROPE_THETA, -jnp.arange(0, ROPE_DIM, 2, dtype=F32) / ROPE_DIM)
    ang = positions.astype(F32)[..., None] * inv_freq
    cos, sin = jnp.cos(ang), jnp.sin(ang)
    rest = HEAD_DIM - ROPE_DIM
    cos_t = jnp.concatenate([cos, cos, jnp.ones((B, S, rest), F32)], axis=-1)
    sin_t = jnp.concatenate([sin, sin, jnp.zeros((B, S, rest), F32)], axis=-1)
    return cos_t, sin_t


def _gained_tables(rope, q_gain, k_gain):
    cos_t, sin_t = rope
    B, S, _ = cos_t.shape
    half = ROPE_DIM // 2
    gains = jnp.stack([q_gain, k_gain])
    lane = jnp.arange(HEAD_DIM)
    partner = jnp.where(lane < half, jnp.roll(gains, -half, axis=-1), jnp.roll(gains, half, axis=-1))
    stacked = (gains[:, None, None, :] * cos_t[None], partner[:, None, None, :] * sin_t[None])
    out = []
    for _, d in DIL_GROUPS:
        out.append(tuple(t.reshape(2, B, S // d, d, HEAD_DIM).transpose(0, 1, 3, 2, 4) for t in stacked))
    return out


def kernel(x, c, positions, ada_w, ada_b, norm_g, attn_w_in, attn_q_gain, attn_k_gain, attn_w_out,
           sgu_w_in, sgu_ln_g, sgu_ln_b, sgu_ws, sgu_bs, sgu_w_out,
           conv_w_in, conv_dw_w, conv_dw_b, conv_ln_g, conv_ln_b, conv_w_out):
    B, S, D = x.shape
    mod = _ada_modulation(c, ada_w, ada_b)
    rope = _rope_tables(positions)
    for i in range(DEPTH):
        shift = mod[i, :, None, :D]
        scale = mod[i, :, None, D:2 * D]
        gate = mod[i, :, None, 2 * D:]
        g_row = norm_g[i].reshape(1, D)
        kind, l = i % N_MIXERS, i // N_MIXERS
        if kind == 0:
            tables = _gained_tables(rope, attn_q_gain[l] * (HEAD_DIM ** -0.5), attn_k_gain[l])
            proj = [_attn_proj(x, scale, shift, g_row, attn_w_in, l, tables[g], group=g, with_gate=(g == 0))
                    for g in range(N_DIL)]
            y = _attention(*proj).reshape(B * S, ATTN_OUT)
            x = _out_proj(y, attn_w_out[l].astype(BF16), x, gate)
        elif kind == 1:
            proj = _sgu_proj(x, scale, shift, g_row, sgu_w_in, l)
            y = _sgu_core(proj, sgu_ln_g[l], sgu_ln_b[l], sgu_ws[l], sgu_bs[l])
            x = _out_proj(y, sgu_w_out[l].astype(BF16), x, gate)
        else:
            y = _conv_mix(x, scale, shift, g_row, conv_w_in[l].astype(BF16), conv_dw_w[l], conv_dw_b[l],
                          conv_ln_g[l], conv_ln_b[l])
            x = _out_proj(y, conv_w_out[l].astype(BF16), x, gate)
    return x
```

```python
import functools

import jax
import jax.numpy as jnp
from jax import lax
from jax.experimental import pallas as pl
from jax.experimental.pallas import tpu as pltpu

D_MODEL = 2048
DEPTH = 4
N_MIXERS = 3
HEAD_DIM = 128
DIL_GROUPS = ((128, 1), (512, 4), (2048, 16))
N_DIL = len(DIL_GROUPS)
HEADS_PER_GROUP = D_MODEL // HEAD_DIM
ATTN_OUT = HEADS_PER_GROUP * HEAD_DIM
ATTN_QKV = N_DIL * ATTN_OUT
ROPE_DIM = HEAD_DIM // 4
ROPE_THETA = 500000.0
BLOCK = 128
SGU_WIDTH = 2 * D_MODEL
SGU_CHUNK = 128
SGU_GROUPS = 16
CONV_WIDTH = 2 * D_MODEL
CONV_K = 31
NORM_EPS = 1e-6

LANES = 128
MXU_N = 256
SUB = 4
HALO = 32
VMEM_LIMIT = 56 * 1024 * 1024
MASKED = -1e30

F32 = jnp.float32
BF16 = jnp.bfloat16


def _params(n_axes):
    return pltpu.CompilerParams(dimension_semantics=("arbitrary",) * n_axes,
                                vmem_limit_bytes=VMEM_LIMIT)


def _sigmoid(x):
    return 1.0 / (1.0 + jnp.exp(-x))


def _silu(x):
    return x * _sigmoid(x)


def _gelu_tanh(x):
    return x * (0.5 * (1.0 + jnp.tanh(0.7978845608028654 * (x + 0.044715 * (x * x * x)))))


def _pipelined_subtiles(n_sub, matmul, finish):
    for c in range(n_sub + 1):
        if c < n_sub:
            matmul(c, c % 2)
        if c >= 1:
            finish(c - 1, (c - 1) % 2)


def _split_bf16(v):
    hi = v.astype(BF16)
    return hi, (v - hi.astype(F32)).astype(BF16)


def _ada_kernel(c_ref, w_ref, b_ref, o_ref):
    c_hi, c_lo = _split_bf16(_silu(c_ref[...]))
    w_hi, w_lo = _split_bf16(w_ref[0])
    rows = c_hi.shape[0]
    both = jnp.dot(jnp.concatenate([c_hi, c_lo], axis=0), w_hi, preferred_element_type=F32)
    o_ref[0] = (both[:rows] + both[rows:] + jnp.dot(c_hi, w_lo, preferred_element_type=F32)
                + b_ref[0])


def _ada_modulation(c, ada_w, ada_b):
    B, D = c.shape
    rows = 16
    tn = 1024
    c8 = jnp.pad(c, ((0, rows - B), (0, 0)))
    out = pl.pallas_call(
        _ada_kernel,
        out_shape=jax.ShapeDtypeStruct((DEPTH, rows, 3 * D), F32),
        grid=(DEPTH, 3 * D // tn),
        in_specs=[pl.BlockSpec((rows, D), lambda l, j: (0, 0)),
                  pl.BlockSpec((1, D, tn), lambda l, j: (l, 0, j)),
                  pl.BlockSpec((1, 1, tn), lambda l, j: (l, 0, j))],
        out_specs=pl.BlockSpec((1, rows, tn), lambda l, j: (l, 0, j)),
        compiler_params=_params(2),
        name="ada_modulation",
    )(c8, ada_w, ada_b.reshape(DEPTH, 1, 3 * D))
    return out[:, :B]


def _norm_prologue(x_ref, sc_ref, sh_ref, g_ref, h_ref, work_ref, *, d, tm):
    rc = 128
    a = g_ref[...] * (1.0 + sc_ref[0])
    sh = sh_ref[0]

    def chunk_rows(i):
        return pl.ds(pl.multiple_of(i * rc, rc), rc)

    def inv_rms(x):
        return lax.rsqrt(jnp.mean(x * x, axis=-1, keepdims=True) + NORM_EPS)

    if d == 1:
        def body(i, carry):
            x = x_ref[0, chunk_rows(i), :]
            h_ref[chunk_rows(i), :] = (x * inv_rms(x) * a + sh).astype(BF16)
            return carry

        lax.fori_loop(0, tm // rc, body, 0)
        return

    INV, NAT, TMP = 0, 1, 2

    def stats(i, carry):
        work_ref[INV, chunk_rows(i), :] = jnp.broadcast_to(inv_rms(x_ref[0, chunk_rows(i), :]), (rc, LANES))
        return carry

    lax.fori_loop(0, tm // rc, stats, 0)
    rows = tm // d
    for c in range(D_MODEL // LANES):
        cols = slice(c * LANES, (c + 1) * LANES)
        work_ref[NAT] = x_ref[0, :, cols] * work_ref[INV] * a[:, cols] + sh[:, cols]
        if d <= SUB:
            for r in range(d):
                h_ref[r * rows:(r + 1) * rows, cols] = work_ref[NAT, pl.ds(r, rows, stride=d), :].astype(BF16)
            continue
        part = tm // SUB
        for r_lo in range(SUB):
            work_ref[TMP, r_lo * part:(r_lo + 1) * part, :] = work_ref[NAT, pl.ds(r_lo, part, stride=SUB), :]
        for r in range(d):
            src = pl.ds((r % SUB) * part + r // SUB, rows, stride=d // SUB)
            h_ref[r * rows:(r + 1) * rows, cols] = work_ref[TMP, src, :].astype(BF16)


def _attn_proj_kernel(x_ref, sc_ref, sh_ref, g_ref, w_ref, perm_ref, gain_ref, cos_ref, sin_ref,
                      o_ref, h_ref, acc_a, acc_b, part_ref, *work, d, tm, tn, n_rope):
    j = pl.program_id(1)
    rows = tm // d
    rc = min(32, rows)
    per_res = rows // rc
    bufs = (acc_a, acc_b)

    @pl.when(j == 0)
    def _():
        _norm_prologue(x_ref, sc_ref, sh_ref, g_ref, h_ref, work[0] if work else None, d=d, tm=tm)

    def matmul(c, slot):
        bufs[slot][...] = jnp.dot(h_ref[...], w_ref[:, c * MXU_N:(c + 1) * MXU_N].astype(BF16),
                                  preferred_element_type=F32)

    def finish(c, slot, rope):
        acc_ref, c0 = bufs[slot], c * MXU_N
        if rope:
            part_ref[...] = jnp.dot(acc_ref[...].astype(BF16), perm_ref[...], preferred_element_type=F32)
            gains = gain_ref[j // (n_rope // 2)]
        for ci in range(tm // rc):
            r, m0, p0 = ci // per_res, (ci % per_res) * rc, ci * rc
            if not rope:
                o_ref[0, r, m0:m0 + rc, c0:c0 + MXU_N] = acc_ref[p0:p0 + rc, :].astype(BF16)
                continue
            cos = cos_ref[0, r, m0:m0 + rc, :] * gains[0:1]
            sin = sin_ref[0, r, m0:m0 + rc, :] * gains[1:2]
            for hc in range(MXU_N // LANES):
                cols = slice(hc * LANES, (hc + 1) * LANES)
                a = acc_ref[p0:p0 + rc, cols]
                ms = jnp.sum(a * a, axis=-1, keepdims=True) * (1.0 / HEAD_DIM)
                rot = (a * cos + part_ref[p0:p0 + rc, cols] * sin) * lax.rsqrt(ms + NORM_EPS)
                o_ref[0, r, m0:m0 + rc, c0 + hc * LANES:c0 + (hc + 1) * LANES] = rot.astype(BF16)

    @pl.when(j < n_rope)
    def _():
        _pipelined_subtiles(tn // MXU_N, matmul, functools.partial(finish, rope=True))

    @pl.when(j >= n_rope)
    def _():
        _pipelined_subtiles(tn // MXU_N, matmul, functools.partial(finish, rope=False))


def _attn_proj(x, scale, shift, norm_g, w_in, layer, gains, tables, *, group, with_gate, tm=1024, tn=1024):
    B, S, D = x.shape
    d = DIL_GROUPS[group][1]
    n_sec = 4 if with_gate else 3
    per_sec = ATTN_OUT // tn
    tiles_m = S // tm

    def w_index(i, j):
        sec = j // per_sec
        base = jnp.where(sec < 3, sec * (ATTN_QKV // tn) + group * per_sec, 3 * (ATTN_QKV // tn))
        return (layer, 0, base + j % per_sec)

    tab_spec = pl.BlockSpec((1, d, tm // d, LANES), lambda i, j: (i // tiles_m, 0, i % tiles_m, 0))
    scratch = [pltpu.VMEM((tm, D), BF16)] + [pltpu.VMEM((tm, MXU_N), F32)] * 3
    if d > 1:
        scratch.append(pltpu.VMEM((3, tm, LANES), F32))
    return pl.pallas_call(
        functools.partial(_attn_proj_kernel, d=d, tm=tm, tn=tn, n_rope=2 * per_sec),
        out_shape=jax.ShapeDtypeStruct((B, d, S // d, n_sec * ATTN_OUT), BF16),
        grid=(B * tiles_m, n_sec * per_sec),
        in_specs=[pl.BlockSpec((1, tm, D), lambda i, j: (i // tiles_m, i % tiles_m, 0)),
                  pl.BlockSpec((1, 1, D), lambda i, j: (i // tiles_m, 0, 0)),
                  pl.BlockSpec((1, 1, D), lambda i, j: (i // tiles_m, 0, 0)),
                  pl.BlockSpec((1, D), lambda i, j: (0, 0)),
                  pl.BlockSpec((None, D, tn), w_index),
                  pl.BlockSpec((MXU_N, MXU_N), lambda i, j: (0, 0)),
                  pl.BlockSpec((2, 2, LANES), lambda i, j: (0, 0, 0)),
                  tab_spec, tab_spec],
        out_specs=pl.BlockSpec((1, d, tm // d, tn), lambda i, j: (i // tiles_m, 0, i % tiles_m, j)),
        scratch_shapes=scratch,
        compiler_params=_params(2),
        name=f"attn_proj_g{group}",
    )(x, scale, shift, norm_g, w_in, _rotate_half_matrix(), gains, *tables)


def _attn_block(q, k, v, bias):
    s = lax.dot_general(q, k, (((1,), (1,)), ((), ())), preferred_element_type=F32) + bias
    m = jnp.max(s, axis=-1, keepdims=True)
    p = jnp.exp(s - m).astype(BF16)
    v_ext = jnp.concatenate([v, jnp.ones(v.shape, BF16)], axis=1)
    ol = jnp.dot(p, v_ext, preferred_element_type=F32)
    l = ol[:, HEAD_DIM:]
    return ol[:, :HEAD_DIM] * (1.0 / l), m + jnp.log(l)


def _attn_kernel(q0, k0, v0, z0, q1, k1, v1, q2, k2, v2, y_ref, o_scr, lse_scr, *, seq):
    dist = (lax.broadcasted_iota(jnp.int32, (BLOCK, 2 * BLOCK), 0) + BLOCK
            - lax.broadcasted_iota(jnp.int32, (BLOCK, 2 * BLOCK), 1))
    bias_both = jnp.where(dist >= 0, jnp.where(dist <= BLOCK, 0.0, MASKED), MASKED).astype(F32)
    bias_first = bias_both[:, BLOCK:]
    row = lax.broadcasted_iota(jnp.int32, (2 * BLOCK, 2 * BLOCK), 0)
    col = lax.broadcasted_iota(jnp.int32, (2 * BLOCK, 2 * BLOCK), 1)
    same_seq = (row // BLOCK) == (col // BLOCK)
    bias_pair = jnp.where(same_seq, jnp.where(col <= row, 0.0, MASKED), MASKED).astype(F32)

    groups = ((q0, k0, v0), (q1, k1, v1), (q2, k2, v2))
    for g, (q_ref, k_ref, v_ref) in enumerate(groups):
        d = DIL_GROUPS[g][1]
        nb = seq // d // BLOCK

        def store(r, n, o, lse, g=g, d=d):
            if d == 1:
                rows = pl.ds(BLOCK * n, BLOCK)
            elif d <= SUB:
                rows = pl.ds(r + d * BLOCK * n, BLOCK, stride=d)
            else:
                rows = pl.ds((r % SUB) * (seq // SUB) + r // SUB + (d // SUB) * BLOCK * n, BLOCK,
                             stride=d // SUB)
            slot = g if d <= SUB else N_DIL
            o_scr[slot, rows, :] = o
            lse_scr[slot, rows, :] = lse

        def first_block(r, q_ref=q_ref, k_ref=k_ref, v_ref=v_ref, store=store):
            o, lse = _attn_block(q_ref[0, r, 0:BLOCK, :], k_ref[0, r, 0:BLOCK, :],
                                 v_ref[0, r, 0:BLOCK, :], bias_first)
            store(r, 0, o, lse)

        def later_block(r, n, q_ref=q_ref, k_ref=k_ref, v_ref=v_ref, store=store):
            q_rows, kv_rows = pl.ds(n * BLOCK, BLOCK), pl.ds((n - 1) * BLOCK, 2 * BLOCK)
            o, lse = _attn_block(q_ref[0, r, q_rows, :], k_ref[0, r, kv_rows, :], v_ref[0, r, kv_rows, :],
                                 bias_both)
            store(r, n, o, lse)

        def block_pair(i, q_ref=q_ref, k_ref=k_ref, v_ref=v_ref, store=store):
            def both(ref):
                return jnp.concatenate([ref[0, 2 * i], ref[0, 2 * i + 1]], axis=0)

            o, lse = _attn_block(both(q_ref), both(k_ref), both(v_ref), bias_pair)
            store(2 * i, 0, o[:BLOCK], lse[:BLOCK])
            store(2 * i + 1, 0, o[BLOCK:], lse[BLOCK:])

        if nb > 1:
            for r in range(d):
                first_block(r)
                for n in range(1, nb):
                    later_block(r, n)
        else:
            for i in range(d // 2):
                block_pair(i)

        if d > SUB:
            part = seq // SUB
            for r_lo in range(SUB):
                for scr in (o_scr, lse_scr):
                    scr[g, pl.ds(r_lo, part, stride=SUB), :] = scr[N_DIL, r_lo * part:(r_lo + 1) * part, :]

    rc = 256

    def combine(i, carry):
        rows = pl.ds(pl.multiple_of(i * rc, rc), rc)
        l0, l1, l2 = lse_scr[0, rows, :], lse_scr[1, rows, :], lse_scr[2, rows, :]
        m = jnp.maximum(jnp.maximum(l0, l1), l2)
        e0, e1, e2 = jnp.exp(l0 - m), jnp.exp(l1 - m), jnp.exp(l2 - m)
        o = (e0 * o_scr[0, rows, :] + e1 * o_scr[1, rows, :] + e2 * o_scr[2, rows, :]) / (e0 + e1 + e2)
        z = z0[0, 0, rows, :].astype(F32)
        y_ref[0, rows, :] = (o * _silu(z)).astype(BF16)
        return carry

    lax.fori_loop(0, seq // rc, combine, 0)


def _attention(p0, p1, p2):
    B, _, S, _ = p0.shape
    H = HEADS_PER_GROUP

    def spec(d, sec):
        return pl.BlockSpec((1, d, S // d, LANES), lambda b, h: (b, 0, 0, sec * H + h))

    in_specs = [spec(1, 0), spec(1, 1), spec(1, 2), spec(1, 3)]
    for d in (DIL_GROUPS[1][1], DIL_GROUPS[2][1]):
        in_specs += [spec(d, 0), spec(d, 1), spec(d, 2)]
    return pl.pallas_call(
        functools.partial(_attn_kernel, seq=S),
        out_shape=jax.ShapeDtypeStruct((B, S, ATTN_OUT), BF16),
        grid=(B, H),
        in_specs=in_specs,
        out_specs=pl.BlockSpec((1, S, LANES), lambda b, h: (b, 0, h)),
        scratch_shapes=[pltpu.VMEM((N_DIL + 1, S, LANES), F32), pltpu.VMEM((N_DIL + 1, S, LANES), F32)],
        compiler_params=_params(2),
        name="dilated_attention",
    )(p0, p0, p0, p0, p1, p1, p1, p2, p2, p2)


def _out_proj_kernel(y_ref, w_ref, x_ref, gate_ref, o_ref, acc_a, acc_b):
    bufs = (acc_a, acc_b)

    def matmul(c, slot):
        bufs[slot][...] = jnp.dot(y_ref[...], w_ref[:, c * MXU_N:(c + 1) * MXU_N],
                                  preferred_element_type=F32)

    def finish(c, slot):
        cols = slice(c * MXU_N, (c + 1) * MXU_N)
        o_ref[:, cols] = x_ref[:, cols] + gate_ref[0][:, cols] * bufs[slot][...]

    _pipelined_subtiles(o_ref.shape[1] // MXU_N, matmul, finish)


def _out_proj(y, w_bf16, x, gate, *, tm=512):
    B, S, D = x.shape
    M, K = y.shape
    tiles_m = S // tm
    out = pl.pallas_call(
        _out_proj_kernel,
        out_shape=jax.ShapeDtypeStruct((M, D), F32),
        grid=(M // tm,),
        in_specs=[pl.BlockSpec((tm, K), lambda i: (i, 0)),
                  pl.BlockSpec((K, D), lambda i: (0, 0), pipeline_mode=pl.Buffered(1)),
                  pl.BlockSpec((tm, D), lambda i: (i, 0)),
                  pl.BlockSpec((1, 1, D), lambda i: (i // tiles_m, 0, 0))],
        out_specs=pl.BlockSpec((tm, D), lambda i: (i, 0)),
        scratch_shapes=[pltpu.VMEM((tm, MXU_N), F32)] * 2,
        compiler_params=_params(1),
        name="out_proj",
    )(y, w_bf16, x.reshape(M, D), gate)
    return out.reshape(B, S, D)


def _sgu_proj_kernel(x_ref, sc_ref, sh_ref, g_ref, w_ref, o_ref, h_ref, acc_a, acc_b, *, tm, tn, n_gelu):
    j = pl.program_id(1)
    bufs = (acc_a, acc_b)

    @pl.when(j == 0)
    def _():
        _norm_prologue(x_ref, sc_ref, sh_ref, g_ref, h_ref, None, d=1, tm=tm)

    def matmul(c, slot):
        bufs[slot][...] = jnp.dot(h_ref[...], w_ref[:, c * MXU_N:(c + 1) * MXU_N].astype(BF16),
                                  preferred_element_type=F32)

    def finish(c, slot, gelu):
        a = bufs[slot][...]
        o_ref[:, c * MXU_N:(c + 1) * MXU_N] = (_gelu_tanh(a) if gelu else a).astype(BF16)

    @pl.when(j < n_gelu)
    def _():
        _pipelined_subtiles(tn // MXU_N, matmul, functools.partial(finish, gelu=True))

    @pl.when(j >= n_gelu)
    def _():
        _pipelined_subtiles(tn // MXU_N, matmul, functools.partial(finish, gelu=False))


def _sgu_proj(x, scale, shift, norm_g, w_in, layer, *, tm=1024, tn=1024):
    B, S, D = x.shape
    N = w_in.shape[2]
    tiles_m = S // tm
    return pl.pallas_call(
        functools.partial(_sgu_proj_kernel, tm=tm, tn=tn, n_gelu=2 * SGU_WIDTH // tn),
        out_shape=jax.ShapeDtypeStruct((B * S, N), BF16),
        grid=(B * tiles_m, N // tn),
        in_specs=[pl.BlockSpec((1, tm, D), lambda i, j: (i // tiles_m, i % tiles_m, 0)),
                  pl.BlockSpec((1, 1, D), lambda i, j: (i // tiles_m, 0, 0)),
                  pl.BlockSpec((1, 1, D), lambda i, j: (i // tiles_m, 0, 0)),
                  pl.BlockSpec((1, D), lambda i, j: (0, 0)),
                  pl.BlockSpec((None, D, tn), lambda i, j: (layer, 0, j))],
        out_specs=pl.BlockSpec((tm, tn), lambda i, j: (i, j)),
        scratch_shapes=[pltpu.VMEM((tm, D), BF16)] + [pltpu.VMEM((tm, MXU_N), F32)] * 2,
        compiler_params=_params(2),
        name="sgu_proj",
    )(x, scale, shift, norm_g, w_in)


def _sgu_core_kernel(u_ref, v_ref, z_ref, lng_ref, lnb_ref, ws_ref, bst_ref, y_ref, wm_ref, vn_ref, *, tc):
    t_out = lax.broadcasted_iota(jnp.int32, (SGU_CHUNK, SGU_CHUNK), 0)
    s_in = lax.broadcasted_iota(jnp.int32, (SGU_CHUNK, SGU_CHUNK), 1)
    causal = s_in <= t_out
    for g in range(SGU_GROUPS):
        wm_ref[g] = jnp.where(causal, ws_ref[g], 0.0).astype(BF16)
    gw = SGU_WIDTH // SGU_GROUPS

    def chunk(ci, carry):
        rows = pl.ds(pl.multiple_of(ci * SGU_CHUNK, SGU_CHUNK), SGU_CHUNK)
        v = v_ref[rows, :].astype(F32)
        mu = jnp.mean(v, axis=-1, keepdims=True)
        vc = v - mu
        var = jnp.mean(vc * vc, axis=-1, keepdims=True)
        vn_ref[...] = (vc * lax.rsqrt(var + NORM_EPS) * lng_ref[...] + lnb_ref[...]).astype(BF16)
        for g in range(SGU_GROUPS):
            cols = slice(g * gw, (g + 1) * gw)
            sv = jnp.dot(wm_ref[g], vn_ref[:, cols], preferred_element_type=F32) + bst_ref[:, g:g + 1]
            z = z_ref[rows, cols].astype(F32)
            y_ref[rows, cols] = (u_ref[rows, cols].astype(F32) * sv * _silu(z)).astype(BF16)
        return carry

    lax.fori_loop(0, tc // SGU_CHUNK, chunk, 0)


def _sgu_core(proj, ln_g, ln_b, ws, bs, *, tc=256):
    M = proj.shape[0]
    E = SGU_WIDTH
    return pl.pallas_call(
        functools.partial(_sgu_core_kernel, tc=tc),
        out_shape=jax.ShapeDtypeStruct((M, E), BF16),
        grid=(M // tc,),
        in_specs=[pl.BlockSpec((tc, E), lambda i: (i, 0)),
                  pl.BlockSpec((tc, E), lambda i: (i, 1)),
                  pl.BlockSpec((tc, E), lambda i: (i, 2)),
                  pl.BlockSpec((1, E), lambda i: (0, 0)),
                  pl.BlockSpec((1, E), lambda i: (0, 0)),
                  pl.BlockSpec((SGU_GROUPS, SGU_CHUNK, SGU_CHUNK), lambda i: (0, 0, 0)),
                  pl.BlockSpec((SGU_CHUNK, SGU_GROUPS), lambda i: (0, 0))],
        out_specs=pl.BlockSpec((tc, E), lambda i: (i, 0)),
        scratch_shapes=[pltpu.VMEM((SGU_GROUPS, SGU_CHUNK, SGU_CHUNK), BF16),
                        pltpu.VMEM((SGU_CHUNK, E), BF16)],
        compiler_params=_params(1),
        name="sgu_core",
    )(proj, proj, proj, ln_g.reshape(1, E), ln_b.reshape(1, E), ws, bs.T)


def _conv_mix_kernel(x_ref, sc_ref, sh_ref, g_ref, wa_ref, wb_ref, dw_ref, db_ref, lng_ref, lnb_ref,
                     y_ref, h_ref, acc_a, acc_b, gate_a, gate_b, ext_ref, halo_ref, cv_ref, s1_ref, s2_ref,
                     *, tm, tn, n_glu, tiles_per_seq):
    i, j = pl.program_id(0), pl.program_id(1)
    per_sub = MXU_N // LANES
    per_step = tn // LANES
    accs, gates = (acc_a, acc_b), (gate_a, gate_b)
    first_of_seq = (i % tiles_per_seq) == 0
    lead = HALO - (CONV_K - 1)
    rc = 64

    @pl.when(j == 0)
    def _():
        _norm_prologue(x_ref, sc_ref, sh_ref, g_ref, h_ref, None, d=1, tm=tm)
        s1_ref[...] = jnp.zeros_like(s1_ref)
        s2_ref[...] = jnp.zeros_like(s2_ref)

    def matmul(c, slot, glu):
        cols = slice(c * MXU_N, (c + 1) * MXU_N)
        accs[slot][...] = jnp.dot(h_ref[...], wa_ref[:, cols], preferred_element_type=F32)
        if glu:
            gates[slot][...] = jnp.dot(h_ref[...], wb_ref[:, cols], preferred_element_type=F32)

    def finish_glu(c, slot):
        for s in range(per_sub):
            lanes = slice(s * LANES, (s + 1) * LANES)
            slab = j * per_step + c * per_sub + s
            stage = (c % 2) * per_sub + s
            ext_ref[stage, 0:HALO, :] = jnp.where(first_of_seq, 0.0, halo_ref[slab])
            for r0 in range(0, tm, rc):
                ext_ref[stage, HALO + r0:HALO + r0 + rc, :] = (
                    accs[slot][r0:r0 + rc, lanes] * _sigmoid(gates[slot][r0:r0 + rc, lanes]))
            halo_ref[slab] = ext_ref[stage, tm:tm + HALO, :]
            for r0 in range(0, tm, rc):
                acc = jnp.broadcast_to(db_ref[slab], (rc, LANES))
                for k in range(CONV_K):
                    acc = acc + dw_ref[slab, k:k + 1, :] * ext_ref[stage, r0 + lead + k:r0 + lead + k + rc, :]
                cv_ref[slab, r0:r0 + rc, :] = acc.astype(BF16)
                s1_ref[r0:r0 + rc, :] += acc
                s2_ref[r0:r0 + rc, :] += acc * acc

    def finish_z(c, slot):
        mu, inv = s1_ref[...], s2_ref[...]
        for s in range(per_sub):
            lanes = slice(s * LANES, (s + 1) * LANES)
            slab = (j - n_glu) * per_step + c * per_sub + s
            t = (cv_ref[slab].astype(F32) - mu) * inv * lng_ref[slab] + lnb_ref[slab]
            y_ref[:, c * MXU_N + s * LANES:c * MXU_N + (s + 1) * LANES] = (
                _silu(t) * _silu(accs[slot][:, lanes])).astype(BF16)

    @pl.when(j < n_glu)
    def _():
        _pipelined_subtiles(tn // MXU_N, functools.partial(matmul, glu=True), finish_glu)

    @pl.when(j == n_glu)
    def _():
        inv_e = 1.0 / CONV_WIDTH
        mu = jnp.sum(s1_ref[...], axis=-1, keepdims=True) * inv_e
        var = jnp.maximum(jnp.sum(s2_ref[...], axis=-1, keepdims=True) * inv_e - mu * mu, 0.0)
        s1_ref[...] = jnp.broadcast_to(mu, s1_ref.shape)
        s2_ref[...] = jnp.broadcast_to(lax.rsqrt(var + NORM_EPS), s2_ref.shape)

    @pl.when(j >= n_glu)
    def _():
        _pipelined_subtiles(tn // MXU_N, functools.partial(matmul, glu=False), finish_z)


def _conv_mix(x, scale, shift, norm_g, w_bf16, dw_w, dw_b, ln_g, ln_b, *, tm=512, tn=1024):
    B, S, D = x.shape
    E = CONV_WIDTH
    nj = E // tn
    n_slabs = E // LANES
    tiles_m = S // tm
    w_slabs = dw_w.reshape(CONV_K, n_slabs, LANES).transpose(1, 0, 2)

    def slabs(v):
        return v.reshape(n_slabs, 1, LANES)

    def whole(shape):
        return pl.BlockSpec(shape, lambda i, j: (0,) * len(shape))

    return pl.pallas_call(
        functools.partial(_conv_mix_kernel, tm=tm, tn=tn, n_glu=nj, tiles_per_seq=tiles_m),
        out_shape=jax.ShapeDtypeStruct((B * S, E), BF16),
        grid=(B * tiles_m, 2 * nj),
        in_specs=[pl.BlockSpec((1, tm, D), lambda i, j: (i // tiles_m, i % tiles_m, 0)),
                  pl.BlockSpec((1, 1, D), lambda i, j: (i // tiles_m, 0, 0)),
                  pl.BlockSpec((1, 1, D), lambda i, j: (i // tiles_m, 0, 0)),
                  pl.BlockSpec((1, D), lambda i, j: (0, 0)),
                  pl.BlockSpec((D, tn), lambda i, j: (0, jnp.where(j < nj, j, nj + j))),
                  pl.BlockSpec((D, tn), lambda i, j: (0, jnp.minimum(nj + j, 2 * nj - 1))),
                  whole((n_slabs, CONV_K, LANES)), whole((n_slabs, 1, LANES)),
                  whole((n_slabs, 1, LANES)), whole((n_slabs, 1, LANES))],
        out_specs=pl.BlockSpec((tm, tn), lambda i, j: (i, jnp.maximum(j - nj, 0))),
        scratch_shapes=[pltpu.VMEM((tm, D), BF16)] + [pltpu.VMEM((tm, MXU_N), F32)] * 4
        + [pltpu.VMEM((2 * MXU_N // LANES, tm + HALO, LANES), F32),
           pltpu.VMEM((n_slabs, HALO, LANES), F32),
           pltpu.VMEM((n_slabs, tm, LANES), BF16),
           pltpu.VMEM((tm, LANES), F32), pltpu.VMEM((tm, LANES), F32)],
        compiler_params=_params(2),
        name="conv_mix",
    )(x, scale, shift, norm_g, w_bf16, w_bf16, w_slabs, slabs(dw_b), slabs(ln_g), slabs(ln_b))


def _rotate_half_matrix():
    half = ROPE_DIM // 2
    k = jnp.arange(HEAD_DIM)[:, None]
    j = jnp.arange(HEAD_DIM)[None, :]
    p = (jnp.where((j < half) & (k == j + half), -1.0, 0.0)
         + jnp.where((j >= half) & (j < ROPE_DIM) & (k == j - half), 1.0, 0.0))
    return jnp.kron(jnp.eye(MXU_N // HEAD_DIM, dtype=F32), p).astype(BF16)


def _rope_tables(positions):
    B, S = positions.shape
    inv_freq = jnp.power(ROPE_THETA, -jnp.arange(0, ROPE_DIM, 2, dtype=F32) / ROPE_DIM)
    ang = positions.astype(F32)[..., None] * inv_freq
    cos, sin = lax.optimization_barrier((jnp.cos(ang), jnp.sin(ang)))
    rest = HEAD_DIM - ROPE_DIM
    cos_t = jnp.concatenate([cos, cos, jnp.ones((B, S, rest), F32)], axis=-1)
    sin_t = jnp.concatenate([sin, sin, jnp.zeros((B, S, rest), F32)], axis=-1)
    return [tuple(t.reshape(B, S // d, d, HEAD_DIM).transpose(0, 2, 1, 3) for t in (cos_t, sin_t))
            for _, d in DIL_GROUPS]


def _rope_gains(q_gain, k_gain):
    half = ROPE_DIM // 2
    gains = jnp.stack([q_gain, k_gain])
    lane = jnp.arange(HEAD_DIM)
    partner = jnp.where(lane < half, jnp.roll(gains, -half, axis=-1), jnp.roll(gains, half, axis=-1))
    return jnp.stack([gains, partner], axis=1)


def kernel(x, c, positions, ada_w, ada_b, norm_g, attn_w_in, attn_q_gain, attn_k_gain, attn_w_out,
           sgu_w_in, sgu_ln_g, sgu_ln_b, sgu_ws, sgu_bs, sgu_w_out,
           conv_w_in, conv_dw_w, conv_dw_b, conv_ln_g, conv_ln_b, conv_w_out):
    B, S, D = x.shape
    mod = _ada_modulation(c, ada_w, ada_b)
    tables = _rope_tables(positions)
    for i in range(DEPTH):
        shift = mod[i, :, None, :D]
        scale = mod[i, :, None, D:2 * D]
        gate = mod[i, :, None, 2 * D:]
        g_row = norm_g[i].reshape(1, D)
        kind, l = i % N_MIXERS, i // N_MIXERS
        if kind == 0:
            gains = _rope_gains(attn_q_gain[l] * (HEAD_DIM ** -0.5), attn_k_gain[l])
            proj = [_attn_proj(x, scale, shift, g_row, attn_w_in, l, gains, tables[g], group=g,
                               with_gate=(g == 0)) for g in range(N_DIL)]
            y = _attention(*proj).reshape(B * S, ATTN_OUT)
            x = _out_proj(y, attn_w_out[l].astype(BF16), x, gate)
        elif kind == 1:
            proj = _sgu_proj(x, scale, shift, g_row, sgu_w_in, l)
            y = _sgu_core(proj, sgu_ln_g[l], sgu_ln_b[l], sgu_ws[l], sgu_bs[l])
            x = _out_proj(y, sgu_w_out[l].astype(BF16), x, gate)
        else:
            y = _conv_mix(x, scale, shift, g_row, conv_w_in[l].astype(BF16), conv_dw_w[l], conv_dw_b[l],
                          conv_ln_g[l], conv_ln_b[l])
            x = _out_proj(y, conv_w_out[l].astype(BF16), x, gate)
    return x
```

```python
import functools

import jax
import jax.numpy as jnp
from jax import lax
from jax.experimental import pallas as pl
from jax.experimental.pallas import tpu as pltpu

D_MODEL = 2048
DEPTH = 4
N_MIXERS = 3
HEAD_DIM = 128
DIL_GROUPS = ((128, 1), (512, 4), (2048, 16))
N_DIL = len(DIL_GROUPS)
HEADS_PER_GROUP = D_MODEL // HEAD_DIM
ATTN_OUT = HEADS_PER_GROUP * HEAD_DIM
ATTN_QKV = N_DIL * ATTN_OUT
ROPE_DIM = HEAD_DIM // 4
ROPE_THETA = 500000.0
BLOCK = 128
SGU_WIDTH = 2 * D_MODEL
SGU_CHUNK = 128
SGU_GROUPS = 16
CONV_WIDTH = 2 * D_MODEL
CONV_K = 31
NORM_EPS = 1e-6

LANES = 128
MXU_N = 256
SUB = 4
HALO = 32
VMEM_LIMIT = 56 * 1024 * 1024
MASKED = -1e30

PROJ_TM, PROJ_TN = 1024, 1024
CONV_TM = 512
OUT_TM = 512
SGU_TC = 512
ADA_ROWS, ADA_TN = 16, 1024
NORM_ROWS = 128
ROPE_ROWS = 32
CONV_ROWS = 64
COMBINE_ROWS = 256

F32 = jnp.float32
BF16 = jnp.bfloat16


def _params(n_axes):
    return pltpu.CompilerParams(dimension_semantics=("arbitrary",) * n_axes,
                                vmem_limit_bytes=VMEM_LIMIT)


def _sigmoid(x):
    return 1.0 / (1.0 + jnp.exp(-x))


def _silu(x):
    return x * _sigmoid(x)


def _gelu_tanh(x):
    return x * (0.5 * (1.0 + jnp.tanh(0.7978845608028654 * (x + 0.044715 * (x * x * x)))))


def _pipelined_subtiles(n_sub, matmul, finish):
    for c in range(n_sub + 1):
        if c < n_sub:
            matmul(c, c % 2)
        if c >= 1:
            finish(c - 1, (c - 1) % 2)


def _split_bf16(v):
    hi = v.astype(BF16)
    return hi, (v - hi.astype(F32)).astype(BF16)


def _ada_kernel(c_ref, w_ref, b_ref, o_ref):
    c_hi, c_lo = _split_bf16(_silu(c_ref[...]))
    w_hi, w_lo = _split_bf16(w_ref[0])
    rows = c_hi.shape[0]
    both = jnp.dot(jnp.concatenate([c_hi, c_lo], axis=0), w_hi, preferred_element_type=F32)
    o_ref[0] = (both[:rows] + both[rows:] + jnp.dot(c_hi, w_lo, preferred_element_type=F32)
                + b_ref[0])


def _ada_modulation(c, ada_w, ada_b):
    B, D = c.shape
    rows, tn = ADA_ROWS, ADA_TN
    c8 = jnp.pad(c, ((0, rows - B), (0, 0)))
    out = pl.pallas_call(
        _ada_kernel,
        out_shape=jax.ShapeDtypeStruct((DEPTH, rows, 3 * D), F32),
        grid=(DEPTH, 3 * D // tn),
        in_specs=[pl.BlockSpec((rows, D), lambda l, j: (0, 0)),
                  pl.BlockSpec((1, D, tn), lambda l, j: (l, 0, j)),
                  pl.BlockSpec((1, 1, tn), lambda l, j: (l, 0, j))],
        out_specs=pl.BlockSpec((1, rows, tn), lambda l, j: (l, 0, j)),
        compiler_params=_params(2),
        name="ada_modulation",
    )(c8, ada_w, ada_b.reshape(DEPTH, 1, 3 * D))
    return out[:, :B]


def _norm_prologue(x_ref, sc_ref, sh_ref, g_ref, h_ref, work_ref, *, d, tm):
    rc = NORM_ROWS
    a = g_ref[...] * (1.0 + sc_ref[0])
    sh = sh_ref[0]

    def chunk_rows(i):
        return pl.ds(pl.multiple_of(i * rc, rc), rc)

    def inv_rms(x):
        return lax.rsqrt(jnp.mean(x * x, axis=-1, keepdims=True) + NORM_EPS)

    if d == 1:
        def body(i, carry):
            x = x_ref[0, chunk_rows(i), :]
            h_ref[chunk_rows(i), :] = (x * inv_rms(x) * a + sh).astype(BF16)
            return carry

        lax.fori_loop(0, tm // rc, body, 0)
        return

    INV, NAT, TMP = 0, 1, 2

    def stats(i, carry):
        work_ref[INV, chunk_rows(i), :] = jnp.broadcast_to(inv_rms(x_ref[0, chunk_rows(i), :]), (rc, LANES))
        return carry

    lax.fori_loop(0, tm // rc, stats, 0)
    rows = tm // d
    for c in range(D_MODEL // LANES):
        cols = slice(c * LANES, (c + 1) * LANES)
        work_ref[NAT] = x_ref[0, :, cols] * work_ref[INV] * a[:, cols] + sh[:, cols]
        if d <= SUB:
            for r in range(d):
                h_ref[r * rows:(r + 1) * rows, cols] = work_ref[NAT, pl.ds(r, rows, stride=d), :].astype(BF16)
            continue
        part = tm // SUB
        for r_lo in range(SUB):
            work_ref[TMP, r_lo * part:(r_lo + 1) * part, :] = work_ref[NAT, pl.ds(r_lo, part, stride=SUB), :]
        for r in range(d):
            src = pl.ds((r % SUB) * part + r // SUB, rows, stride=d // SUB)
            h_ref[r * rows:(r + 1) * rows, cols] = work_ref[TMP, src, :].astype(BF16)


def _attn_proj_kernel(x_ref, sc_ref, sh_ref, g_ref, w_ref, perm_ref, gain_ref, cos_ref, sin_ref,
                      o_ref, h_ref, acc_a, acc_b, part_ref, *work, d, tm, tn, n_rope):
    j = pl.program_id(1)
    rows = tm // d
    rc = min(ROPE_ROWS, rows)
    per_res = rows // rc
    bufs = (acc_a, acc_b)

    @pl.when(j == 0)
    def _():
        _norm_prologue(x_ref, sc_ref, sh_ref, g_ref, h_ref, work[0] if work else None, d=d, tm=tm)

    def matmul(c, slot):
        bufs[slot][...] = jnp.dot(h_ref[...], w_ref[:, c * MXU_N:(c + 1) * MXU_N].astype(BF16),
                                  preferred_element_type=F32)

    def finish(c, slot, rope):
        acc_ref, c0 = bufs[slot], c * MXU_N
        if rope:
            part_ref[...] = jnp.dot(acc_ref[...].astype(BF16), perm_ref[...], preferred_element_type=F32)
            gains = gain_ref[j // (n_rope // 2)]
        for ci in range(tm // rc):
            r, m0, p0 = ci // per_res, (ci % per_res) * rc, ci * rc
            if not rope:
                o_ref[0, r, m0:m0 + rc, c0:c0 + MXU_N] = acc_ref[p0:p0 + rc, :].astype(BF16)
                continue
            cos = cos_ref[0, r, m0:m0 + rc, :] * gains[0:1]
            sin = sin_ref[0, r, m0:m0 + rc, :] * gains[1:2]
            for hc in range(MXU_N // LANES):
                cols = slice(hc * LANES, (hc + 1) * LANES)
                a = acc_ref[p0:p0 + rc, cols]
                ms = jnp.sum(a * a, axis=-1, keepdims=True) * (1.0 / HEAD_DIM)
                rot = (a * cos + part_ref[p0:p0 + rc, cols] * sin) * lax.rsqrt(ms + NORM_EPS)
                o_ref[0, r, m0:m0 + rc, c0 + hc * LANES:c0 + (hc + 1) * LANES] = rot.astype(BF16)

    @pl.when(j < n_rope)
    def _():
        _pipelined_subtiles(tn // MXU_N, matmul, functools.partial(finish, rope=True))

    @pl.when(j >= n_rope)
    def _():
        _pipelined_subtiles(tn // MXU_N, matmul, functools.partial(finish, rope=False))


def _attn_proj(x, scale, shift, norm_g, w_in, layer, gains, tables, *, group, with_gate, tm=PROJ_TM, tn=PROJ_TN):
    B, S, D = x.shape
    d = DIL_GROUPS[group][1]
    n_sec = 4 if with_gate else 3
    per_sec = ATTN_OUT // tn
    tiles_m = S // tm

    def w_index(i, j):
        sec = j // per_sec
        base = jnp.where(sec < 3, sec * (ATTN_QKV // tn) + group * per_sec, 3 * (ATTN_QKV // tn))
        return (layer, 0, base + j % per_sec)

    tab_spec = pl.BlockSpec((1, d, tm // d, LANES), lambda i, j: (i // tiles_m, 0, i % tiles_m, 0))
    scratch = [pltpu.VMEM((tm, D), BF16)] + [pltpu.VMEM((tm, MXU_N), F32)] * 3
    if d > 1:
        scratch.append(pltpu.VMEM((3, tm, LANES), F32))
    return pl.pallas_call(
        functools.partial(_attn_proj_kernel, d=d, tm=tm, tn=tn, n_rope=2 * per_sec),
        out_shape=jax.ShapeDtypeStruct((B, d, S // d, n_sec * ATTN_OUT), BF16),
        grid=(B * tiles_m, n_sec * per_sec),
        in_specs=[pl.BlockSpec((1, tm, D), lambda i, j: (i // tiles_m, i % tiles_m, 0)),
                  pl.BlockSpec((1, 1, D), lambda i, j: (i // tiles_m, 0, 0)),
                  pl.BlockSpec((1, 1, D), lambda i, j: (i // tiles_m, 0, 0)),
                  pl.BlockSpec((1, D), lambda i, j: (0, 0)),
                  pl.BlockSpec((None, D, tn), w_index),
                  pl.BlockSpec((MXU_N, MXU_N), lambda i, j: (0, 0)),
                  pl.BlockSpec((2, 2, LANES), lambda i, j: (0, 0, 0)),
                  tab_spec, tab_spec],
        out_specs=pl.BlockSpec((1, d, tm // d, tn), lambda i, j: (i // tiles_m, 0, i % tiles_m, j)),
        scratch_shapes=scratch,
        compiler_params=_params(2),
        name=f"attn_proj_g{group}",
    )(x, scale, shift, norm_g, w_in, _rotate_half_matrix(), gains, *tables)


def _attn_block(q, k, v, bias):
    s = lax.dot_general(q, k, (((1,), (1,)), ((), ())), preferred_element_type=F32) + bias
    m = jnp.max(s, axis=-1, keepdims=True)
    p = jnp.exp(s - m).astype(BF16)
    v_ext = jnp.concatenate([v, jnp.ones(v.shape, BF16)], axis=1)
    ol = jnp.dot(p, v_ext, preferred_element_type=F32)
    l = ol[:, HEAD_DIM:]
    return ol[:, :HEAD_DIM] * (1.0 / l), m + jnp.log(l)


def _attn_kernel(q0, k0, v0, z0, q1, k1, v1, q2, k2, v2, y_ref, o_scr, lse_scr, *, seq):
    dist = (lax.broadcasted_iota(jnp.int32, (BLOCK, 2 * BLOCK), 0) + BLOCK
            - lax.broadcasted_iota(jnp.int32, (BLOCK, 2 * BLOCK), 1))
    bias_both = jnp.where(dist >= 0, jnp.where(dist <= BLOCK, 0.0, MASKED), MASKED).astype(F32)
    bias_first = bias_both[:, BLOCK:]
    row = lax.broadcasted_iota(jnp.int32, (2 * BLOCK, 2 * BLOCK), 0)
    col = lax.broadcasted_iota(jnp.int32, (2 * BLOCK, 2 * BLOCK), 1)
    same_seq = (row // BLOCK) == (col // BLOCK)
    bias_pair = jnp.where(same_seq, jnp.where(col <= row, 0.0, MASKED), MASKED).astype(F32)

    groups = ((q0, k0, v0), (q1, k1, v1), (q2, k2, v2))
    for g, (q_ref, k_ref, v_ref) in enumerate(groups):
        d = DIL_GROUPS[g][1]
        nb = seq // d // BLOCK

        def store(r, n, o, lse, g=g, d=d):
            if d == 1:
                rows = pl.ds(BLOCK * n, BLOCK)
            elif d <= SUB:
                rows = pl.ds(r + d * BLOCK * n, BLOCK, stride=d)
            else:
                rows = pl.ds((r % SUB) * (seq // SUB) + r // SUB + (d // SUB) * BLOCK * n, BLOCK,
                             stride=d // SUB)
            slot = g if d <= SUB else N_DIL
            o_scr[slot, rows, :] = o
            lse_scr[slot, rows, :] = lse

        def first_block(r, q_ref=q_ref, k_ref=k_ref, v_ref=v_ref, store=store):
            o, lse = _attn_block(q_ref[0, r, 0:BLOCK, :], k_ref[0, r, 0:BLOCK, :],
                                 v_ref[0, r, 0:BLOCK, :], bias_first)
            store(r, 0, o, lse)

        def later_block(r, n, q_ref=q_ref, k_ref=k_ref, v_ref=v_ref, store=store):
            q_rows, kv_rows = pl.ds(n * BLOCK, BLOCK), pl.ds((n - 1) * BLOCK, 2 * BLOCK)
            o, lse = _attn_block(q_ref[0, r, q_rows, :], k_ref[0, r, kv_rows, :], v_ref[0, r, kv_rows, :],
                                 bias_both)
            store(r, n, o, lse)

        def block_pair(i, q_ref=q_ref, k_ref=k_ref, v_ref=v_ref, store=store):
            def both(ref):
                return jnp.concatenate([ref[0, 2 * i], ref[0, 2 * i + 1]], axis=0)

            o, lse = _attn_block(both(q_ref), both(k_ref), both(v_ref), bias_pair)
            store(2 * i, 0, o[:BLOCK], lse[:BLOCK])
            store(2 * i + 1, 0, o[BLOCK:], lse[BLOCK:])

        if nb > 1:
            for r in range(d):
                first_block(r)
                for n in range(1, nb):
                    later_block(r, n)
        else:
            for i in range(d // 2):
                block_pair(i)

        if d > SUB:
            part = seq // SUB
            for r_lo in range(SUB):
                for scr in (o_scr, lse_scr):
                    scr[g, pl.ds(r_lo, part, stride=SUB), :] = scr[N_DIL, r_lo * part:(r_lo + 1) * part, :]

    rc = COMBINE_ROWS

    def combine(i, carry):
        rows = pl.ds(pl.multiple_of(i * rc, rc), rc)
        l0, l1, l2 = lse_scr[0, rows, :], lse_scr[1, rows, :], lse_scr[2, rows, :]
        m = jnp.maximum(jnp.maximum(l0, l1), l2)
        e0, e1, e2 = jnp.exp(l0 - m), jnp.exp(l1 - m), jnp.exp(l2 - m)
        o = (e0 * o_scr[0, rows, :] + e1 * o_scr[1, rows, :] + e2 * o_scr[2, rows, :]) / (e0 + e1 + e2)
        z = z0[0, 0, rows, :].astype(F32)
        y_ref[0, rows, :] = (o * _silu(z)).astype(BF16)
        return carry

    lax.fori_loop(0, seq // rc, combine, 0)


def _attention(p0, p1, p2):
    B, _, S, _ = p0.shape
    H = HEADS_PER_GROUP

    def spec(d, sec):
        return pl.BlockSpec((1, d, S // d, LANES), lambda b, h: (b, 0, 0, sec * H + h))

    in_specs = [spec(1, 0), spec(1, 1), spec(1, 2), spec(1, 3)]
    for d in (DIL_GROUPS[1][1], DIL_GROUPS[2][1]):
        in_specs += [spec(d, 0), spec(d, 1), spec(d, 2)]
    return pl.pallas_call(
        functools.partial(_attn_kernel, seq=S),
        out_shape=jax.ShapeDtypeStruct((B, S, ATTN_OUT), BF16),
        grid=(B, H),
        in_specs=in_specs,
        out_specs=pl.BlockSpec((1, S, LANES), lambda b, h: (b, 0, h)),
        scratch_shapes=[pltpu.VMEM((N_DIL + 1, S, LANES), F32), pltpu.VMEM((N_DIL + 1, S, LANES), F32)],
        compiler_params=_params(2),
        name="dilated_attention",
    )(p0, p0, p0, p0, p1, p1, p1, p2, p2, p2)


def _out_proj_kernel(y_ref, w_ref, x_ref, gate_ref, o_ref, acc_a, acc_b):
    bufs = (acc_a, acc_b)

    def matmul(c, slot):
        bufs[slot][...] = jnp.dot(y_ref[...], w_ref[:, c * MXU_N:(c + 1) * MXU_N],
                                  preferred_element_type=F32)

    def finish(c, slot):
        cols = slice(c * MXU_N, (c + 1) * MXU_N)
        o_ref[:, cols] = x_ref[:, cols] + gate_ref[0][:, cols] * bufs[slot][...]

    _pipelined_subtiles(o_ref.shape[1] // MXU_N, matmul, finish)


def _out_proj(y, w_bf16, x, gate, *, tm=OUT_TM):
    B, S, D = x.shape
    M, K = y.shape
    tiles_m = S // tm
    out = pl.pallas_call(
        _out_proj_kernel,
        out_shape=jax.ShapeDtypeStruct((M, D), F32),
        grid=(M // tm,),
        in_specs=[pl.BlockSpec((tm, K), lambda i: (i, 0)),
                  pl.BlockSpec((K, D), lambda i: (0, 0), pipeline_mode=pl.Buffered(1)),
                  pl.BlockSpec((tm, D), lambda i: (i, 0)),
                  pl.BlockSpec((1, 1, D), lambda i: (i // tiles_m, 0, 0))],
        out_specs=pl.BlockSpec((tm, D), lambda i: (i, 0)),
        scratch_shapes=[pltpu.VMEM((tm, MXU_N), F32)] * 2,
        compiler_params=_params(1),
        name="out_proj",
    )(y, w_bf16, x.reshape(M, D), gate)
    return out.reshape(B, S, D)


def _sgu_proj_kernel(x_ref, sc_ref, sh_ref, g_ref, w_ref, o_ref, h_ref, acc_a, acc_b, *, tm, tn, n_gelu):
    j = pl.program_id(1)
    bufs = (acc_a, acc_b)

    @pl.when(j == 0)
    def _():
        _norm_prologue(x_ref, sc_ref, sh_ref, g_ref, h_ref, None, d=1, tm=tm)

    def matmul(c, slot):
        bufs[slot][...] = jnp.dot(h_ref[...], w_ref[:, c * MXU_N:(c + 1) * MXU_N].astype(BF16),
                                  preferred_element_type=F32)

    def finish(c, slot, gelu):
        a = bufs[slot][...]
        o_ref[:, c * MXU_N:(c + 1) * MXU_N] = (_gelu_tanh(a) if gelu else a).astype(BF16)

    @pl.when(j < n_gelu)
    def _():
        _pipelined_subtiles(tn // MXU_N, matmul, functools.partial(finish, gelu=True))

    @pl.when(j >= n_gelu)
    def _():
        _pipelined_subtiles(tn // MXU_N, matmul, functools.partial(finish, gelu=False))


def _sgu_proj(x, scale, shift, norm_g, w_in, layer, *, tm=PROJ_TM, tn=PROJ_TN):
    B, S, D = x.shape
    N = w_in.shape[2]
    tiles_m = S // tm
    return pl.pallas_call(
        functools.partial(_sgu_proj_kernel, tm=tm, tn=tn, n_gelu=2 * SGU_WIDTH // tn),
        out_shape=jax.ShapeDtypeStruct((B * S, N), BF16),
        grid=(B * tiles_m, N // tn),
        in_specs=[pl.BlockSpec((1, tm, D), lambda i, j: (i // tiles_m, i % tiles_m, 0)),
                  pl.BlockSpec((1, 1, D), lambda i, j: (i // tiles_m, 0, 0)),
                  pl.BlockSpec((1, 1, D), lambda i, j: (i // tiles_m, 0, 0)),
                  pl.BlockSpec((1, D), lambda i, j: (0, 0)),
                  pl.BlockSpec((None, D, tn), lambda i, j: (layer, 0, j))],
        out_specs=pl.BlockSpec((tm, tn), lambda i, j: (i, j)),
        scratch_shapes=[pltpu.VMEM((tm, D), BF16)] + [pltpu.VMEM((tm, MXU_N), F32)] * 2,
        compiler_params=_params(2),
        name="sgu_proj",
    )(x, scale, shift, norm_g, w_in)


def _sgu_core_kernel(u_ref, v_ref, z_ref, lng_ref, lnb_ref, ws_ref, bst_ref, y_ref, wm_ref, vn_ref, *, tc):
    t_out = lax.broadcasted_iota(jnp.int32, (SGU_CHUNK, SGU_CHUNK), 0)
    s_in = lax.broadcasted_iota(jnp.int32, (SGU_CHUNK, SGU_CHUNK), 1)
    causal = s_in <= t_out
    for g in range(SGU_GROUPS):
        wm_ref[g] = jnp.where(causal, ws_ref[g], 0.0).astype(BF16)
    gw = SGU_WIDTH // SGU_GROUPS

    def chunk(ci, carry):
        rows = pl.ds(pl.multiple_of(ci * SGU_CHUNK, SGU_CHUNK), SGU_CHUNK)
        v = v_ref[rows, :].astype(F32)
        mu = jnp.mean(v, axis=-1, keepdims=True)
        vc = v - mu
        var = jnp.mean(vc * vc, axis=-1, keepdims=True)
        vn_ref[...] = (vc * lax.rsqrt(var + NORM_EPS) * lng_ref[...] + lnb_ref[...]).astype(BF16)
        for g in range(SGU_GROUPS):
            cols = slice(g * gw, (g + 1) * gw)
            sv = jnp.dot(wm_ref[g], vn_ref[:, cols], preferred_element_type=F32) + bst_ref[:, g:g + 1]
            z = z_ref[rows, cols].astype(F32)
            y_ref[rows, cols] = (u_ref[rows, cols].astype(F32) * sv * _silu(z)).astype(BF16)
        return carry

    lax.fori_loop(0, tc // SGU_CHUNK, chunk, 0)


def _sgu_core(proj, ln_g, ln_b, ws, bs, *, tc=SGU_TC):
    M = proj.shape[0]
    E = SGU_WIDTH
    return pl.pallas_call(
        functools.partial(_sgu_core_kernel, tc=tc),
        out_shape=jax.ShapeDtypeStruct((M, E), BF16),
        grid=(M // tc,),
        in_specs=[pl.BlockSpec((tc, E), lambda i: (i, 0)),
                  pl.BlockSpec((tc, E), lambda i: (i, 1)),
                  pl.BlockSpec((tc, E), lambda i: (i, 2)),
                  pl.BlockSpec((1, E), lambda i: (0, 0)),
                  pl.BlockSpec((1, E), lambda i: (0, 0)),
                  pl.BlockSpec((SGU_GROUPS, SGU_CHUNK, SGU_CHUNK), lambda i: (0, 0, 0)),
                  pl.BlockSpec((SGU_CHUNK, SGU_GROUPS), lambda i: (0, 0))],
        out_specs=pl.BlockSpec((tc, E), lambda i: (i, 0)),
        scratch_shapes=[pltpu.VMEM((SGU_GROUPS, SGU_CHUNK, SGU_CHUNK), BF16),
                        pltpu.VMEM((SGU_CHUNK, E), BF16)],
        compiler_params=_params(1),
        name="sgu_core",
    )(proj, proj, proj, ln_g.reshape(1, E), ln_b.reshape(1, E), ws, bs.T)


def _conv_mix_kernel(x_ref, sc_ref, sh_ref, g_ref, wa_ref, wb_ref, dw_ref, db_ref, lng_ref, lnb_ref,
                     y_ref, h_ref, acc_a, acc_b, gate_a, gate_b, ext_ref, halo_ref, cv_ref, s1_ref, s2_ref,
                     *, tm, tn, n_glu, tiles_per_seq):
    i, j = pl.program_id(0), pl.program_id(1)
    per_sub = MXU_N // LANES
    per_step = tn // LANES
    accs, gates = (acc_a, acc_b), (gate_a, gate_b)
    first_of_seq = (i % tiles_per_seq) == 0
    lead = HALO - (CONV_K - 1)
    rc = CONV_ROWS

    @pl.when(j == 0)
    def _():
        _norm_prologue(x_ref, sc_ref, sh_ref, g_ref, h_ref, None, d=1, tm=tm)
        s1_ref[...] = jnp.zeros_like(s1_ref)
        s2_ref[...] = jnp.zeros_like(s2_ref)

    def matmul(c, slot, glu):
        cols = slice(c * MXU_N, (c + 1) * MXU_N)
        accs[slot][...] = jnp.dot(h_ref[...], wa_ref[:, cols], preferred_element_type=F32)
        if glu:
            gates[slot][...] = jnp.dot(h_ref[...], wb_ref[:, cols], preferred_element_type=F32)

    def finish_glu(c, slot):
        for s in range(per_sub):
            lanes = slice(s * LANES, (s + 1) * LANES)
            slab = j * per_step + c * per_sub + s
            stage = (c % 2) * per_sub + s
            ext_ref[stage, 0:HALO, :] = jnp.where(first_of_seq, 0.0, halo_ref[slab])
            for r0 in range(0, tm, rc):
                ext_ref[stage, HALO + r0:HALO + r0 + rc, :] = (
                    accs[slot][r0:r0 + rc, lanes] * _sigmoid(gates[slot][r0:r0 + rc, lanes]))
            halo_ref[slab] = ext_ref[stage, tm:tm + HALO, :]
            for r0 in range(0, tm, rc):
                acc = jnp.broadcast_to(db_ref[slab], (rc, LANES))
                for k in range(CONV_K):
                    acc = acc + dw_ref[slab, k:k + 1, :] * ext_ref[stage, r0 + lead + k:r0 + lead + k + rc, :]
                cv_ref[slab, r0:r0 + rc, :] = acc.astype(BF16)
                s1_ref[r0:r0 + rc, :] += acc
                s2_ref[r0:r0 + rc, :] += acc * acc

    def finish_z(c, slot):
        mu, inv = s1_ref[...], s2_ref[...]
        for s in range(per_sub):
            lanes = slice(s * LANES, (s + 1) * LANES)
            slab = (j - n_glu) * per_step + c * per_sub + s
            t = (cv_ref[slab].astype(F32) - mu) * inv * lng_ref[slab] + lnb_ref[slab]
            y_ref[:, c * MXU_N + s * LANES:c * MXU_N + (s + 1) * LANES] = (
                _silu(t) * _silu(accs[slot][:, lanes])).astype(BF16)

    @pl.when(j < n_glu)
    def _():
        _pipelined_subtiles(tn // MXU_N, functools.partial(matmul, glu=True), finish_glu)

    @pl.when(j == n_glu)
    def _():
        inv_e = 1.0 / CONV_WIDTH
        mu = jnp.sum(s1_ref[...], axis=-1, keepdims=True) * inv_e
        var = jnp.maximum(jnp.sum(s2_ref[...], axis=-1, keepdims=True) * inv_e - mu * mu, 0.0)
        s1_ref[...] = jnp.broadcast_to(mu, s1_ref.shape)
        s2_ref[...] = jnp.broadcast_to(lax.rsqrt(var + NORM_EPS), s2_ref.shape)

    @pl.when(j >= n_glu)
    def _():
        _pipelined_subtiles(tn // MXU_N, functools.partial(matmul, glu=False), finish_z)


def _conv_mix(x, scale, shift, norm_g, w_bf16, dw_w, dw_b, ln_g, ln_b, *, tm=CONV_TM, tn=PROJ_TN):
    B, S, D = x.shape
    E = CONV_WIDTH
    nj = E // tn
    n_slabs = E // LANES
    tiles_m = S // tm
    w_slabs = dw_w.reshape(CONV_K, n_slabs, LANES).transpose(1, 0, 2)

    def slabs(v):
        return v.reshape(n_slabs, 1, LANES)

    def whole(shape):
        return pl.BlockSpec(shape, lambda i, j: (0,) * len(shape))

    return pl.pallas_call(
        functools.partial(_conv_mix_kernel, tm=tm, tn=tn, n_glu=nj, tiles_per_seq=tiles_m),
        out_shape=jax.ShapeDtypeStruct((B * S, E), BF16),
        grid=(B * tiles_m, 2 * nj),
        in_specs=[pl.BlockSpec((1, tm, D), lambda i, j: (i // tiles_m, i % tiles_m, 0)),
                  pl.BlockSpec((1, 1, D), lambda i, j: (i // tiles_m, 0, 0)),
                  pl.BlockSpec((1, 1, D), lambda i, j: (i // tiles_m, 0, 0)),
                  pl.BlockSpec((1, D), lambda i, j: (0, 0)),
                  pl.BlockSpec((D, tn), lambda i, j: (0, jnp.where(j < nj, j, nj + j))),
                  pl.BlockSpec((D, tn), lambda i, j: (0, jnp.minimum(nj + j, 2 * nj - 1))),
                  whole((n_slabs, CONV_K, LANES)), whole((n_slabs, 1, LANES)),
                  whole((n_slabs, 1, LANES)), whole((n_slabs, 1, LANES))],
        out_specs=pl.BlockSpec((tm, tn), lambda i, j: (i, jnp.maximum(j - nj, 0))),
        scratch_shapes=[pltpu.VMEM((tm, D), BF16)] + [pltpu.VMEM((tm, MXU_N), F32)] * 4
        + [pltpu.VMEM((2 * MXU_N // LANES, tm + HALO, LANES), F32),
           pltpu.VMEM((n_slabs, HALO, LANES), F32),
           pltpu.VMEM((n_slabs, tm, LANES), BF16),
           pltpu.VMEM((tm, LANES), F32), pltpu.VMEM((tm, LANES), F32)],
        compiler_params=_params(2),
        name="conv_mix",
    )(x, scale, shift, norm_g, w_bf16, w_bf16, w_slabs, slabs(dw_b), slabs(ln_g), slabs(ln_b))


def _rotate_half_matrix():
    half = ROPE_DIM // 2
    k = jnp.arange(HEAD_DIM)[:, None]
    j = jnp.arange(HEAD_DIM)[None, :]
    p = (jnp.where((j < half) & (k == j + half), -1.0, 0.0)
         + jnp.where((j >= half) & (j < ROPE_DIM) & (k == j - half), 1.0, 0.0))
    return jnp.kron(jnp.eye(MXU_N // HEAD_DIM, dtype=F32), p).astype(BF16)


def _rope_tables(positions):
    B, S = positions.shape
    inv_freq = jnp.power(ROPE_THETA, -jnp.arange(0, ROPE_DIM, 2, dtype=F32) / ROPE_DIM)
    ang = positions.astype(F32)[..., None] * inv_freq
    cos, sin = lax.optimization_barrier((jnp.cos(ang), jnp.sin(ang)))
    rest = HEAD_DIM - ROPE_DIM
    cos_t = jnp.concatenate([cos, cos, jnp.ones((B, S, rest), F32)], axis=-1)
    sin_t = jnp.concatenate([sin, sin, jnp.zeros((B, S, rest), F32)], axis=-1)
    return [tuple(t.reshape(B, S // d, d, HEAD_DIM).transpose(0, 2, 1, 3) for t in (cos_t, sin_t))
            for _, d in DIL_GROUPS]


def _rope_gains(q_gain, k_gain):
    half = ROPE_DIM // 2
    gains = jnp.stack([q_gain, k_gain])
    lane = jnp.arange(HEAD_DIM)
    partner = jnp.where(lane < half, jnp.roll(gains, -half, axis=-1), jnp.roll(gains, half, axis=-1))
    return jnp.stack([gains, partner], axis=1)


def kernel(x, c, positions, ada_w, ada_b, norm_g, attn_w_in, attn_q_gain, attn_k_gain, attn_w_out,
           sgu_w_in, sgu_ln_g, sgu_ln_b, sgu_ws, sgu_bs, sgu_w_out,
           conv_w_in, conv_dw_w, conv_dw_b, conv_ln_g, conv_ln_b, conv_w_out):
    B, S, D = x.shape
    mod = _ada_modulation(c, ada_w, ada_b)
    tables = _rope_tables(positions)
    for i in range(DEPTH):
        shift = mod[i, :, None, :D]
        scale = mod[i, :, None, D:2 * D]
        gate = mod[i, :, None, 2 * D:]
        g_row = norm_g[i].reshape(1, D)
        kind, l = i % N_MIXERS, i // N_MIXERS
        if kind == 0:
            gains = _rope_gains(attn_q_gain[l] * (HEAD_DIM ** -0.5), attn_k_gain[l])
            proj = [_attn_proj(x, scale, shift, g_row, attn_w_in, l, gains, tables[g], group=g,
                               with_gate=(g == 0)) for g in range(N_DIL)]
            y = _attention(*proj).reshape(B * S, ATTN_OUT)
            x = _out_proj(y, attn_w_out[l].astype(BF16), x, gate)
        elif kind == 1:
            proj = _sgu_proj(x, scale, shift, g_row, sgu_w_in, l)
            y = _sgu_core(proj, sgu_ln_g[l], sgu_ln_b[l], sgu_ws[l], sgu_bs[l])
            x = _out_proj(y, sgu_w_out[l].astype(BF16), x, gate)
        else:
            y = _conv_mix(x, scale, shift, g_row, conv_w_in[l].astype(BF16), conv_dw_w[l], conv_dw_b[l],
                          conv_ln_g[l], conv_ln_b[l])
            x = _out_proj(y, conv_w_out[l].astype(BF16), x, gate)
    return x
```

```python
import functools

import jax
import jax.numpy as jnp
from jax import lax
from jax.experimental import pallas as pl
from jax.experimental.pallas import tpu as pltpu

D_MODEL = 2048
DEPTH = 4
N_MIXERS = 3
HEAD_DIM = 128
DIL_GROUPS = ((128, 1), (512, 4), (2048, 16))
N_DIL = len(DIL_GROUPS)
HEADS_PER_GROUP = D_MODEL // HEAD_DIM
ATTN_OUT = HEADS_PER_GROUP * HEAD_DIM
ATTN_QKV = N_DIL * ATTN_OUT
ROPE_DIM = HEAD_DIM // 4
ROPE_THETA = 500000.0
BLOCK = 128
SGU_WIDTH = 2 * D_MODEL
SGU_CHUNK = 128
SGU_GROUPS = 16
CONV_WIDTH = 2 * D_MODEL
CONV_K = 31
NORM_EPS = 1e-6

LANES = 128
MXU_N = 256
SUB = 4
HALO = 32
VMEM_LIMIT = 56 * 1024 * 1024
MASKED = -1e30
LOG2_E = 1.4426950408889634

PROJ_TM, PROJ_TN = 1024, 1024
CONV_TM = 512
OUT_TM = 512
SGU_TC = 512
ADA_ROWS, ADA_TN = 16, 1024
NORM_ROWS = 128
ROPE_ROWS = 32
CONV_ROWS = 64
COMBINE_ROWS = 256

F32 = jnp.float32
BF16 = jnp.bfloat16


def _params(n_axes):
    return pltpu.CompilerParams(dimension_semantics=("arbitrary",) * n_axes,
                                vmem_limit_bytes=VMEM_LIMIT)


def _sigmoid(x):
    return 1.0 / (1.0 + jnp.exp(-x))


def _silu(x):
    return x * _sigmoid(x)


def _gelu_tanh(x):
    return x * (0.5 * (1.0 + jnp.tanh(0.7978845608028654 * (x + 0.044715 * (x * x * x)))))


def _pipelined_subtiles(n_sub, matmul, finish):
    for c in range(n_sub + 1):
        if c < n_sub:
            matmul(c, c % 2)
        if c >= 1:
            finish(c - 1, (c - 1) % 2)


def _split_bf16(v):
    hi = v.astype(BF16)
    return hi, (v - hi.astype(F32)).astype(BF16)


def _ada_kernel(c_ref, w_ref, b_ref, o_ref):
    c_hi, c_lo = _split_bf16(_silu(c_ref[...]))
    w_hi, w_lo = _split_bf16(w_ref[0])
    rows = c_hi.shape[0]
    both = jnp.dot(jnp.concatenate([c_hi, c_lo], axis=0), w_hi, preferred_element_type=F32)
    o_ref[0] = (both[:rows] + both[rows:] + jnp.dot(c_hi, w_lo, preferred_element_type=F32)
                + b_ref[0])


def _ada_modulation(c, ada_w, ada_b):
    B, D = c.shape
    rows, tn = ADA_ROWS, ADA_TN
    c8 = jnp.pad(c, ((0, rows - B), (0, 0)))
    out = pl.pallas_call(
        _ada_kernel,
        out_shape=jax.ShapeDtypeStruct((DEPTH, rows, 3 * D), F32),
        grid=(DEPTH, 3 * D // tn),
        in_specs=[pl.BlockSpec((rows, D), lambda l, j: (0, 0)),
                  pl.BlockSpec((1, D, tn), lambda l, j: (l, 0, j)),
                  pl.BlockSpec((1, 1, tn), lambda l, j: (l, 0, j))],
        out_specs=pl.BlockSpec((1, rows, tn), lambda l, j: (l, 0, j)),
        compiler_params=_params(2),
        name="ada_modulation",
    )(c8, ada_w, ada_b.reshape(DEPTH, 1, 3 * D))
    return out[:, :B]


def _norm_prologue(x_ref, sc_ref, sh_ref, g_ref, h_ref, work_ref, *, d, tm):
    rc = NORM_ROWS
    a = g_ref[...] * (1.0 + sc_ref[0])
    sh = sh_ref[0]

    def chunk_rows(i):
        return pl.ds(pl.multiple_of(i * rc, rc), rc)

    def inv_rms(x):
        return lax.rsqrt(jnp.mean(x * x, axis=-1, keepdims=True) + NORM_EPS)

    if d == 1:
        def body(i, carry):
            x = x_ref[0, chunk_rows(i), :]
            h_ref[chunk_rows(i), :] = (x * inv_rms(x) * a + sh).astype(BF16)
            return carry

        lax.fori_loop(0, tm // rc, body, 0)
        return

    INV, NAT, TMP = 0, 1, 2

    def stats(i, carry):
        work_ref[INV, chunk_rows(i), :] = jnp.broadcast_to(inv_rms(x_ref[0, chunk_rows(i), :]), (rc, LANES))
        return carry

    lax.fori_loop(0, tm // rc, stats, 0)
    rows = tm // d
    for c in range(D_MODEL // LANES):
        cols = slice(c * LANES, (c + 1) * LANES)
        work_ref[NAT] = x_ref[0, :, cols] * work_ref[INV] * a[:, cols] + sh[:, cols]
        if d <= SUB:
            for r in range(d):
                h_ref[r * rows:(r + 1) * rows, cols] = work_ref[NAT, pl.ds(r, rows, stride=d), :].astype(BF16)
            continue
        part = tm // SUB
        for r_lo in range(SUB):
            work_ref[TMP, r_lo * part:(r_lo + 1) * part, :] = work_ref[NAT, pl.ds(r_lo, part, stride=SUB), :]
        for r in range(d):
            src = pl.ds((r % SUB) * part + r // SUB, rows, stride=d // SUB)
            h_ref[r * rows:(r + 1) * rows, cols] = work_ref[TMP, src, :].astype(BF16)


def _attn_proj_kernel(x_ref, sc_ref, sh_ref, g_ref, w_ref, perm_ref, gain_ref, cos_ref, sin_ref,
                      o_ref, h_ref, acc_a, acc_b, part_ref, *work, d, tm, tn, n_rope):
    j = pl.program_id(1)
    rows = tm // d
    rc = min(ROPE_ROWS, rows)
    per_res = rows // rc
    bufs = (acc_a, acc_b)

    @pl.when(j == 0)
    def _():
        _norm_prologue(x_ref, sc_ref, sh_ref, g_ref, h_ref, work[0] if work else None, d=d, tm=tm)

    def matmul(c, slot):
        bufs[slot][...] = jnp.dot(h_ref[...], w_ref[:, c * MXU_N:(c + 1) * MXU_N].astype(BF16),
                                  preferred_element_type=F32)

    def finish(c, slot, rope):
        acc_ref, c0 = bufs[slot], c * MXU_N
        if rope:
            part_ref[...] = jnp.dot(acc_ref[...].astype(BF16), perm_ref[...], preferred_element_type=F32)
            gains = gain_ref[j // (n_rope // 2)]
        for ci in range(tm // rc):
            r, m0, p0 = ci // per_res, (ci % per_res) * rc, ci * rc
            if not rope:
                o_ref[0, r, m0:m0 + rc, c0:c0 + MXU_N] = acc_ref[p0:p0 + rc, :].astype(BF16)
                continue
            cos = cos_ref[0, r, m0:m0 + rc, :] * gains[0:1]
            sin = sin_ref[0, r, m0:m0 + rc, :] * gains[1:2]
            for hc in range(MXU_N // LANES):
                cols = slice(hc * LANES, (hc + 1) * LANES)
                a = acc_ref[p0:p0 + rc, cols]
                ms = jnp.sum(a * a, axis=-1, keepdims=True) * (1.0 / HEAD_DIM)
                rot = (a * cos + part_ref[p0:p0 + rc, cols] * sin) * lax.rsqrt(ms + NORM_EPS)
                o_ref[0, r, m0:m0 + rc, c0 + hc * LANES:c0 + (hc + 1) * LANES] = rot.astype(BF16)

    @pl.when(j < n_rope)
    def _():
        _pipelined_subtiles(tn // MXU_N, matmul, functools.partial(finish, rope=True))

    @pl.when(j >= n_rope)
    def _():
        _pipelined_subtiles(tn // MXU_N, matmul, functools.partial(finish, rope=False))


def _attn_proj(x, scale, shift, norm_g, w_in, layer, gains, tables, *, group, with_gate, tm=PROJ_TM, tn=PROJ_TN):
    B, S, D = x.shape
    d = DIL_GROUPS[group][1]
    n_sec = 4 if with_gate else 3
    per_sec = ATTN_OUT // tn
    tiles_m = S // tm

    def w_index(i, j):
        sec = j // per_sec
        base = jnp.where(sec < 3, sec * (ATTN_QKV // tn) + group * per_sec, 3 * (ATTN_QKV // tn))
        return (layer, 0, base + j % per_sec)

    tab_spec = pl.BlockSpec((1, d, tm // d, LANES), lambda i, j: (i // tiles_m, 0, i % tiles_m, 0))
    scratch = [pltpu.VMEM((tm, D), BF16)] + [pltpu.VMEM((tm, MXU_N), F32)] * 3
    if d > 1:
        scratch.append(pltpu.VMEM((3, tm, LANES), F32))
    return pl.pallas_call(
        functools.partial(_attn_proj_kernel, d=d, tm=tm, tn=tn, n_rope=2 * per_sec),
        out_shape=jax.ShapeDtypeStruct((B, d, S // d, n_sec * ATTN_OUT), BF16),
        grid=(B * tiles_m, n_sec * per_sec),
        in_specs=[pl.BlockSpec((1, tm, D), lambda i, j: (i // tiles_m, i % tiles_m, 0)),
                  pl.BlockSpec((1, 1, D), lambda i, j: (i // tiles_m, 0, 0)),
                  pl.BlockSpec((1, 1, D), lambda i, j: (i // tiles_m, 0, 0)),
                  pl.BlockSpec((1, D), lambda i, j: (0, 0)),
                  pl.BlockSpec((None, D, tn), w_index),
                  pl.BlockSpec((MXU_N, MXU_N), lambda i, j: (0, 0)),
                  pl.BlockSpec((2, 2, LANES), lambda i, j: (0, 0, 0)),
                  tab_spec, tab_spec],
        out_specs=pl.BlockSpec((1, d, tm // d, tn), lambda i, j: (i // tiles_m, 0, i % tiles_m, j)),
        scratch_shapes=scratch,
        compiler_params=_params(2),
        name=f"attn_proj_g{group}",
    )(x, scale, shift, norm_g, w_in, _rotate_half_matrix(), gains, *tables)


def _attn_block(q, k, v, bias):
    s = lax.dot_general(q, k, (((1,), (1,)), ((), ())), preferred_element_type=F32) + bias
    m = jnp.max(s, axis=-1, keepdims=True)
    p = jnp.exp2(s - m).astype(BF16)
    v_ext = jnp.concatenate([v, jnp.ones(v.shape, BF16)], axis=1)
    ol = jnp.dot(p, v_ext, preferred_element_type=F32)
    l = ol[:, HEAD_DIM:]
    return ol[:, :HEAD_DIM] * (1.0 / l), m + jnp.log2(l)


def _attn_kernel(q0, k0, v0, z0, q1, k1, v1, q2, k2, v2, y_ref, o_scr, lse_scr, bias_ref, *, seq):
    @pl.when((pl.program_id(0) == 0) & (pl.program_id(1) == 0))
    def _():
        dist = (lax.broadcasted_iota(jnp.int32, (BLOCK, 2 * BLOCK), 0) + BLOCK
                - lax.broadcasted_iota(jnp.int32, (BLOCK, 2 * BLOCK), 1))
        bias_ref[0:BLOCK, :] = jnp.where(dist >= 0, jnp.where(dist <= BLOCK, 0.0, MASKED), MASKED).astype(F32)
        row = lax.broadcasted_iota(jnp.int32, (2 * BLOCK, 2 * BLOCK), 0)
        col = lax.broadcasted_iota(jnp.int32, (2 * BLOCK, 2 * BLOCK), 1)
        same_seq = (row // BLOCK) == (col // BLOCK)
        bias_ref[BLOCK:, :] = jnp.where(same_seq, jnp.where(col <= row, 0.0, MASKED), MASKED).astype(F32)

    bias_both = bias_ref[0:BLOCK, :]
    bias_first = bias_ref[0:BLOCK, BLOCK:]
    bias_pair = bias_ref[BLOCK:, :]

    groups = ((q0, k0, v0), (q1, k1, v1), (q2, k2, v2))
    for g, (q_ref, k_ref, v_ref) in enumerate(groups):
        d = DIL_GROUPS[g][1]
        nb = seq // d // BLOCK

        def store(r, n, o, lse, g=g, d=d):
            if d == 1:
                rows = pl.ds(BLOCK * n, BLOCK)
            elif d <= SUB:
                rows = pl.ds(r + d * BLOCK * n, BLOCK, stride=d)
            else:
                rows = pl.ds((r % SUB) * (seq // SUB) + r // SUB + (d // SUB) * BLOCK * n, BLOCK,
                             stride=d // SUB)
            slot = g if d <= SUB else N_DIL
            o_scr[slot, rows, :] = o
            lse_scr[slot, rows, :] = lse

        def first_block(r, q_ref=q_ref, k_ref=k_ref, v_ref=v_ref, store=store):
            o, lse = _attn_block(q_ref[0, r, 0:BLOCK, :], k_ref[0, r, 0:BLOCK, :],
                                 v_ref[0, r, 0:BLOCK, :], bias_first)
            store(r, 0, o, lse)

        def later_block(r, n, q_ref=q_ref, k_ref=k_ref, v_ref=v_ref, store=store):
            q_rows, kv_rows = pl.ds(n * BLOCK, BLOCK), pl.ds((n - 1) * BLOCK, 2 * BLOCK)
            o, lse = _attn_block(q_ref[0, r, q_rows, :], k_ref[0, r, kv_rows, :], v_ref[0, r, kv_rows, :],
                                 bias_both)
            store(r, n, o, lse)

        def block_pair(i, q_ref=q_ref, k_ref=k_ref, v_ref=v_ref, store=store):
            def both(ref):
                return jnp.concatenate([ref[0, 2 * i], ref[0, 2 * i + 1]], axis=0)

            o, lse = _attn_block(both(q_ref), both(k_ref), both(v_ref), bias_pair)
            store(2 * i, 0, o[:BLOCK], lse[:BLOCK])
            store(2 * i + 1, 0, o[BLOCK:], lse[BLOCK:])

        if nb > 1:
            for r in range(d):
                first_block(r)
                for n in range(1, nb):
                    later_block(r, n)
        else:
            for i in range(d // 2):
                block_pair(i)

        if d > SUB:
            part = seq // SUB
            for r_lo in range(SUB):
                for scr in (o_scr, lse_scr):
                    scr[g, pl.ds(r_lo, part, stride=SUB), :] = scr[N_DIL, r_lo * part:(r_lo + 1) * part, :]

    rc = COMBINE_ROWS

    for r0 in range(0, seq, rc):
        rows = slice(r0, r0 + rc)
        l0, l1, l2 = lse_scr[0, rows, :], lse_scr[1, rows, :], lse_scr[2, rows, :]
        m = jnp.maximum(jnp.maximum(l0, l1), l2)
        e0, e1, e2 = jnp.exp2(l0 - m), jnp.exp2(l1 - m), jnp.exp2(l2 - m)
        o = (e0 * o_scr[0, rows, :] + e1 * o_scr[1, rows, :] + e2 * o_scr[2, rows, :]) / (e0 + e1 + e2)
        z = z0[0, 0, rows, :].astype(F32)
        y_ref[0, rows, :] = (o * _silu(z)).astype(BF16)


def _attention(p0, p1, p2):
    B, _, S, _ = p0.shape
    H = HEADS_PER_GROUP

    def spec(d, sec):
        return pl.BlockSpec((1, d, S // d, LANES), lambda b, h: (b, 0, 0, sec * H + h))

    in_specs = [spec(1, 0), spec(1, 1), spec(1, 2), spec(1, 3)]
    for d in (DIL_GROUPS[1][1], DIL_GROUPS[2][1]):
        in_specs += [spec(d, 0), spec(d, 1), spec(d, 2)]
    return pl.pallas_call(
        functools.partial(_attn_kernel, seq=S),
        out_shape=jax.ShapeDtypeStruct((B, S, ATTN_OUT), BF16),
        grid=(B, H),
        in_specs=in_specs,
        out_specs=pl.BlockSpec((1, S, LANES), lambda b, h: (b, 0, h)),
        scratch_shapes=[pltpu.VMEM((N_DIL + 1, S, LANES), F32)] * 2 + [pltpu.VMEM((3 * BLOCK, 2 * BLOCK), F32)],
        compiler_params=_params(2),
        name="dilated_attention",
    )(p0, p0, p0, p0, p1, p1, p1, p2, p2, p2)


def _out_proj_kernel(y_ref, w_ref, x_ref, gate_ref, o_ref, acc_a, acc_b):
    bufs = (acc_a, acc_b)

    def matmul(c, slot):
        bufs[slot][...] = jnp.dot(y_ref[...], w_ref[:, c * MXU_N:(c + 1) * MXU_N],
                                  preferred_element_type=F32)

    def finish(c, slot):
        cols = slice(c * MXU_N, (c + 1) * MXU_N)
        o_ref[:, cols] = x_ref[:, cols] + gate_ref[0][:, cols] * bufs[slot][...]

    _pipelined_subtiles(o_ref.shape[1] // MXU_N, matmul, finish)


def _out_proj(y, w_bf16, x, gate, *, tm=OUT_TM):
    B, S, D = x.shape
    M, K = y.shape
    tiles_m = S // tm
    out = pl.pallas_call(
        _out_proj_kernel,
        out_shape=jax.ShapeDtypeStruct((M, D), F32),
        grid=(M // tm,),
        in_specs=[pl.BlockSpec((tm, K), lambda i: (i, 0)),
                  pl.BlockSpec((K, D), lambda i: (0, 0), pipeline_mode=pl.Buffered(1)),
                  pl.BlockSpec((tm, D), lambda i: (i, 0)),
                  pl.BlockSpec((1, 1, D), lambda i: (i // tiles_m, 0, 0))],
        out_specs=pl.BlockSpec((tm, D), lambda i: (i, 0)),
        scratch_shapes=[pltpu.VMEM((tm, MXU_N), F32)] * 2,
        compiler_params=_params(1),
        name="out_proj",
    )(y, w_bf16, x.reshape(M, D), gate)
    return out.reshape(B, S, D)


def _sgu_proj_kernel(x_ref, sc_ref, sh_ref, g_ref, w_ref, o_ref, h_ref, acc_a, acc_b, *, tm, tn, n_gelu):
    j = pl.program_id(1)
    bufs = (acc_a, acc_b)

    @pl.when(j == 0)
    def _():
        _norm_prologue(x_ref, sc_ref, sh_ref, g_ref, h_ref, None, d=1, tm=tm)

    def matmul(c, slot):
        bufs[slot][...] = jnp.dot(h_ref[...], w_ref[:, c * MXU_N:(c + 1) * MXU_N].astype(BF16),
                                  preferred_element_type=F32)

    def finish(c, slot, gelu):
        a = bufs[slot][...]
        o_ref[:, c * MXU_N:(c + 1) * MXU_N] = (_gelu_tanh(a) if gelu else a).astype(BF16)

    @pl.when(j < n_gelu)
    def _():
        _pipelined_subtiles(tn // MXU_N, matmul, functools.partial(finish, gelu=True))

    @pl.when(j >= n_gelu)
    def _():
        _pipelined_subtiles(tn // MXU_N, matmul, functools.partial(finish, gelu=False))


def _sgu_proj(x, scale, shift, norm_g, w_in, layer, *, tm=PROJ_TM, tn=PROJ_TN):
    B, S, D = x.shape
    N = w_in.shape[2]
    tiles_m = S // tm
    return pl.pallas_call(
        functools.partial(_sgu_proj_kernel, tm=tm, tn=tn, n_gelu=2 * SGU_WIDTH // tn),
        out_shape=jax.ShapeDtypeStruct((B * S, N), BF16),
        grid=(B * tiles_m, N // tn),
        in_specs=[pl.BlockSpec((1, tm, D), lambda i, j: (i // tiles_m, i % tiles_m, 0)),
                  pl.BlockSpec((1, 1, D), lambda i, j: (i // tiles_m, 0, 0)),
                  pl.BlockSpec((1, 1, D), lambda i, j: (i // tiles_m, 0, 0)),
                  pl.BlockSpec((1, D), lambda i, j: (0, 0)),
                  pl.BlockSpec((None, D, tn), lambda i, j: (layer, 0, j))],
        out_specs=pl.BlockSpec((tm, tn), lambda i, j: (i, j)),
        scratch_shapes=[pltpu.VMEM((tm, D), BF16)] + [pltpu.VMEM((tm, MXU_N), F32)] * 2,
        compiler_params=_params(2),
        name="sgu_proj",
    )(x, scale, shift, norm_g, w_in)


def _sgu_core_kernel(u_ref, v_ref, z_ref, lng_ref, lnb_ref, ws_ref, bst_ref, y_ref, wm_ref, vn_ref, *, tc):
    t_out = lax.broadcasted_iota(jnp.int32, (SGU_CHUNK, SGU_CHUNK), 0)
    s_in = lax.broadcasted_iota(jnp.int32, (SGU_CHUNK, SGU_CHUNK), 1)
    causal = s_in <= t_out
    for g in range(SGU_GROUPS):
        wm_ref[g] = jnp.where(causal, ws_ref[g], 0.0).astype(BF16)
    gw = SGU_WIDTH // SGU_GROUPS

    def chunk(ci, carry):
        rows = pl.ds(pl.multiple_of(ci * SGU_CHUNK, SGU_CHUNK), SGU_CHUNK)
        v = v_ref[rows, :].astype(F32)
        mu = jnp.mean(v, axis=-1, keepdims=True)
        vc = v - mu
        var = jnp.mean(vc * vc, axis=-1, keepdims=True)
        vn_ref[...] = (vc * lax.rsqrt(var + NORM_EPS) * lng_ref[...] + lnb_ref[...]).astype(BF16)
        for g in range(SGU_GROUPS):
            cols = slice(g * gw, (g + 1) * gw)
            sv = jnp.dot(wm_ref[g], vn_ref[:, cols], preferred_element_type=F32) + bst_ref[:, g:g + 1]
            z = z_ref[rows, cols].astype(F32)
            y_ref[rows, cols] = (u_ref[rows, cols].astype(F32) * sv * _silu(z)).astype(BF16)
        return carry

    lax.fori_loop(0, tc // SGU_CHUNK, chunk, 0)


def _sgu_core(proj, ln_g, ln_b, ws, bs, *, tc=SGU_TC):
    M = proj.shape[0]
    E = SGU_WIDTH
    return pl.pallas_call(
        functools.partial(_sgu_core_kernel, tc=tc),
        out_shape=jax.ShapeDtypeStruct((M, E), BF16),
        grid=(M // tc,),
        in_specs=[pl.BlockSpec((tc, E), lambda i: (i, 0)),
                  pl.BlockSpec((tc, E), lambda i: (i, 1)),
                  pl.BlockSpec((tc, E), lambda i: (i, 2)),
                  pl.BlockSpec((1, E), lambda i: (0, 0)),
                  pl.BlockSpec((1, E), lambda i: (0, 0)),
                  pl.BlockSpec((SGU_GROUPS, SGU_CHUNK, SGU_CHUNK), lambda i: (0, 0, 0)),
                  pl.BlockSpec((SGU_CHUNK, SGU_GROUPS), lambda i: (0, 0))],
        out_specs=pl.BlockSpec((tc, E), lambda i: (i, 0)),
        scratch_shapes=[pltpu.VMEM((SGU_GROUPS, SGU_CHUNK, SGU_CHUNK), BF16),
                        pltpu.VMEM((SGU_CHUNK, E), BF16)],
        compiler_params=_params(1),
        name="sgu_core",
    )(proj, proj, proj, ln_g.reshape(1, E), ln_b.reshape(1, E), ws, bs.T)


def _conv_mix_kernel(x_ref, sc_ref, sh_ref, g_ref, wa_ref, wb_ref, dw_ref, db_ref, lng_ref, lnb_ref,
                     y_ref, h_ref, acc_a, acc_b, gate_a, gate_b, ext_ref, halo_ref, cv_ref, s1_ref, s2_ref,
                     *, tm, tn, n_glu, tiles_per_seq):
    i, j = pl.program_id(0), pl.program_id(1)
    per_sub = MXU_N // LANES
    per_step = tn // LANES
    accs, gates = (acc_a, acc_b), (gate_a, gate_b)
    first_of_seq = (i % tiles_per_seq) == 0
    lead = HALO - (CONV_K - 1)
    rc = CONV_ROWS

    @pl.when(j == 0)
    def _():
        _norm_prologue(x_ref, sc_ref, sh_ref, g_ref, h_ref, None, d=1, tm=tm)
        s1_ref[...] = jnp.zeros_like(s1_ref)
        s2_ref[...] = jnp.zeros_like(s2_ref)

    def matmul(c, slot, glu):
        cols = slice(c * MXU_N, (c + 1) * MXU_N)
        accs[slot][...] = jnp.dot(h_ref[...], wa_ref[:, cols], preferred_element_type=F32)
        if glu:
            gates[slot][...] = jnp.dot(h_ref[...], wb_ref[:, cols], preferred_element_type=F32)

    def finish_glu(c, slot):
        for s in range(per_sub):
            lanes = slice(s * LANES, (s + 1) * LANES)
            slab = j * per_step + c * per_sub + s
            stage = (c % 2) * per_sub + s
            ext_ref[stage, 0:HALO, :] = jnp.where(first_of_seq, 0.0, halo_ref[slab])
            for r0 in range(0, tm, rc):
                ext_ref[stage, HALO + r0:HALO + r0 + rc, :] = (
                    accs[slot][r0:r0 + rc, lanes] * _sigmoid(gates[slot][r0:r0 + rc, lanes]))
            halo_ref[slab] = ext_ref[stage, tm:tm + HALO, :]
            for r0 in range(0, tm, rc):
                acc = jnp.broadcast_to(db_ref[slab], (rc, LANES))
                for k in range(CONV_K):
                    acc = acc + dw_ref[slab, k:k + 1, :] * ext_ref[stage, r0 + lead + k:r0 + lead + k + rc, :]
                cv_ref[slab, r0:r0 + rc, :] = acc.astype(BF16)
                s1_ref[r0:r0 + rc, :] += acc
                s2_ref[r0:r0 + rc, :] += acc * acc

    def finish_z(c, slot):
        mu, inv = s1_ref[...], s2_ref[...]
        for s in range(per_sub):
            lanes = slice(s * LANES, (s + 1) * LANES)
            slab = (j - n_glu) * per_step + c * per_sub + s
            t = (cv_ref[slab].astype(F32) - mu) * inv * lng_ref[slab] + lnb_ref[slab]
            y_ref[:, c * MXU_N + s * LANES:c * MXU_N + (s + 1) * LANES] = (
                _silu(t) * _silu(accs[slot][:, lanes])).astype(BF16)

    @pl.when(j < n_glu)
    def _():
        _pipelined_subtiles(tn // MXU_N, functools.partial(matmul, glu=True), finish_glu)

    @pl.when(j == n_glu)
    def _():
        inv_e = 1.0 / CONV_WIDTH
        mu = jnp.sum(s1_ref[...], axis=-1, keepdims=True) * inv_e
        var = jnp.maximum(jnp.sum(s2_ref[...], axis=-1, keepdims=True) * inv_e - mu * mu, 0.0)
        s1_ref[...] = jnp.broadcast_to(mu, s1_ref.shape)
        s2_ref[...] = jnp.broadcast_to(lax.rsqrt(var + NORM_EPS), s2_ref.shape)

    @pl.when(j >= n_glu)
    def _():
        _pipelined_subtiles(tn // MXU_N, functools.partial(matmul, glu=False), finish_z)


def _conv_mix(x, scale, shift, norm_g, w_bf16, dw_w, dw_b, ln_g, ln_b, *, tm=CONV_TM, tn=PROJ_TN):
    B, S, D = x.shape
    E = CONV_WIDTH
    nj = E // tn
    n_slabs = E // LANES
    tiles_m = S // tm
    w_slabs = dw_w.reshape(CONV_K, n_slabs, LANES).transpose(1, 0, 2)

    def slabs(v):
        return v.reshape(n_slabs, 1, LANES)

    def whole(shape):
        return pl.BlockSpec(shape, lambda i, j: (0,) * len(shape))

    return pl.pallas_call(
        functools.partial(_conv_mix_kernel, tm=tm, tn=tn, n_glu=nj, tiles_per_seq=tiles_m),
        out_shape=jax.ShapeDtypeStruct((B * S, E), BF16),
        grid=(B * tiles_m, 2 * nj),
        in_specs=[pl.BlockSpec((1, tm, D), lambda i, j: (i // tiles_m, i % tiles_m, 0)),
                  pl.BlockSpec((1, 1, D), lambda i, j: (i // tiles_m, 0, 0)),
                  pl.BlockSpec((1, 1, D), lambda i, j: (i // tiles_m, 0, 0)),
                  pl.BlockSpec((1, D), lambda i, j: (0, 0)),
                  pl.BlockSpec((D, tn), lambda i, j: (0, jnp.where(j < nj, j, nj + j))),
                  pl.BlockSpec((D, tn), lambda i, j: (0, jnp.minimum(nj + j, 2 * nj - 1))),
                  whole((n_slabs, CONV_K, LANES)), whole((n_slabs, 1, LANES)),
                  whole((n_slabs, 1, LANES)), whole((n_slabs, 1, LANES))],
        out_specs=pl.BlockSpec((tm, tn), lambda i, j: (i, jnp.maximum(j - nj, 0))),
        scratch_shapes=[pltpu.VMEM((tm, D), BF16)] + [pltpu.VMEM((tm, MXU_N), F32)] * 4
        + [pltpu.VMEM((2 * MXU_N // LANES, tm + HALO, LANES), F32),
           pltpu.VMEM((n_slabs, HALO, LANES), F32),
           pltpu.VMEM((n_slabs, tm, LANES), BF16),
           pltpu.VMEM((tm, LANES), F32), pltpu.VMEM((tm, LANES), F32)],
        compiler_params=_params(2),
        name="conv_mix",
    )(x, scale, shift, norm_g, w_bf16, w_bf16, w_slabs, slabs(dw_b), slabs(ln_g), slabs(ln_b))


def _rotate_half_matrix():
    half = ROPE_DIM // 2
    k = jnp.arange(HEAD_DIM)[:, None]
    j = jnp.arange(HEAD_DIM)[None, :]
    p = (jnp.where((j < half) & (k == j + half), -1.0, 0.0)
         + jnp.where((j >= half) & (j < ROPE_DIM) & (k == j - half), 1.0, 0.0))
    return jnp.kron(jnp.eye(MXU_N // HEAD_DIM, dtype=F32), p).astype(BF16)


def _rope_tables(positions):
    B, S = positions.shape
    inv_freq = jnp.power(ROPE_THETA, -jnp.arange(0, ROPE_DIM, 2, dtype=F32) / ROPE_DIM)
    ang = positions.astype(F32)[..., None] * inv_freq
    cos, sin = lax.optimization_barrier((jnp.cos(ang), jnp.sin(ang)))
    rest = HEAD_DIM - ROPE_DIM
    cos_t = jnp.concatenate([cos, cos, jnp.ones((B, S, rest), F32)], axis=-1)
    sin_t = jnp.concatenate([sin, sin, jnp.zeros((B, S, rest), F32)], axis=-1)
    return [tuple(t.reshape(B, S // d, d, HEAD_DIM).transpose(0, 2, 1, 3) for t in (cos_t, sin_t))
            for _, d in DIL_GROUPS]


def _rope_gains(q_gain, k_gain):
    half = ROPE_DIM // 2
    gains = jnp.stack([q_gain, k_gain])
    lane = jnp.arange(HEAD_DIM)
    partner = jnp.where(lane < half, jnp.roll(gains, -half, axis=-1), jnp.roll(gains, half, axis=-1))
    return jnp.stack([gains, partner], axis=1)


def kernel(x, c, positions, ada_w, ada_b, norm_g, attn_w_in, attn_q_gain, attn_k_gain, attn_w_out,
           sgu_w_in, sgu_ln_g, sgu_ln_b, sgu_ws, sgu_bs, sgu_w_out,
           conv_w_in, conv_dw_w, conv_dw_b, conv_ln_g, conv_ln_b, conv_w_out):
    B, S, D = x.shape
    mod = _ada_modulation(c, ada_w, ada_b)
    tables = _rope_tables(positions)
    for i in range(DEPTH):
        shift = mod[i, :, None, :D]
        scale = mod[i, :, None, D:2 * D]
        gate = mod[i, :, None, 2 * D:]
        g_row = norm_g[i].reshape(1, D)
        kind, l = i % N_MIXERS, i // N_MIXERS
        if kind == 0:
            gains = _rope_gains(attn_q_gain[l] * (LOG2_E * HEAD_DIM ** -0.5), attn_k_gain[l])
            proj = [_attn_proj(x, scale, shift, g_row, attn_w_in, l, gains, tables[g], group=g,
                               with_gate=(g == 0)) for g in range(N_DIL)]
            y = _attention(*proj).reshape(B * S, ATTN_OUT)
            x = _out_proj(y, attn_w_out[l].astype(BF16), x, gate)
        elif kind == 1:
            proj = _sgu_proj(x, scale, shift, g_row, sgu_w_in, l)
            y = _sgu_core(proj, sgu_ln_g[l], sgu_ln_b[l], sgu_ws[l], sgu_bs[l])
            x = _out_proj(y, sgu_w_out[l].astype(BF16), x, gate)
        else:
            y = _conv_mix(x, scale, shift, g_row, conv_w_in[l].astype(BF16), conv_dw_w[l], conv_dw_b[l],
                          conv_ln_g[l], conv_ln_b[l])
            x = _out_proj(y, conv_w_out[l].astype(BF16), x, gate)
    return x
```

```python
import functools

import jax
import jax.numpy as jnp
from jax import lax
from jax.experimental import pallas as pl
from jax.experimental.pallas import tpu as pltpu

D_MODEL = 2048
DEPTH = 4
N_MIXERS = 3
HEAD_DIM = 128
DIL_GROUPS = ((128, 1), (512, 4), (2048, 16))
N_DIL = len(DIL_GROUPS)
HEADS_PER_GROUP = D_MODEL // HEAD_DIM
ATTN_OUT = HEADS_PER_GROUP * HEAD_DIM
ATTN_QKV = N_DIL * ATTN_OUT
ROPE_DIM = HEAD_DIM // 4
ROPE_THETA = 500000.0
BLOCK = 128
SGU_WIDTH = 2 * D_MODEL
SGU_CHUNK = 128
SGU_GROUPS = 16
CONV_WIDTH = 2 * D_MODEL
CONV_K = 31
NORM_EPS = 1e-6

LANES = 128
MXU_N = 256
SUB = 4
HALO = 32
VMEM_LIMIT = 56 * 1024 * 1024
MASKED = -1e30
LOG2_E = 1.4426950408889634

PROJ_TM, PROJ_TN = 1024, 1024
CONV_TM = 512
OUT_TM = 512
SGU_TC = 512
ADA_ROWS, ADA_TN = 16, 1024
NORM_ROWS = 128
ROPE_ROWS = 32
CONV_ROWS = 64
COMBINE_ROWS = 256

F32 = jnp.float32
BF16 = jnp.bfloat16


def _params(n_axes):
    return pltpu.CompilerParams(dimension_semantics=("arbitrary",) * n_axes,
                                vmem_limit_bytes=VMEM_LIMIT)


def _sigmoid(x):
    return 1.0 / (1.0 + jnp.exp(-x))


def _silu(x):
    return x * _sigmoid(x)


def _gelu_tanh(x):
    return x * (0.5 * (1.0 + jnp.tanh(0.7978845608028654 * (x + 0.044715 * (x * x * x)))))


def _pipelined_subtiles(n_sub, matmul, finish):
    for c in range(n_sub + 1):
        if c < n_sub:
            matmul(c, c % 2)
        if c >= 1:
            finish(c - 1, (c - 1) % 2)


def _split_bf16(v):
    hi = v.astype(BF16)
    return hi, (v - hi.astype(F32)).astype(BF16)


def _ada_kernel(c_ref, w_ref, b_ref, o_ref):
    c_hi, c_lo = _split_bf16(_silu(c_ref[...]))
    w_hi, w_lo = _split_bf16(w_ref[0])
    rows = c_hi.shape[0]
    both = jnp.dot(jnp.concatenate([c_hi, c_lo], axis=0), w_hi, preferred_element_type=F32)
    o_ref[0] = (both[:rows] + both[rows:] + jnp.dot(c_hi, w_lo, preferred_element_type=F32)
                + b_ref[0])


def _ada_modulation(c, ada_w, ada_b):
    B, D = c.shape
    rows, tn = ADA_ROWS, ADA_TN
    c8 = jnp.pad(c, ((0, rows - B), (0, 0)))
    out = pl.pallas_call(
        _ada_kernel,
        out_shape=jax.ShapeDtypeStruct((DEPTH, rows, 3 * D), F32),
        grid=(DEPTH, 3 * D // tn),
        in_specs=[pl.BlockSpec((rows, D), lambda l, j: (0, 0)),
                  pl.BlockSpec((1, D, tn), lambda l, j: (l, 0, j)),
                  pl.BlockSpec((1, 1, tn), lambda l, j: (l, 0, j))],
        out_specs=pl.BlockSpec((1, rows, tn), lambda l, j: (l, 0, j)),
        compiler_params=_params(2),
        name="ada_modulation",
    )(c8, ada_w, ada_b.reshape(DEPTH, 1, 3 * D))
    return out[:, :B]


def _norm_prologue(x_ref, sc_ref, sh_ref, g_ref, h_ref, work_ref, *, d, tm):
    rc = NORM_ROWS
    a = g_ref[...] * (1.0 + sc_ref[0])
    sh = sh_ref[0]

    def chunk_rows(i):
        return pl.ds(pl.multiple_of(i * rc, rc), rc)

    def inv_rms(x):
        return lax.rsqrt(jnp.mean(x * x, axis=-1, keepdims=True) + NORM_EPS)

    if d == 1:
        def body(i, carry):
            x = x_ref[0, chunk_rows(i), :]
            h_ref[chunk_rows(i), :] = (x * inv_rms(x) * a + sh).astype(BF16)
            return carry

        lax.fori_loop(0, tm // rc, body, 0)
        return

    INV, NAT, TMP = 0, 1, 2

    def stats(i, carry):
        work_ref[INV, chunk_rows(i), :] = jnp.broadcast_to(inv_rms(x_ref[0, chunk_rows(i), :]), (rc, LANES))
        return carry

    lax.fori_loop(0, tm // rc, stats, 0)
    rows = tm // d
    for c in range(D_MODEL // LANES):
        cols = slice(c * LANES, (c + 1) * LANES)
        work_ref[NAT] = x_ref[0, :, cols] * work_ref[INV] * a[:, cols] + sh[:, cols]
        if d <= SUB:
            for r in range(d):
                h_ref[r * rows:(r + 1) * rows, cols] = work_ref[NAT, pl.ds(r, rows, stride=d), :].astype(BF16)
            continue
        part = tm // SUB
        for r_lo in range(SUB):
            work_ref[TMP, r_lo * part:(r_lo + 1) * part, :] = work_ref[NAT, pl.ds(r_lo, part, stride=SUB), :]
        for r in range(d):
            src = pl.ds((r % SUB) * part + r // SUB, rows, stride=d // SUB)
            h_ref[r * rows:(r + 1) * rows, cols] = work_ref[TMP, src, :].astype(BF16)


def _attn_proj_kernel(x_ref, sc_ref, sh_ref, g_ref, w_ref, perm_ref, gain_ref, cos_ref, sin_ref,
                      o_ref, h_ref, acc_a, acc_b, part_ref, *work, d, tm, tn, n_rope):
    j = pl.program_id(1)
    rows = tm // d
    rc = min(ROPE_ROWS, rows)
    per_res = rows // rc
    bufs = (acc_a, acc_b)

    @pl.when(j == 0)
    def _():
        _norm_prologue(x_ref, sc_ref, sh_ref, g_ref, h_ref, work[0] if work else None, d=d, tm=tm)

    def matmul(c, slot):
        bufs[slot][...] = jnp.dot(h_ref[...], w_ref[:, c * MXU_N:(c + 1) * MXU_N].astype(BF16),
                                  preferred_element_type=F32)

    def finish(c, slot, rope):
        acc_ref, c0 = bufs[slot], c * MXU_N
        if rope:
            part_ref[...] = jnp.dot(acc_ref[...].astype(BF16), perm_ref[...], preferred_element_type=F32)
            gains = gain_ref[j // (n_rope // 2)]
        for ci in range(tm // rc):
            r, m0, p0 = ci // per_res, (ci % per_res) * rc, ci * rc
            if not rope:
                o_ref[0, r, m0:m0 + rc, c0:c0 + MXU_N] = acc_ref[p0:p0 + rc, :].astype(BF16)
                continue
            cos = cos_ref[0, r, m0:m0 + rc, :] * gains[0:1]
            sin = sin_ref[0, r, m0:m0 + rc, :] * gains[1:2]
            for hc in range(MXU_N // LANES):
                cols = slice(hc * LANES, (hc + 1) * LANES)
                a = acc_ref[p0:p0 + rc, cols]
                ms = jnp.sum(a * a, axis=-1, keepdims=True) * (1.0 / HEAD_DIM)
                rot = (a * cos + part_ref[p0:p0 + rc, cols] * sin) * lax.rsqrt(ms + NORM_EPS)
                o_ref[0, r, m0:m0 + rc, c0 + hc * LANES:c0 + (hc + 1) * LANES] = rot.astype(BF16)

    @pl.when(j < n_rope)
    def _():
        _pipelined_subtiles(tn // MXU_N, matmul, functools.partial(finish, rope=True))

    @pl.when(j >= n_rope)
    def _():
        _pipelined_subtiles(tn // MXU_N, matmul, functools.partial(finish, rope=False))


def _attn_proj(x, scale, shift, norm_g, w_in, layer, gains, tables, *, group, with_gate, tm=PROJ_TM, tn=PROJ_TN):
    B, S, D = x.shape
    d = DIL_GROUPS[group][1]
    n_sec = 4 if with_gate else 3
    per_sec = ATTN_OUT // tn
    tiles_m = S // tm

    def w_index(i, j):
        sec = j // per_sec
        base = jnp.where(sec < 3, sec * (ATTN_QKV // tn) + group * per_sec, 3 * (ATTN_QKV // tn))
        return (layer, 0, base + j % per_sec)

    tab_spec = pl.BlockSpec((1, d, tm // d, LANES), lambda i, j: (i // tiles_m, 0, i % tiles_m, 0))
    scratch = [pltpu.VMEM((tm, D), BF16)] + [pltpu.VMEM((tm, MXU_N), F32)] * 3
    if d > 1:
        scratch.append(pltpu.VMEM((3, tm, LANES), F32))
    return pl.pallas_call(
        functools.partial(_attn_proj_kernel, d=d, tm=tm, tn=tn, n_rope=2 * per_sec),
        out_shape=jax.ShapeDtypeStruct((B, d, S // d, n_sec * ATTN_OUT), BF16),
        grid=(B * tiles_m, n_sec * per_sec),
        in_specs=[pl.BlockSpec((1, tm, D), lambda i, j: (i // tiles_m, i % tiles_m, 0)),
                  pl.BlockSpec((1, 1, D), lambda i, j: (i // tiles_m, 0, 0)),
                  pl.BlockSpec((1, 1, D), lambda i, j: (i // tiles_m, 0, 0)),
                  pl.BlockSpec((1, D), lambda i, j: (0, 0)),
                  pl.BlockSpec((None, D, tn), w_index),
                  pl.BlockSpec((MXU_N, MXU_N), lambda i, j: (0, 0)),
                  pl.BlockSpec((2, 2, LANES), lambda i, j: (0, 0, 0)),
                  tab_spec, tab_spec],
        out_specs=pl.BlockSpec((1, d, tm // d, tn), lambda i, j: (i // tiles_m, 0, i % tiles_m, j)),
        scratch_shapes=scratch,
        compiler_params=_params(2),
        name=f"attn_proj_g{group}",
    )(x, scale, shift, norm_g, w_in, _rotate_half_matrix(), gains, *tables)


def _attn_block(q, k, v, bias):
    s = lax.dot_general(q, k, (((1,), (1,)), ((), ())), preferred_element_type=F32) + bias
    m = jnp.max(s, axis=-1, keepdims=True)
    p = jnp.exp2(s - m).astype(BF16)
    v_ext = jnp.concatenate([v, jnp.ones(v.shape, BF16)], axis=1)
    ol = jnp.dot(p, v_ext, preferred_element_type=F32)
    l = ol[:, HEAD_DIM:]
    return ol[:, :HEAD_DIM] * (1.0 / l), m + jnp.log2(l)


def _attn_kernel(q0, k0, v0, z0, q1, k1, v1, q2, k2, v2, y_ref, o_scr, lse_scr, bias_ref, *, seq):
    @pl.when((pl.program_id(0) == 0) & (pl.program_id(1) == 0))
    def _():
        dist = (lax.broadcasted_iota(jnp.int32, (BLOCK, 2 * BLOCK), 0) + BLOCK
                - lax.broadcasted_iota(jnp.int32, (BLOCK, 2 * BLOCK), 1))
        bias_ref[0:BLOCK, :] = jnp.where(dist >= 0, jnp.where(dist <= BLOCK, 0.0, MASKED), MASKED).astype(F32)
        row = lax.broadcasted_iota(jnp.int32, (2 * BLOCK, 2 * BLOCK), 0)
        col = lax.broadcasted_iota(jnp.int32, (2 * BLOCK, 2 * BLOCK), 1)
        same_seq = (row // BLOCK) == (col // BLOCK)
        bias_ref[BLOCK:, :] = jnp.where(same_seq, jnp.where(col <= row, 0.0, MASKED), MASKED).astype(F32)

    bias_both = bias_ref[0:BLOCK, :]
    bias_first = bias_ref[0:BLOCK, BLOCK:]
    bias_pair = bias_ref[BLOCK:, :]

    groups = ((q0, k0, v0), (q1, k1, v1), (q2, k2, v2))
    for g, (q_ref, k_ref, v_ref) in enumerate(groups):
        d = DIL_GROUPS[g][1]
        nb = seq // d // BLOCK

        def store(r, n, o, lse, g=g, d=d):
            if d == 1:
                rows = pl.ds(BLOCK * n, BLOCK)
            elif d <= SUB:
                rows = pl.ds(r + d * BLOCK * n, BLOCK, stride=d)
            else:
                rows = pl.ds((r % SUB) * (seq // SUB) + r // SUB + (d // SUB) * BLOCK * n, BLOCK,
                             stride=d // SUB)
            slot = g if d <= SUB else N_DIL
            o_scr[slot, rows, :] = o
            lse_scr[slot, rows, :] = lse

        def first_block(r, q_ref=q_ref, k_ref=k_ref, v_ref=v_ref, store=store):
            o, lse = _attn_block(q_ref[0, r, 0:BLOCK, :], k_ref[0, r, 0:BLOCK, :],
                                 v_ref[0, r, 0:BLOCK, :], bias_first)
            store(r, 0, o, lse)

        def later_block(r, n, q_ref=q_ref, k_ref=k_ref, v_ref=v_ref, store=store):
            q_rows, kv_rows = pl.ds(n * BLOCK, BLOCK), pl.ds((n - 1) * BLOCK, 2 * BLOCK)
            o, lse = _attn_block(q_ref[0, r, q_rows, :], k_ref[0, r, kv_rows, :], v_ref[0, r, kv_rows, :],
                                 bias_both)
            store(r, n, o, lse)

        def block_pair(i, q_ref=q_ref, k_ref=k_ref, v_ref=v_ref, store=store):
            def both(ref):
                return jnp.concatenate([ref[0, 2 * i], ref[0, 2 * i + 1]], axis=0)

            o, lse = _attn_block(both(q_ref), both(k_ref), both(v_ref), bias_pair)
            store(2 * i, 0, o[:BLOCK], lse[:BLOCK])
            store(2 * i + 1, 0, o[BLOCK:], lse[BLOCK:])

        if nb > 1:
            for r in range(d):
                first_block(r)
                for n in range(1, nb):
                    later_block(r, n)
        else:
            for i in range(d // 2):
                block_pair(i)

        if d > SUB:
            part = seq // SUB
            for r_lo in range(SUB):
                for scr in (o_scr, lse_scr):
                    scr[g, pl.ds(r_lo, part, stride=SUB), :] = scr[N_DIL, r_lo * part:(r_lo + 1) * part, :]

    rc = COMBINE_ROWS

    for r0 in range(0, seq, rc):
        rows = slice(r0, r0 + rc)
        l0, l1, l2 = lse_scr[0, rows, :], lse_scr[1, rows, :], lse_scr[2, rows, :]
        m = jnp.maximum(jnp.maximum(l0, l1), l2)
        e0, e1, e2 = jnp.exp2(l0 - m), jnp.exp2(l1 - m), jnp.exp2(l2 - m)
        o = (e0 * o_scr[0, rows, :] + e1 * o_scr[1, rows, :] + e2 * o_scr[2, rows, :]) / (e0 + e1 + e2)
        z = z0[0, 0, rows, :].astype(F32)
        y_ref[0, rows, :] = (o * _silu(z)).astype(BF16)


def _attention(p0, p1, p2):
    B, _, S, _ = p0.shape
    H = HEADS_PER_GROUP

    def spec(d, sec):
        return pl.BlockSpec((1, d, S // d, LANES), lambda b, h: (b, 0, 0, sec * H + h))

    in_specs = [spec(1, 0), spec(1, 1), spec(1, 2), spec(1, 3)]
    for d in (DIL_GROUPS[1][1], DIL_GROUPS[2][1]):
        in_specs += [spec(d, 0), spec(d, 1), spec(d, 2)]
    return pl.pallas_call(
        functools.partial(_attn_kernel, seq=S),
        out_shape=jax.ShapeDtypeStruct((B, S, ATTN_OUT), BF16),
        grid=(B, H),
        in_specs=in_specs,
        out_specs=pl.BlockSpec((1, S, LANES), lambda b, h: (b, 0, h)),
        scratch_shapes=[pltpu.VMEM((N_DIL + 1, S, LANES), F32)] * 2 + [pltpu.VMEM((3 * BLOCK, 2 * BLOCK), F32)],
        compiler_params=_params(2),
        name="dilated_attention",
    )(p0, p0, p0, p0, p1, p1, p1, p2, p2, p2)


def _out_proj_kernel(y_ref, w_ref, x_ref, gate_ref, o_ref, acc_a, acc_b):
    bufs = (acc_a, acc_b)

    def matmul(c, slot):
        bufs[slot][...] = jnp.dot(y_ref[...], w_ref[:, c * MXU_N:(c + 1) * MXU_N],
                                  preferred_element_type=F32)

    def finish(c, slot):
        cols = slice(c * MXU_N, (c + 1) * MXU_N)
        o_ref[:, cols] = x_ref[:, cols] + gate_ref[0][:, cols] * bufs[slot][...]

    _pipelined_subtiles(o_ref.shape[1] // MXU_N, matmul, finish)


def _out_proj(y, w_bf16, x, gate, *, tm=OUT_TM):
    B, S, D = x.shape
    M, K = y.shape
    tiles_m = S // tm
    out = pl.pallas_call(
        _out_proj_kernel,
        out_shape=jax.ShapeDtypeStruct((M, D), F32),
        grid=(M // tm,),
        in_specs=[pl.BlockSpec((tm, K), lambda i: (i, 0)),
                  pl.BlockSpec((K, D), lambda i: (0, 0), pipeline_mode=pl.Buffered(1)),
                  pl.BlockSpec((tm, D), lambda i: (i, 0)),
                  pl.BlockSpec((1, 1, D), lambda i: (i // tiles_m, 0, 0))],
        out_specs=pl.BlockSpec((tm, D), lambda i: (i, 0)),
        scratch_shapes=[pltpu.VMEM((tm, MXU_N), F32)] * 2,
        compiler_params=_params(1),
        name="out_proj",
    )(y, w_bf16, x.reshape(M, D), gate)
    return out.reshape(B, S, D)


def _sgu_proj_kernel(x_ref, sc_ref, sh_ref, g_ref, w_ref, o_ref, h_ref, acc_a, acc_b, *, tm, tn, n_gelu):
    j = pl.program_id(1)
    bufs = (acc_a, acc_b)

    @pl.when(j == 0)
    def _():
        _norm_prologue(x_ref, sc_ref, sh_ref, g_ref, h_ref, None, d=1, tm=tm)

    def matmul(c, slot):
        bufs[slot][...] = jnp.dot(h_ref[...], w_ref[:, c * MXU_N:(c + 1) * MXU_N].astype(BF16),
                                  preferred_element_type=F32)

    def finish(c, slot, gelu):
        a = bufs[slot][...]
        o_ref[:, c * MXU_N:(c + 1) * MXU_N] = (_gelu_tanh(a) if gelu else a).astype(BF16)

    @pl.when(j < n_gelu)
    def _():
        _pipelined_subtiles(tn // MXU_N, matmul, functools.partial(finish, gelu=True))

    @pl.when(j >= n_gelu)
    def _():
        _pipelined_subtiles(tn // MXU_N, matmul, functools.partial(finish, gelu=False))


def _sgu_proj(x, scale, shift, norm_g, w_in, layer, *, tm=PROJ_TM, tn=PROJ_TN):
    B, S, D = x.shape
    N = w_in.shape[2]
    tiles_m = S // tm
    return pl.pallas_call(
        functools.partial(_sgu_proj_kernel, tm=tm, tn=tn, n_gelu=2 * SGU_WIDTH // tn),
        out_shape=jax.ShapeDtypeStruct((B * S, N), BF16),
        grid=(B * tiles_m, N // tn),
        in_specs=[pl.BlockSpec((1, tm, D), lambda i, j: (i // tiles_m, i % tiles_m, 0)),
                  pl.BlockSpec((1, 1, D), lambda i, j: (i // tiles_m, 0, 0)),
                  pl.BlockSpec((1, 1, D), lambda i, j: (i // tiles_m, 0, 0)),
                  pl.BlockSpec((1, D), lambda i, j: (0, 0)),
                  pl.BlockSpec((None, D, tn), lambda i, j: (layer, 0, j))],
        out_specs=pl.BlockSpec((tm, tn), lambda i, j: (i, j)),
        scratch_shapes=[pltpu.VMEM((tm, D), BF16)] + [pltpu.VMEM((tm, MXU_N), F32)] * 2,
        compiler_params=_params(2),
        name="sgu_proj",
    )(x, scale, shift, norm_g, w_in)


def _sgu_core_kernel(u_ref, v_ref, z_ref, lng_ref, lnb_ref, ws_ref, bst_ref, y_ref, wm_ref, vn_ref, *, tc):
    t_out = lax.broadcasted_iota(jnp.int32, (SGU_CHUNK, SGU_CHUNK), 0)
    s_in = lax.broadcasted_iota(jnp.int32, (SGU_CHUNK, SGU_CHUNK), 1)
    causal = s_in <= t_out
    for g in range(SGU_GROUPS):
        wm_ref[g] = jnp.where(causal, ws_ref[g], 0.0).astype(BF16)
    gw = SGU_WIDTH // SGU_GROUPS

    def chunk(ci, carry):
        rows = pl.ds(pl.multiple_of(ci * SGU_CHUNK, SGU_CHUNK), SGU_CHUNK)
        v = v_ref[rows, :].astype(F32)
        mu = jnp.mean(v, axis=-1, keepdims=True)
        vc = v - mu
        var = jnp.mean(vc * vc, axis=-1, keepdims=True)
        vn_ref[...] = (vc * lax.rsqrt(var + NORM_EPS) * lng_ref[...] + lnb_ref[...]).astype(BF16)
        for g in range(SGU_GROUPS):
            cols = slice(g * gw, (g + 1) * gw)
            sv = jnp.dot(wm_ref[g], vn_ref[:, cols], preferred_element_type=F32) + bst_ref[:, g:g + 1]
            z = z_ref[rows, cols].astype(F32)
            y_ref[rows, cols] = (u_ref[rows, cols].astype(F32) * sv * _silu(z)).astype(BF16)
        return carry

    lax.fori_loop(0, tc // SGU_CHUNK, chunk, 0)


def _sgu_core(proj, ln_g, ln_b, ws, bs, *, tc=SGU_TC):
    M = proj.shape[0]
    E = SGU_WIDTH
    return pl.pallas_call(
        functools.partial(_sgu_core_kernel, tc=tc),
        out_shape=jax.ShapeDtypeStruct((M, E), BF16),
        grid=(M // tc,),
        in_specs=[pl.BlockSpec((tc, E), lambda i: (i, 0)),
                  pl.BlockSpec((tc, E), lambda i: (i, 1)),
                  pl.BlockSpec((tc, E), lambda i: (i, 2)),
                  pl.BlockSpec((1, E), lambda i: (0, 0)),
                  pl.BlockSpec((1, E), lambda i: (0, 0)),
                  pl.BlockSpec((SGU_GROUPS, SGU_CHUNK, SGU_CHUNK), lambda i: (0, 0, 0)),
                  pl.BlockSpec((SGU_CHUNK, SGU_GROUPS), lambda i: (0, 0))],
        out_specs=pl.BlockSpec((tc, E), lambda i: (i, 0)),
        scratch_shapes=[pltpu.VMEM((SGU_GROUPS, SGU_CHUNK, SGU_CHUNK), BF16),
                        pltpu.VMEM((SGU_CHUNK, E), BF16)],
        compiler_params=_params(1),
        name="sgu_core",
    )(proj, proj, proj, ln_g.reshape(1, E), ln_b.reshape(1, E), ws, bs.T)


def _conv_mix_kernel(x_ref, sc_ref, sh_ref, g_ref, wa_ref, wb_ref, dw_ref, db_ref, off_ref, lng_ref, lnb_ref,
                     y_ref, h_ref, acc_a, acc_b, gate_a, gate_b, ext_ref, halo_ref, cv_ref, s1_ref, s2_ref,
                     *, tm, tn, n_glu, tiles_per_seq):
    i, j = pl.program_id(0), pl.program_id(1)
    per_sub = MXU_N // LANES
    per_step = tn // LANES
    accs, gates = (acc_a, acc_b), (gate_a, gate_b)
    first_of_seq = (i % tiles_per_seq) == 0
    lead = HALO - (CONV_K - 1)
    rc = CONV_ROWS

    @pl.when(j == 0)
    def _():
        _norm_prologue(x_ref, sc_ref, sh_ref, g_ref, h_ref, None, d=1, tm=tm)
        s1_ref[...] = jnp.zeros_like(s1_ref)
        s2_ref[...] = jnp.zeros_like(s2_ref)

    def matmul(c, slot, glu):
        cols = slice(c * MXU_N, (c + 1) * MXU_N)
        accs[slot][...] = jnp.dot(h_ref[...], wa_ref[:, cols], preferred_element_type=F32)
        if glu:
            gates[slot][...] = jnp.dot(h_ref[...], wb_ref[:, cols], preferred_element_type=F32)

    def finish_glu(c, slot):
        for s in range(per_sub):
            lanes = slice(s * LANES, (s + 1) * LANES)
            slab = j * per_step + c * per_sub + s
            stage = (c % 2) * per_sub + s
            ext_ref[stage, 0:HALO, :] = jnp.where(first_of_seq, 0.0, halo_ref[slab])
            for r0 in range(0, tm, rc):
                ext_ref[stage, HALO + r0:HALO + r0 + rc, :] = (
                    accs[slot][r0:r0 + rc, lanes] * _sigmoid(gates[slot][r0:r0 + rc, lanes]))
            halo_ref[slab] = ext_ref[stage, tm:tm + HALO, :]
            for r0 in range(0, tm, rc):
                acc = jnp.broadcast_to(db_ref[slab], (rc, LANES))
                for k in range(CONV_K):
                    acc = acc + dw_ref[slab, k:k + 1, :] * ext_ref[stage, r0 + lead + k:r0 + lead + k + rc, :]
                cv_ref[slab, r0:r0 + rc, :] = acc.astype(BF16)
                dev = acc - off_ref[...]
                s1_ref[r0:r0 + rc, :] += dev
                s2_ref[r0:r0 + rc, :] += dev * dev

    def finish_z(c, slot):
        mu, inv = s1_ref[...], s2_ref[...]
        for s in range(per_sub):
            lanes = slice(s * LANES, (s + 1) * LANES)
            slab = (j - n_glu) * per_step + c * per_sub + s
            t = (cv_ref[slab].astype(F32) - mu) * inv * lng_ref[slab] + lnb_ref[slab]
            y_ref[:, c * MXU_N + s * LANES:c * MXU_N + (s + 1) * LANES] = (
                _silu(t) * _silu(accs[slot][:, lanes])).astype(BF16)

    @pl.when(j < n_glu)
    def _():
        _pipelined_subtiles(tn // MXU_N, functools.partial(matmul, glu=True), finish_glu)

    @pl.when(j == n_glu)
    def _():
        inv_e = 1.0 / CONV_WIDTH
        mu = jnp.sum(s1_ref[...], axis=-1, keepdims=True) * inv_e
        var = jnp.maximum(jnp.sum(s2_ref[...], axis=-1, keepdims=True) * inv_e - mu * mu, 0.0)
        s1_ref[...] = jnp.broadcast_to(mu, s1_ref.shape) + off_ref[...]
        s2_ref[...] = jnp.broadcast_to(lax.rsqrt(var + NORM_EPS), s2_ref.shape)

    @pl.when(j >= n_glu)
    def _():
        _pipelined_subtiles(tn // MXU_N, functools.partial(matmul, glu=False), finish_z)


def _conv_mix(x, scale, shift, norm_g, w_bf16, dw_w, dw_b, ln_g, ln_b, *, tm=CONV_TM, tn=PROJ_TN):
    B, S, D = x.shape
    E = CONV_WIDTH
    nj = E // tn
    n_slabs = E // LANES
    tiles_m = S // tm
    w_slabs = dw_w.reshape(CONV_K, n_slabs, LANES).transpose(1, 0, 2)

    def slabs(v):
        return v.reshape(n_slabs, 1, LANES)

    def whole(shape):
        return pl.BlockSpec(shape, lambda i, j: (0,) * len(shape))

    return pl.pallas_call(
        functools.partial(_conv_mix_kernel, tm=tm, tn=tn, n_glu=nj, tiles_per_seq=tiles_m),
        out_shape=jax.ShapeDtypeStruct((B * S, E), BF16),
        grid=(B * tiles_m, 2 * nj),
        in_specs=[pl.BlockSpec((1, tm, D), lambda i, j: (i // tiles_m, i % tiles_m, 0)),
                  pl.BlockSpec((1, 1, D), lambda i, j: (i // tiles_m, 0, 0)),
                  pl.BlockSpec((1, 1, D), lambda i, j: (i // tiles_m, 0, 0)),
                  pl.BlockSpec((1, D), lambda i, j: (0, 0)),
                  pl.BlockSpec((D, tn), lambda i, j: (0, jnp.where(j < nj, j, nj + j))),
                  pl.BlockSpec((D, tn), lambda i, j: (0, jnp.minimum(nj + j, 2 * nj - 1))),
                  whole((n_slabs, CONV_K, LANES)), whole((n_slabs, 1, LANES)), whole((1, LANES)),
                  whole((n_slabs, 1, LANES)), whole((n_slabs, 1, LANES))],
        out_specs=pl.BlockSpec((tm, tn), lambda i, j: (i, jnp.maximum(j - nj, 0))),
        scratch_shapes=[pltpu.VMEM((tm, D), BF16)] + [pltpu.VMEM((tm, MXU_N), F32)] * 4
        + [pltpu.VMEM((2 * MXU_N // LANES, tm + HALO, LANES), F32),
           pltpu.VMEM((n_slabs, HALO, LANES), F32),
           pltpu.VMEM((n_slabs, tm, LANES), BF16),
           pltpu.VMEM((tm, LANES), F32), pltpu.VMEM((tm, LANES), F32)],
        compiler_params=_params(2),
        name="conv_mix",
    )(x, scale, shift, norm_g, w_bf16, w_bf16, w_slabs, slabs(dw_b),
      jnp.full((1, LANES), jnp.mean(dw_b), F32), slabs(ln_g), slabs(ln_b))


def _rotate_half_matrix():
    half = ROPE_DIM // 2
    k = jnp.arange(HEAD_DIM)[:, None]
    j = jnp.arange(HEAD_DIM)[None, :]
    p = (jnp.where((j < half) & (k == j + half), -1.0, 0.0)
         + jnp.where((j >= half) & (j < ROPE_DIM) & (k == j - half), 1.0, 0.0))
    return jnp.kron(jnp.eye(MXU_N // HEAD_DIM, dtype=F32), p).astype(BF16)


def _rope_tables(positions):
    B, S = positions.shape
    inv_freq = jnp.power(ROPE_THETA, -jnp.arange(0, ROPE_DIM, 2, dtype=F32) / ROPE_DIM)
    ang = positions.astype(F32)[..., None] * inv_freq
    cos, sin = lax.optimization_barrier((jnp.cos(ang), jnp.sin(ang)))
    rest = HEAD_DIM - ROPE_DIM
    cos_t = jnp.concatenate([cos, cos, jnp.ones((B, S, rest), F32)], axis=-1)
    sin_t = jnp.concatenate([sin, sin, jnp.zeros((B, S, rest), F32)], axis=-1)
    return [tuple(t.reshape(B, S // d, d, HEAD_DIM).transpose(0, 2, 1, 3) for t in (cos_t, sin_t))
            for _, d in DIL_GROUPS]


def _rope_gains(q_gain, k_gain):
    half = ROPE_DIM // 2
    gains = jnp.stack([q_gain, k_gain])
    lane = jnp.arange(HEAD_DIM)
    partner = jnp.where(lane < half, jnp.roll(gains, -half, axis=-1), jnp.roll(gains, half, axis=-1))
    return jnp.stack([gains, partner], axis=1)


def kernel(x, c, positions, ada_w, ada_b, norm_g, attn_w_in, attn_q_gain, attn_k_gain, attn_w_out,
           sgu_w_in, sgu_ln_g, sgu_ln_b, sgu_ws, sgu_bs, sgu_w_out,
           conv_w_in, conv_dw_w, conv_dw_b, conv_ln_g, conv_ln_b, conv_w_out):
    B, S, D = x.shape
    mod = _ada_modulation(c, ada_w, ada_b)
    tables = _rope_tables(positions)
    for i in range(DEPTH):
        shift = mod[i, :, None, :D]
        scale = mod[i, :, None, D:2 * D]
        gate = mod[i, :, None, 2 * D:]
        g_row = norm_g[i].reshape(1, D)
        kind, l = i % N_MIXERS, i // N_MIXERS
        if kind == 0:
            gains = _rope_gains(attn_q_gain[l] * (LOG2_E * HEAD_DIM ** -0.5), attn_k_gain[l])
            proj = [_attn_proj(x, scale, shift, g_row, attn_w_in, l, gains, tables[g], group=g,
                               with_gate=(g == 0)) for g in range(N_DIL)]
            y = _attention(*proj).reshape(B * S, ATTN_OUT)
            x = _out_proj(y, attn_w_out[l].astype(BF16), x, gate)
        elif kind == 1:
            proj = _sgu_proj(x, scale, shift, g_row, sgu_w_in, l)
            y = _sgu_core(proj, sgu_ln_g[l], sgu_ln_b[l], sgu_ws[l], sgu_bs[l])
            x = _out_proj(y, sgu_w_out[l].astype(BF16), x, gate)
        else:
            y = _conv_mix(x, scale, shift, g_row, conv_w_in[l].astype(BF16), conv_dw_w[l], conv_dw_b[l],
                          conv_ln_g[l], conv_ln_b[l])
            x = _out_proj(y, conv_w_out[l].astype(BF16), x, gate)
    return x
```

```python
import functools

import jax
import jax.numpy as jnp
from jax import lax
from jax.experimental import pallas as pl
from jax.experimental.pallas import tpu as pltpu

D_MODEL = 2048
DEPTH = 4
N_MIXERS = 3
HEAD_DIM = 128
DIL_GROUPS = ((128, 1), (512, 4), (2048, 16))
N_DIL = len(DIL_GROUPS)
HEADS_PER_GROUP = D_MODEL // HEAD_DIM
ATTN_OUT = HEADS_PER_GROUP * HEAD_DIM
ATTN_QKV = N_DIL * ATTN_OUT
ROPE_DIM = HEAD_DIM // 4
ROPE_THETA = 500000.0
BLOCK = 128
SGU_WIDTH = 2 * D_MODEL
SGU_CHUNK = 128
SGU_GROUPS = 16
CONV_WIDTH = 2 * D_MODEL
CONV_K = 31
NORM_EPS = 1e-6

LANES = 128
MXU_N = 256
SUB = 4
HALO = 32
VMEM_LIMIT = 56 * 1024 * 1024
MASKED = -1e30
LOG2_E = 1.4426950408889634

PROJ_TM, PROJ_TN = 1024, 1024
CONV_TM = 512
OUT_TM = 512
SGU_TC = 512
ADA_ROWS, ADA_TN = 16, 1024
NORM_ROWS = 128
ROPE_ROWS = 32
CONV_ROWS = 64
COMBINE_ROWS = 256
X_CHUNKS = 4

F32 = jnp.float32
BF16 = jnp.bfloat16


def _params(n_axes):
    return pltpu.CompilerParams(dimension_semantics=("arbitrary",) * n_axes,
                                vmem_limit_bytes=VMEM_LIMIT)


def _sigmoid(x):
    return 1.0 / (1.0 + jnp.exp(-x))


def _silu(x):
    return x * _sigmoid(x)


def _gelu_tanh(x):
    return x * (0.5 * (1.0 + jnp.tanh(0.7978845608028654 * (x + 0.044715 * (x * x * x)))))


def _x_specs(tiles_m, n_row_tiles, tm):
    def spec(c):
        def index(i, j):
            t = jnp.minimum(i + jnp.where(j > c, 1, 0), n_row_tiles - 1)
            return (t // tiles_m, t % tiles_m, c)
        return pl.BlockSpec((1, tm, D_MODEL // X_CHUNKS), index)

    return [spec(c) for c in range(X_CHUNKS)]


def _pipelined_subtiles(n_sub, matmul, finish):
    for c in range(n_sub + 1):
        if c < n_sub:
            matmul(c, c % 2)
        if c >= 1:
            finish(c - 1, (c - 1) % 2)


def _split_bf16(v):
    hi = v.astype(BF16)
    return hi, (v - hi.astype(F32)).astype(BF16)


def _ada_kernel(c_ref, w_ref, b_ref, o_ref):
    c_hi, c_lo = _split_bf16(_silu(c_ref[...]))
    w_hi, w_lo = _split_bf16(w_ref[0])
    rows = c_hi.shape[0]
    both = jnp.dot(jnp.concatenate([c_hi, c_lo], axis=0), w_hi, preferred_element_type=F32)
    o_ref[0] = (both[:rows] + both[rows:] + jnp.dot(c_hi, w_lo, preferred_element_type=F32)
                + b_ref[0])


def _ada_modulation(c, ada_w, ada_b):
    B, D = c.shape
    rows, tn = ADA_ROWS, ADA_TN
    c8 = jnp.pad(c, ((0, rows - B), (0, 0)))
    out = pl.pallas_call(
        _ada_kernel,
        out_shape=jax.ShapeDtypeStruct((DEPTH, rows, 3 * D), F32),
        grid=(DEPTH, 3 * D // tn),
        in_specs=[pl.BlockSpec((rows, D), lambda l, j: (0, 0)),
                  pl.BlockSpec((1, D, tn), lambda l, j: (l, 0, j)),
                  pl.BlockSpec((1, 1, tn), lambda l, j: (l, 0, j))],
        out_specs=pl.BlockSpec((1, rows, tn), lambda l, j: (l, 0, j)),
        compiler_params=_params(2),
        name="ada_modulation",
    )(c8, ada_w, ada_b.reshape(DEPTH, 1, 3 * D))
    return out[:, :B]


def _norm_prologue(x_refs, sc_ref, sh_ref, g_ref, h_ref, work_ref, *, d, tm):
    rc = NORM_ROWS
    a = g_ref[...] * (1.0 + sc_ref[0])
    sh = sh_ref[0]

    def chunk_rows(i):
        return pl.ds(pl.multiple_of(i * rc, rc), rc)

    def x_rows(rows):
        return jnp.concatenate([x_ref[0, rows, :] for x_ref in x_refs], axis=1)

    def x_slab(c):
        per_chunk = D_MODEL // X_CHUNKS // LANES
        lane0 = (c % per_chunk) * LANES
        return x_refs[c // per_chunk][0, :, lane0:lane0 + LANES]

    def inv_rms(x):
        return lax.rsqrt(jnp.mean(x * x, axis=-1, keepdims=True) + NORM_EPS)

    if d == 1:
        def body(i, carry):
            x = x_rows(chunk_rows(i))
            h_ref[chunk_rows(i), :] = (x * inv_rms(x) * a + sh).astype(BF16)
            return carry

        lax.fori_loop(0, tm // rc, body, 0)
        return

    INV, NAT, TMP = 0, 1, 2

    def stats(i, carry):
        work_ref[INV, chunk_rows(i), :] = jnp.broadcast_to(inv_rms(x_rows(chunk_rows(i))), (rc, LANES))
        return carry

    lax.fori_loop(0, tm // rc, stats, 0)
    rows = tm // d
    for c in range(D_MODEL // LANES):
        cols = slice(c * LANES, (c + 1) * LANES)
        work_ref[NAT] = x_slab(c) * work_ref[INV] * a[:, cols] + sh[:, cols]
        if d <= SUB:
            for r in range(d):
                h_ref[r * rows:(r + 1) * rows, cols] = work_ref[NAT, pl.ds(r, rows, stride=d), :].astype(BF16)
            continue
        part = tm // SUB
        for r_lo in range(SUB):
            work_ref[TMP, r_lo * part:(r_lo + 1) * part, :] = work_ref[NAT, pl.ds(r_lo, part, stride=SUB), :]
        for r in range(d):
            src = pl.ds((r % SUB) * part + r // SUB, rows, stride=d // SUB)
            h_ref[r * rows:(r + 1) * rows, cols] = work_ref[TMP, src, :].astype(BF16)


def _attn_proj_kernel(*refs, d, tm, tn, n_rope):
    x_refs, refs = refs[:X_CHUNKS], refs[X_CHUNKS:]
    (sc_ref, sh_ref, g_ref, w_ref, perm_ref, gain_ref, cos_ref, sin_ref,
     o_ref, h_ref, acc_a, acc_b, part_ref, *work) = refs
    j = pl.program_id(1)
    rows = tm // d
    rc = min(ROPE_ROWS, rows)
    per_res = rows // rc
    bufs = (acc_a, acc_b)

    @pl.when(j == 0)
    def _():
        _norm_prologue(x_refs, sc_ref, sh_ref, g_ref, h_ref, work[0] if work else None, d=d, tm=tm)

    def matmul(c, slot):
        bufs[slot][...] = jnp.dot(h_ref[...], w_ref[:, c * MXU_N:(c + 1) * MXU_N].astype(BF16),
                                  preferred_element_type=F32)

    def finish(c, slot, rope):
        acc_ref, c0 = bufs[slot], c * MXU_N
        if rope:
            part_ref[...] = jnp.dot(acc_ref[...].astype(BF16), perm_ref[...], preferred_element_type=F32)
            gains = gain_ref[j // (n_rope // 2)]
        for ci in range(tm // rc):
            r, m0, p0 = ci // per_res, (ci % per_res) * rc, ci * rc
            if not rope:
                o_ref[0, r, m0:m0 + rc, c0:c0 + MXU_N] = acc_ref[p0:p0 + rc, :].astype(BF16)
                continue
            cos = cos_ref[0, r, m0:m0 + rc, :] * gains[0:1]
            sin = sin_ref[0, r, m0:m0 + rc, :] * gains[1:2]
            for hc in range(MXU_N // LANES):
                cols = slice(hc * LANES, (hc + 1) * LANES)
                a = acc_ref[p0:p0 + rc, cols]
                ms = jnp.sum(a * a, axis=-1, keepdims=True) * (1.0 / HEAD_DIM)
                rot = (a * cos + part_ref[p0:p0 + rc, cols] * sin) * lax.rsqrt(ms + NORM_EPS)
                o_ref[0, r, m0:m0 + rc, c0 + hc * LANES:c0 + (hc + 1) * LANES] = rot.astype(BF16)

    @pl.when(j < n_rope)
    def _():
        _pipelined_subtiles(tn // MXU_N, matmul, functools.partial(finish, rope=True))

    @pl.when(j >= n_rope)
    def _():
        _pipelined_subtiles(tn // MXU_N, matmul, functools.partial(finish, rope=False))


def _attn_proj(x, scale, shift, norm_g, w_in, layer, gains, tables, *, group, with_gate, tm=PROJ_TM, tn=PROJ_TN):
    B, S, D = x.shape
    d = DIL_GROUPS[group][1]
    n_sec = 4 if with_gate else 3
    per_sec = ATTN_OUT // tn
    tiles_m = S // tm

    def w_index(i, j):
        sec = j // per_sec
        base = jnp.where(sec < 3, sec * (ATTN_QKV // tn) + group * per_sec, 3 * (ATTN_QKV // tn))
        return (layer, 0, base + j % per_sec)

    tab_spec = pl.BlockSpec((1, d, tm // d, LANES), lambda i, j: (i // tiles_m, 0, i % tiles_m, 0))
    scratch = [pltpu.VMEM((tm, D), BF16)] + [pltpu.VMEM((tm, MXU_N), F32)] * 3
    if d > 1:
        scratch.append(pltpu.VMEM((3, tm, LANES), F32))
    return pl.pallas_call(
        functools.partial(_attn_proj_kernel, d=d, tm=tm, tn=tn, n_rope=2 * per_sec),
        out_shape=jax.ShapeDtypeStruct((B, d, S // d, n_sec * ATTN_OUT), BF16),
        grid=(B * tiles_m, n_sec * per_sec),
        in_specs=_x_specs(tiles_m, B * tiles_m, tm) + [
                  pl.BlockSpec((1, 1, D), lambda i, j: (i // tiles_m, 0, 0)),
                  pl.BlockSpec((1, 1, D), lambda i, j: (i // tiles_m, 0, 0)),
                  pl.BlockSpec((1, D), lambda i, j: (0, 0)),
                  pl.BlockSpec((None, D, tn), w_index),
                  pl.BlockSpec((MXU_N, MXU_N), lambda i, j: (0, 0)),
                  pl.BlockSpec((2, 2, LANES), lambda i, j: (0, 0, 0)),
                  tab_spec, tab_spec],
        out_specs=pl.BlockSpec((1, d, tm // d, tn), lambda i, j: (i // tiles_m, 0, i % tiles_m, j)),
        scratch_shapes=scratch,
        compiler_params=_params(2),
        name=f"attn_proj_g{group}",
    )(*[x] * X_CHUNKS, scale, shift, norm_g, w_in, _rotate_half_matrix(), gains, *tables)


def _attn_block(q, k, v, bias):
    s = lax.dot_general(q, k, (((1,), (1,)), ((), ())), preferred_element_type=F32) + bias
    m = jnp.max(s, axis=-1, keepdims=True)
    p = jnp.exp2(s - m).astype(BF16)
    v_ext = jnp.concatenate([v, jnp.ones(v.shape, BF16)], axis=1)
    ol = jnp.dot(p, v_ext, preferred_element_type=F32)
    l = ol[:, HEAD_DIM:]
    return ol[:, :HEAD_DIM] * (1.0 / l), m + jnp.log2(l)


def _attn_kernel(q0, k0, v0, z0, q1, k1, v1, q2, k2, v2, y_ref, o_scr, lse_scr, bias_ref, *, seq):
    @pl.when((pl.program_id(0) == 0) & (pl.program_id(1) == 0))
    def _():
        dist = (lax.broadcasted_iota(jnp.int32, (BLOCK, 2 * BLOCK), 0) + BLOCK
                - lax.broadcasted_iota(jnp.int32, (BLOCK, 2 * BLOCK), 1))
        bias_ref[0:BLOCK, :] = jnp.where(dist >= 0, jnp.where(dist <= BLOCK, 0.0, MASKED), MASKED).astype(F32)
        row = lax.broadcasted_iota(jnp.int32, (2 * BLOCK, 2 * BLOCK), 0)
        col = lax.broadcasted_iota(jnp.int32, (2 * BLOCK, 2 * BLOCK), 1)
        same_seq = (row // BLOCK) == (col // BLOCK)
        bias_ref[BLOCK:, :] = jnp.where(same_seq, jnp.where(col <= row, 0.0, MASKED), MASKED).astype(F32)

    bias_both = bias_ref[0:BLOCK, :]
    bias_first = bias_ref[0:BLOCK, BLOCK:]
    bias_pair = bias_ref[BLOCK:, :]

    groups = ((q0, k0, v0), (q1, k1, v1), (q2, k2, v2))
    for g, (q_ref, k_ref, v_ref) in enumerate(groups):
        d = DIL_GROUPS[g][1]
        nb = seq // d // BLOCK

        def store(r, n, o, lse, g=g, d=d):
            if d == 1:
                rows = pl.ds(BLOCK * n, BLOCK)
            elif d <= SUB:
                rows = pl.ds(r + d * BLOCK * n, BLOCK, stride=d)
            else:
                rows = pl.ds((r % SUB) * (seq // SUB) + r // SUB + (d // SUB) * BLOCK * n, BLOCK,
                             stride=d // SUB)
            slot = g if d <= SUB else N_DIL
            o_scr[slot, rows, :] = o
            lse_scr[slot, rows, :] = lse

        def first_block(r, q_ref=q_ref, k_ref=k_ref, v_ref=v_ref, store=store):
            o, lse = _attn_block(q_ref[0, r, 0:BLOCK, :], k_ref[0, r, 0:BLOCK, :],
                                 v_ref[0, r, 0:BLOCK, :], bias_first)
            store(r, 0, o, lse)

        def later_block(r, n, q_ref=q_ref, k_ref=k_ref, v_ref=v_ref, store=store):
            q_rows, kv_rows = pl.ds(n * BLOCK, BLOCK), pl.ds((n - 1) * BLOCK, 2 * BLOCK)
            o, lse = _attn_block(q_ref[0, r, q_rows, :], k_ref[0, r, kv_rows, :], v_ref[0, r, kv_rows, :],
                                 bias_both)
            store(r, n, o, lse)

        def block_pair(i, q_ref=q_ref, k_ref=k_ref, v_ref=v_ref, store=store):
            def both(ref):
                return jnp.concatenate([ref[0, 2 * i], ref[0, 2 * i + 1]], axis=0)

            o, lse = _attn_block(both(q_ref), both(k_ref), both(v_ref), bias_pair)
            store(2 * i, 0, o[:BLOCK], lse[:BLOCK])
            store(2 * i + 1, 0, o[BLOCK:], lse[BLOCK:])

        if nb > 1:
            for r in range(d):
                first_block(r)
                for n in range(1, nb):
                    later_block(r, n)
        else:
            for i in range(d // 2):
                block_pair(i)

        if d > SUB:
            part = seq // SUB
            for r_lo in range(SUB):
                for scr in (o_scr, lse_scr):
                    scr[g, pl.ds(r_lo, part, stride=SUB), :] = scr[N_DIL, r_lo * part:(r_lo + 1) * part, :]

    rc = COMBINE_ROWS

    for r0 in range(0, seq, rc):
        rows = slice(r0, r0 + rc)
        l0, l1, l2 = lse_scr[0, rows, :], lse_scr[1, rows, :], lse_scr[2, rows, :]
        m = jnp.maximum(jnp.maximum(l0, l1), l2)
        e0, e1, e2 = jnp.exp2(l0 - m), jnp.exp2(l1 - m), jnp.exp2(l2 - m)
        o = (e0 * o_scr[0, rows, :] + e1 * o_scr[1, rows, :] + e2 * o_scr[2, rows, :]) / (e0 + e1 + e2)
        z = z0[0, 0, rows, :].astype(F32)
        y_ref[0, rows, :] = (o * _silu(z)).astype(BF16)


def _attention(p0, p1, p2):
    B, _, S, _ = p0.shape
    H = HEADS_PER_GROUP

    def spec(d, sec):
        return pl.BlockSpec((1, d, S // d, LANES), lambda b, h: (b, 0, 0, sec * H + h))

    in_specs = [spec(1, 0), spec(1, 1), spec(1, 2), spec(1, 3)]
    for d in (DIL_GROUPS[1][1], DIL_GROUPS[2][1]):
        in_specs += [spec(d, 0), spec(d, 1), spec(d, 2)]
    return pl.pallas_call(
        functools.partial(_attn_kernel, seq=S),
        out_shape=jax.ShapeDtypeStruct((B, S, ATTN_OUT), BF16),
        grid=(B, H),
        in_specs=in_specs,
        out_specs=pl.BlockSpec((1, S, LANES), lambda b, h: (b, 0, h)),
        scratch_shapes=[pltpu.VMEM((N_DIL + 1, S, LANES), F32)] * 2 + [pltpu.VMEM((3 * BLOCK, 2 * BLOCK), F32)],
        compiler_params=_params(2),
        name="dilated_attention",
    )(p0, p0, p0, p0, p1, p1, p1, p2, p2, p2)


def _out_proj_kernel(y_ref, w_ref, x_ref, gate_ref, o_ref, acc_a, acc_b):
    bufs = (acc_a, acc_b)

    def matmul(c, slot):
        bufs[slot][...] = jnp.dot(y_ref[...], w_ref[:, c * MXU_N:(c + 1) * MXU_N],
                                  preferred_element_type=F32)

    def finish(c, slot):
        cols = slice(c * MXU_N, (c + 1) * MXU_N)
        o_ref[:, cols] = x_ref[:, cols] + gate_ref[0][:, cols] * bufs[slot][...]

    _pipelined_subtiles(o_ref.shape[1] // MXU_N, matmul, finish)


def _out_proj(y, w_bf16, x, gate, *, tm=OUT_TM):
    B, S, D = x.shape
    M, K = y.shape
    tiles_m = S // tm
    out = pl.pallas_call(
        _out_proj_kernel,
        out_shape=jax.ShapeDtypeStruct((M, D), F32),
        grid=(M // tm,),
        in_specs=[pl.BlockSpec((tm, K), lambda i: (i, 0)),
                  pl.BlockSpec((K, D), lambda i: (0, 0), pipeline_mode=pl.Buffered(1)),
                  pl.BlockSpec((tm, D), lambda i: (i, 0)),
                  pl.BlockSpec((1, 1, D), lambda i: (i // tiles_m, 0, 0))],
        out_specs=pl.BlockSpec((tm, D), lambda i: (i, 0)),
        scratch_shapes=[pltpu.VMEM((tm, MXU_N), F32)] * 2,
        compiler_params=_params(1),
        name="out_proj",
    )(y, w_bf16, x.reshape(M, D), gate)
    return out.reshape(B, S, D)


def _sgu_proj_kernel(*refs, tm, tn, n_gelu):
    x_refs, (sc_ref, sh_ref, g_ref, w_ref, o_ref, h_ref, acc_a, acc_b) = refs[:X_CHUNKS], refs[X_CHUNKS:]
    j = pl.program_id(1)
    bufs = (acc_a, acc_b)

    @pl.when(j == 0)
    def _():
        _norm_prologue(x_refs, sc_ref, sh_ref, g_ref, h_ref, None, d=1, tm=tm)

    def matmul(c, slot):
        bufs[slot][...] = jnp.dot(h_ref[...], w_ref[:, c * MXU_N:(c + 1) * MXU_N].astype(BF16),
                                  preferred_element_type=F32)

    def finish(c, slot, gelu):
        a = bufs[slot][...]
        o_ref[:, c * MXU_N:(c + 1) * MXU_N] = (_gelu_tanh(a) if gelu else a).astype(BF16)

    @pl.when(j < n_gelu)
    def _():
        _pipelined_subtiles(tn // MXU_N, matmul, functools.partial(finish, gelu=True))

    @pl.when(j >= n_gelu)
    def _():
        _pipelined_subtiles(tn // MXU_N, matmul, functools.partial(finish, gelu=False))


def _sgu_proj(x, scale, shift, norm_g, w_in, layer, *, tm=PROJ_TM, tn=PROJ_TN):
    B, S, D = x.shape
    N = w_in.shape[2]
    tiles_m = S // tm
    return pl.pallas_call(
        functools.partial(_sgu_proj_kernel, tm=tm, tn=tn, n_gelu=2 * SGU_WIDTH // tn),
        out_shape=jax.ShapeDtypeStruct((B * S, N), BF16),
        grid=(B * tiles_m, N // tn),
        in_specs=_x_specs(tiles_m, B * tiles_m, tm) + [
                  pl.BlockSpec((1, 1, D), lambda i, j: (i // tiles_m, 0, 0)),
                  pl.BlockSpec((1, 1, D), lambda i, j: (i // tiles_m, 0, 0)),
                  pl.BlockSpec((1, D), lambda i, j: (0, 0)),
                  pl.BlockSpec((None, D, tn), lambda i, j: (layer, 0, j))],
        out_specs=pl.BlockSpec((tm, tn), lambda i, j: (i, j)),
        scratch_shapes=[pltpu.VMEM((tm, D), BF16)] + [pltpu.VMEM((tm, MXU_N), F32)] * 2,
        compiler_params=_params(2),
        name="sgu_proj",
    )(*[x] * X_CHUNKS, scale, shift, norm_g, w_in)


def _sgu_core_kernel(u_ref, v_ref, z_ref, lng_ref, lnb_ref, ws_ref, bst_ref, y_ref, wm_ref, vn_ref, *, tc):
    t_out = lax.broadcasted_iota(jnp.int32, (SGU_CHUNK, SGU_CHUNK), 0)
    s_in = lax.broadcasted_iota(jnp.int32, (SGU_CHUNK, SGU_CHUNK), 1)
    causal = s_in <= t_out
    for g in range(SGU_GROUPS):
        wm_ref[g] = jnp.where(causal, ws_ref[g], 0.0).astype(BF16)
    gw = SGU_WIDTH // SGU_GROUPS

    def chunk(ci, carry):
        rows = pl.ds(pl.multiple_of(ci * SGU_CHUNK, SGU_CHUNK), SGU_CHUNK)
        v = v_ref[rows, :].astype(F32)
        mu = jnp.mean(v, axis=-1, keepdims=True)
        vc = v - mu
        var = jnp.mean(vc * vc, axis=-1, keepdims=True)
        vn_ref[...] = (vc * lax.rsqrt(var + NORM_EPS) * lng_ref[...] + lnb_ref[...]).astype(BF16)
        for g in range(SGU_GROUPS):
            cols = slice(g * gw, (g + 1) * gw)
            sv = jnp.dot(wm_ref[g], vn_ref[:, cols], preferred_element_type=F32) + bst_ref[:, g:g + 1]
            z = z_ref[rows, cols].astype(F32)
            y_ref[rows, cols] = (u_ref[rows, cols].astype(F32) * sv * _silu(z)).astype(BF16)
        return carry

    lax.fori_loop(0, tc // SGU_CHUNK, chunk, 0)


def _sgu_core(proj, ln_g, ln_b, ws, bs, *, tc=SGU_TC):
    M = proj.shape[0]
    E = SGU_WIDTH
    return pl.pallas_call(
        functools.partial(_sgu_core_kernel, tc=tc),
        out_shape=jax.ShapeDtypeStruct((M, E), BF16),
        grid=(M // tc,),
        in_specs=[pl.BlockSpec((tc, E), lambda i: (i, 0)),
                  pl.BlockSpec((tc, E), lambda i: (i, 1)),
                  pl.BlockSpec((tc, E), lambda i: (i, 2)),
                  pl.BlockSpec((1, E), lambda i: (0, 0)),
                  pl.BlockSpec((1, E), lambda i: (0, 0)),
                  pl.BlockSpec((SGU_GROUPS, SGU_CHUNK, SGU_CHUNK), lambda i: (0, 0, 0)),
                  pl.BlockSpec((SGU_CHUNK, SGU_GROUPS), lambda i: (0, 0))],
        out_specs=pl.BlockSpec((tc, E), lambda i: (i, 0)),
        scratch_shapes=[pltpu.VMEM((SGU_GROUPS, SGU_CHUNK, SGU_CHUNK), BF16),
                        pltpu.VMEM((SGU_CHUNK, E), BF16)],
        compiler_params=_params(1),
        name="sgu_core",
    )(proj, proj, proj, ln_g.reshape(1, E), ln_b.reshape(1, E), ws, bs.T)


def _conv_mix_kernel(*refs, tm, tn, n_glu, tiles_per_seq):
    x_refs, refs = refs[:X_CHUNKS], refs[X_CHUNKS:]
    (sc_ref, sh_ref, g_ref, wa_ref, wb_ref, dw_ref, db_ref, off_ref, lng_ref, lnb_ref,
     y_ref, h_ref, acc_a, acc_b, gate_a, gate_b, ext_ref, halo_ref, cv_ref, s1_ref, s2_ref) = refs
    i, j = pl.program_id(0), pl.program_id(1)
    per_sub = MXU_N // LANES
    per_step = tn // LANES
    accs, gates = (acc_a, acc_b), (gate_a, gate_b)
    first_of_seq = (i % tiles_per_seq) == 0
    lead = HALO - (CONV_K - 1)
    rc = CONV_ROWS

    @pl.when(j == 0)
    def _():
        _norm_prologue(x_refs, sc_ref, sh_ref, g_ref, h_ref, None, d=1, tm=tm)
        s1_ref[...] = jnp.zeros_like(s1_ref)
        s2_ref[...] = jnp.zeros_like(s2_ref)

    def matmul(c, slot, glu):
        cols = slice(c * MXU_N, (c + 1) * MXU_N)
        accs[slot][...] = jnp.dot(h_ref[...], wa_ref[:, cols], preferred_element_type=F32)
        if glu:
            gates[slot][...] = jnp.dot(h_ref[...], wb_ref[:, cols], preferred_element_type=F32)

    def finish_glu(c, slot):
        for s in range(per_sub):
            lanes = slice(s * LANES, (s + 1) * LANES)
            slab = j * per_step + c * per_sub + s
            stage = (c % 2) * per_sub + s
            ext_ref[stage, 0:HALO, :] = jnp.where(first_of_seq, 0.0, halo_ref[slab])
            for r0 in range(0, tm, rc):
                ext_ref[stage, HALO + r0:HALO + r0 + rc, :] = (
                    accs[slot][r0:r0 + rc, lanes] * _sigmoid(gates[slot][r0:r0 + rc, lanes]))
            halo_ref[slab] = ext_ref[stage, tm:tm + HALO, :]
            for r0 in range(0, tm, rc):
                acc = jnp.broadcast_to(db_ref[slab], (rc, LANES))
                for k in range(CONV_K):
                    acc = acc + dw_ref[slab, k:k + 1, :] * ext_ref[stage, r0 + lead + k:r0 + lead + k + rc, :]
                cv_ref[slab, r0:r0 + rc, :] = acc.astype(BF16)
                dev = acc - off_ref[...]
                s1_ref[r0:r0 + rc, :] += dev
                s2_ref[r0:r0 + rc, :] += dev * dev

    def finish_z(c, slot):
        mu, inv = s1_ref[...], s2_ref[...]
        for s in range(per_sub):
            lanes = slice(s * LANES, (s + 1) * LANES)
            slab = (j - n_glu) * per_step + c * per_sub + s
            t = (cv_ref[slab].astype(F32) - mu) * inv * lng_ref[slab] + lnb_ref[slab]
            y_ref[:, c * MXU_N + s * LANES:c * MXU_N + (s + 1) * LANES] = (
                _silu(t) * _silu(accs[slot][:, lanes])).astype(BF16)

    @pl.when(j < n_glu)
    def _():
        _pipelined_subtiles(tn // MXU_N, functools.partial(matmul, glu=True), finish_glu)

    @pl.when(j == n_glu)
    def _():
        inv_e = 1.0 / CONV_WIDTH
        mu = jnp.sum(s1_ref[...], axis=-1, keepdims=True) * inv_e
        var = jnp.maximum(jnp.sum(s2_ref[...], axis=-1, keepdims=True) * inv_e - mu * mu, 0.0)
        s1_ref[...] = jnp.broadcast_to(mu, s1_ref.shape) + off_ref[...]
        s2_ref[...] = jnp.broadcast_to(lax.rsqrt(var + NORM_EPS), s2_ref.shape)

    @pl.when(j >= n_glu)
    def _():
        _pipelined_subtiles(tn // MXU_N, functools.partial(matmul, glu=False), finish_z)


def _conv_mix(x, scale, shift, norm_g, w_bf16, dw_w, dw_b, ln_g, ln_b, *, tm=CONV_TM, tn=PROJ_TN):
    B, S, D = x.shape
    E = CONV_WIDTH
    nj = E // tn
    n_slabs = E // LANES
    tiles_m = S // tm
    w_slabs = dw_w.reshape(CONV_K, n_slabs, LANES).transpose(1, 0, 2)

    def slabs(v):
        return v.reshape(n_slabs, 1, LANES)

    def whole(shape):
        return pl.BlockSpec(shape, lambda i, j: (0,) * len(shape))

    return pl.pallas_call(
        functools.partial(_conv_mix_kernel, tm=tm, tn=tn, n_glu=nj, tiles_per_seq=tiles_m),
        out_shape=jax.ShapeDtypeStruct((B * S, E), BF16),
        grid=(B * tiles_m, 2 * nj),
        in_specs=_x_specs(tiles_m, B * tiles_m, tm) + [
                  pl.BlockSpec((1, 1, D), lambda i, j: (i // tiles_m, 0, 0)),
                  pl.BlockSpec((1, 1, D), lambda i, j: (i // tiles_m, 0, 0)),
                  pl.BlockSpec((1, D), lambda i, j: (0, 0)),
                  pl.BlockSpec((D, tn), lambda i, j: (0, jnp.where(j < nj, j, nj + j))),
                  pl.BlockSpec((D, tn), lambda i, j: (0, jnp.minimum(nj + j, 2 * nj - 1))),
                  whole((n_slabs, CONV_K, LANES)), whole((n_slabs, 1, LANES)), whole((1, LANES)),
                  whole((n_slabs, 1, LANES)), whole((n_slabs, 1, LANES))],
        out_specs=pl.BlockSpec((tm, tn), lambda i, j: (i, jnp.maximum(j - nj, 0))),
        scratch_shapes=[pltpu.VMEM((tm, D), BF16)] + [pltpu.VMEM((tm, MXU_N), F32)] * 4
        + [pltpu.VMEM((2 * MXU_N // LANES, tm + HALO, LANES), F32),
           pltpu.VMEM((n_slabs, HALO, LANES), F32),
           pltpu.VMEM((n_slabs, tm, LANES), BF16),
           pltpu.VMEM((tm, LANES), F32), pltpu.VMEM((tm, LANES), F32)],
        compiler_params=_params(2),
        name="conv_mix",
    )(*[x] * X_CHUNKS, scale, shift, norm_g, w_bf16, w_bf16, w_slabs, slabs(dw_b),
      jnp.full((1, LANES), jnp.mean(dw_b), F32), slabs(ln_g), slabs(ln_b))


def _rotate_half_matrix():
    half = ROPE_DIM // 2
    k = jnp.arange(HEAD_DIM)[:, None]
    j = jnp.arange(HEAD_DIM)[None, :]
    p = (jnp.where((j < half) & (k == j + half), -1.0, 0.0)
         + jnp.where((j >= half) & (j < ROPE_DIM) & (k == j - half), 1.0, 0.0))
    return jnp.kron(jnp.eye(MXU_N // HEAD_DIM, dtype=F32), p).astype(BF16)


def _rope_tables(positions):
    B, S = positions.shape
    inv_freq = jnp.power(ROPE_THETA, -jnp.arange(0, ROPE_DIM, 2, dtype=F32) / ROPE_DIM)
    ang = positions.astype(F32)[..., None] * inv_freq
    cos, sin = lax.optimization_barrier((jnp.cos(ang), jnp.sin(ang)))
    rest = HEAD_DIM - ROPE_DIM
    cos_t = jnp.concatenate([cos, cos, jnp.ones((B, S, rest), F32)], axis=-1)
    sin_t = jnp.concatenate([sin, sin, jnp.zeros((B, S, rest), F32)], axis=-1)
    return [tuple(t.reshape(B, S // d, d, HEAD_DIM).transpose(0, 2, 1, 3) for t in (cos_t, sin_t))
            for _, d in DIL_GROUPS]


def _rope_gains(q_gain, k_gain):
    half = ROPE_DIM // 2
    gains = jnp.stack([q_gain, k_gain])
    lane = jnp.arange(HEAD_DIM)
    partner = jnp.where(lane < half, jnp.roll(gains, -half, axis=-1), jnp.roll(gains, half, axis=-1))
    return jnp.stack([gains, partner], axis=1)


def kernel(x, c, positions, ada_w, ada_b, norm_g, attn_w_in, attn_q_gain, attn_k_gain, attn_w_out,
           sgu_w_in, sgu_ln_g, sgu_ln_b, sgu_ws, sgu_bs, sgu_w_out,
           conv_w_in, conv_dw_w, conv_dw_b, conv_ln_g, conv_ln_b, conv_w_out):
    B, S, D = x.shape
    mod = _ada_modulation(c, ada_w, ada_b)
    tables = _rope_tables(positions)
    for i in range(DEPTH):
        shift = mod[i, :, None, :D]
        scale = mod[i, :, None, D:2 * D]
        gate = mod[i, :, None, 2 * D:]
        g_row = norm_g[i].reshape(1, D)
        kind, l = i % N_MIXERS, i // N_MIXERS
        if kind == 0:
            gains = _rope_gains(attn_q_gain[l] * (LOG2_E * HEAD_DIM ** -0.5), attn_k_gain[l])
            proj = [_attn_proj(x, scale, shift, g_row, attn_w_in, l, gains, tables[g], group=g,
                               with_gate=(g == 0)) for g in range(N_DIL)]
            y = _attention(*proj).reshape(B * S, ATTN_OUT)
            x = _out_proj(y, attn_w_out[l].astype(BF16), x, gate)
        elif kind == 1:
            proj = _sgu_proj(x, scale, shift, g_row, sgu_w_in, l)
            y = _sgu_core(proj, sgu_ln_g[l], sgu_ln_b[l], sgu_ws[l], sgu_bs[l])
            x = _out_proj(y, sgu_w_out[l].astype(BF16), x, gate)
        else:
            y = _conv_mix(x, scale, shift, g_row, conv_w_in[l].astype(BF16), conv_dw_w[l], conv_dw_b[l],
                          conv_ln_g[l], conv_ln_b[l])
            x = _out_proj(y, conv_w_out[l].astype(BF16), x, gate)
    return x
```

```python
import functools

import jax
import jax.numpy as jnp
from jax import lax
from jax.experimental import pallas as pl
from jax.experimental.pallas import tpu as pltpu

D_MODEL = 2048
DEPTH = 4
N_MIXERS = 3
HEAD_DIM = 128
DIL_GROUPS = ((128, 1), (512, 4), (2048, 16))
N_DIL = len(DIL_GROUPS)
HEADS_PER_GROUP = D_MODEL // HEAD_DIM
ATTN_OUT = HEADS_PER_GROUP * HEAD_DIM
ATTN_QKV = N_DIL * ATTN_OUT
ROPE_DIM = HEAD_DIM // 4
ROPE_THETA = 500000.0
BLOCK = 128
SGU_WIDTH = 2 * D_MODEL
SGU_CHUNK = 128
SGU_GROUPS = 16
CONV_WIDTH = 2 * D_MODEL
CONV_K = 31
NORM_EPS = 1e-6

LANES = 128
MXU_N = 256
SUB = 4
HALO = 32
VMEM_LIMIT = 56 * 1024 * 1024
MASKED = -1e30
LOG2_E = 1.4426950408889634

PROJ_TM, PROJ_TN = 1024, 1024
CONV_TM = 512
OUT_TM = 512
SGU_TC = 512
ADA_ROWS, ADA_TN = 16, 1024
NORM_ROWS = 128
ROPE_ROWS = 32
CONV_ROWS = 64
COMBINE_ROWS = 256
X_CHUNKS = 4

F32 = jnp.float32
BF16 = jnp.bfloat16


def _params(n_axes):
    return pltpu.CompilerParams(dimension_semantics=("arbitrary",) * n_axes,
                                vmem_limit_bytes=VMEM_LIMIT)


def _sigmoid(x):
    return 1.0 / (1.0 + jnp.exp(-x))


def _silu(x):
    return x * _sigmoid(x)


def _gelu_tanh(x):
    return x * (0.5 * (1.0 + jnp.tanh(0.7978845608028654 * (x + 0.044715 * (x * x * x)))))


def _x_specs(tiles_m, n_row_tiles, tm):
    def spec(c):
        def index(i, j):
            t = jnp.minimum(i + jnp.where(j > c, 1, 0), n_row_tiles - 1)
            return (t // tiles_m, t % tiles_m, c)
        return pl.BlockSpec((1, tm, D_MODEL // X_CHUNKS), index)

    return [spec(c) for c in range(X_CHUNKS)]


def _pipelined_subtiles(n_sub, matmul, finish):
    for c in range(n_sub + 1):
        if c < n_sub:
            matmul(c, c % 2)
        if c >= 1:
            finish(c - 1, (c - 1) % 2)


def _split_bf16(v):
    hi = v.astype(BF16)
    return hi, (v - hi.astype(F32)).astype(BF16)


def _ada_kernel(c_ref, w_ref, b_ref, o_ref):
    c_hi, c_lo = _split_bf16(_silu(c_ref[...]))
    w_hi, w_lo = _split_bf16(w_ref[0])
    rows = c_hi.shape[0]
    both = jnp.dot(jnp.concatenate([c_hi, c_lo], axis=0), w_hi, preferred_element_type=F32)
    o_ref[0] = (both[:rows] + both[rows:] + jnp.dot(c_hi, w_lo, preferred_element_type=F32)
                + b_ref[0])


def _ada_modulation(c, ada_w, ada_b):
    B, D = c.shape
    rows, tn = ADA_ROWS, ADA_TN
    c8 = jnp.pad(c, ((0, rows - B), (0, 0)))
    out = pl.pallas_call(
        _ada_kernel,
        out_shape=jax.ShapeDtypeStruct((DEPTH, rows, 3 * D), F32),
        grid=(DEPTH, 3 * D // tn),
        in_specs=[pl.BlockSpec((rows, D), lambda l, j: (0, 0)),
                  pl.BlockSpec((1, D, tn), lambda l, j: (l, 0, j)),
                  pl.BlockSpec((1, 1, tn), lambda l, j: (l, 0, j))],
        out_specs=pl.BlockSpec((1, rows, tn), lambda l, j: (l, 0, j)),
        compiler_params=_params(2),
        name="ada_modulation",
    )(c8, ada_w, ada_b.reshape(DEPTH, 1, 3 * D))
    return out[:, :B]


def _norm_prologue(x_refs, sc_ref, sh_ref, g_ref, h_ref, work_ref, *, d, tm, unrolled=False):
    rc = NORM_ROWS
    a = g_ref[...] * (1.0 + sc_ref[0])
    sh = sh_ref[0]

    def for_row_chunks(body):
        if unrolled:
            for i in range(tm // rc):
                body(pl.ds(i * rc, rc))
        else:
            def step(i, carry):
                body(pl.ds(pl.multiple_of(i * rc, rc), rc))
                return carry

            lax.fori_loop(0, tm // rc, step, 0)

    def x_rows(rows):
        return jnp.concatenate([x_ref[0, rows, :] for x_ref in x_refs], axis=1)

    def x_slab(c):
        per_chunk = D_MODEL // X_CHUNKS // LANES
        lane0 = (c % per_chunk) * LANES
        return x_refs[c // per_chunk][0, :, lane0:lane0 + LANES]

    def inv_rms(x):
        return lax.rsqrt(jnp.mean(x * x, axis=-1, keepdims=True) + NORM_EPS)

    if d == 1:
        def body(rows):
            x = x_rows(rows)
            h_ref[rows, :] = (x * inv_rms(x) * a + sh).astype(BF16)

        for_row_chunks(body)
        return

    INV, NAT, TMP = 0, 1, 2

    def stats(rows):
        work_ref[INV, rows, :] = jnp.broadcast_to(inv_rms(x_rows(rows)), (rc, LANES))

    for_row_chunks(stats)
    rows = tm // d
    for c in range(D_MODEL // LANES):
        cols = slice(c * LANES, (c + 1) * LANES)
        work_ref[NAT] = x_slab(c) * work_ref[INV] * a[:, cols] + sh[:, cols]
        if d <= SUB:
            for r in range(d):
                h_ref[r * rows:(r + 1) * rows, cols] = work_ref[NAT, pl.ds(r, rows, stride=d), :].astype(BF16)
            continue
        part = tm // SUB
        for r_lo in range(SUB):
            work_ref[TMP, r_lo * part:(r_lo + 1) * part, :] = work_ref[NAT, pl.ds(r_lo, part, stride=SUB), :]
        for r in range(d):
            src = pl.ds((r % SUB) * part + r // SUB, rows, stride=d // SUB)
            h_ref[r * rows:(r + 1) * rows, cols] = work_ref[TMP, src, :].astype(BF16)


def _attn_proj_kernel(*refs, d, tm, tn, n_rope, n_tiles):
    x_refs, refs = refs[:X_CHUNKS], refs[X_CHUNKS:]
    (sc_ref, sh_ref, scn_ref, shn_ref, g_ref, w_ref, perm_ref, gain_ref, cos_ref, sin_ref,
     o_ref, h_even, h_odd, acc_a, acc_b, part_ref, *work) = refs
    i, j = pl.program_id(0), pl.program_id(1)
    rows = tm // d
    rc = min(ROPE_ROWS, rows)
    per_res = rows // rc
    bufs = (acc_a, acc_b)
    work_ref = work[0] if work else None

    def finish(c, slot, rope):
        acc_ref, c0 = bufs[slot], c * MXU_N
        if rope:
            part_ref[...] = jnp.dot(acc_ref[...].astype(BF16), perm_ref[...], preferred_element_type=F32)
            gains = gain_ref[j // (n_rope // 2)]
        for ci in range(tm // rc):
            r, m0, p0 = ci // per_res, (ci % per_res) * rc, ci * rc
            if not rope:
                o_ref[0, r, m0:m0 + rc, c0:c0 + MXU_N] = acc_ref[p0:p0 + rc, :].astype(BF16)
                continue
            cos = cos_ref[0, r, m0:m0 + rc, :] * gains[0:1]
            sin = sin_ref[0, r, m0:m0 + rc, :] * gains[1:2]
            for hc in range(MXU_N // LANES):
                cols = slice(hc * LANES, (hc + 1) * LANES)
                a = acc_ref[p0:p0 + rc, cols]
                ms = jnp.sum(a * a, axis=-1, keepdims=True) * (1.0 / HEAD_DIM)
                rot = (a * cos + part_ref[p0:p0 + rc, cols] * sin) * lax.rsqrt(ms + NORM_EPS)
                o_ref[0, r, m0:m0 + rc, c0 + hc * LANES:c0 + (hc + 1) * LANES] = rot.astype(BF16)

    def row_tile(h_ref, h_next):
        def matmul(c, slot):
            bufs[slot][...] = jnp.dot(h_ref[...], w_ref[:, c * MXU_N:(c + 1) * MXU_N].astype(BF16),
                                      preferred_element_type=F32)

        @pl.when((i == 0) & (j == 0))
        def _():
            _norm_prologue(x_refs, sc_ref, sh_ref, g_ref, h_ref, work_ref, d=d, tm=tm)

        @pl.when(j < n_rope)
        def _():
            _pipelined_subtiles(tn // MXU_N, matmul, functools.partial(finish, rope=True))

        @pl.when((j >= n_rope) & (j < n_tiles - 1))
        def _():
            _pipelined_subtiles(tn // MXU_N, matmul, functools.partial(finish, rope=False))

        @pl.when(j == n_tiles - 1)
        def _():
            _pipelined_subtiles(tn // MXU_N, matmul, functools.partial(finish, rope=False))
            _norm_prologue(x_refs, scn_ref, shn_ref, g_ref, h_next, work_ref, d=d, tm=tm, unrolled=True)

    @pl.when(i % 2 == 0)
    def _():
        row_tile(h_even, h_odd)

    @pl.when(i % 2 == 1)
    def _():
        row_tile(h_odd, h_even)


def _attn_proj(x, scale, shift, norm_g, w_in, layer, gains, tables, *, group, with_gate, tm=PROJ_TM, tn=PROJ_TN):
    B, S, D = x.shape
    d = DIL_GROUPS[group][1]
    n_sec = 4 if with_gate else 3
    per_sec = ATTN_OUT // tn
    tiles_m = S // tm

    def w_index(i, j):
        sec = j // per_sec
        base = jnp.where(sec < 3, sec * (ATTN_QKV // tn) + group * per_sec, 3 * (ATTN_QKV // tn))
        return (layer, 0, base + j % per_sec)

    tab_spec = pl.BlockSpec((1, d, tm // d, LANES), lambda i, j: (i // tiles_m, 0, i % tiles_m, 0))
    n_row_tiles = B * tiles_m

    def mod_spec(ahead):
        return pl.BlockSpec((1, 1, D), lambda i, j: (jnp.minimum(i + ahead, n_row_tiles - 1) // tiles_m, 0, 0))

    scratch = [pltpu.VMEM((tm, D), BF16)] * 2 + [pltpu.VMEM((tm, MXU_N), F32)] * 3
    if d > 1:
        scratch.append(pltpu.VMEM((3, tm, LANES), F32))
    return pl.pallas_call(
        functools.partial(_attn_proj_kernel, d=d, tm=tm, tn=tn, n_rope=2 * per_sec, n_tiles=n_sec * per_sec),
        out_shape=jax.ShapeDtypeStruct((B, d, S // d, n_sec * ATTN_OUT), BF16),
        grid=(n_row_tiles, n_sec * per_sec),
        in_specs=_x_specs(tiles_m, n_row_tiles, tm) + [
                  mod_spec(0), mod_spec(0), mod_spec(1), mod_spec(1),
                  pl.BlockSpec((1, D), lambda i, j: (0, 0)),
                  pl.BlockSpec((None, D, tn), w_index),
                  pl.BlockSpec((MXU_N, MXU_N), lambda i, j: (0, 0)),
                  pl.BlockSpec((2, 2, LANES), lambda i, j: (0, 0, 0)),
                  tab_spec, tab_spec],
        out_specs=pl.BlockSpec((1, d, tm // d, tn), lambda i, j: (i // tiles_m, 0, i % tiles_m, j)),
        scratch_shapes=scratch,
        compiler_params=_params(2),
        name=f"attn_proj_g{group}",
    )(*[x] * X_CHUNKS, scale, shift, scale, shift, norm_g, w_in, _rotate_half_matrix(), gains, *tables)


def _attn_block(q, k, v, bias):
    s = lax.dot_general(q, k, (((1,), (1,)), ((), ())), preferred_element_type=F32) + bias
    m = jnp.max(s, axis=-1, keepdims=True)
    p = jnp.exp2(s - m).astype(BF16)
    v_ext = jnp.concatenate([v, jnp.ones(v.shape, BF16)], axis=1)
    ol = jnp.dot(p, v_ext, preferred_element_type=F32)
    l = ol[:, HEAD_DIM:]
    return ol[:, :HEAD_DIM] * (1.0 / l), m + jnp.log2(l)


def _attn_kernel(q0, k0, v0, z0, q1, k1, v1, q2, k2, v2, y_ref, o_scr, lse_scr, bias_ref, *, seq):
    @pl.when((pl.program_id(0) == 0) & (pl.program_id(1) == 0))
    def _():
        dist = (lax.broadcasted_iota(jnp.int32, (BLOCK, 2 * BLOCK), 0) + BLOCK
                - lax.broadcasted_iota(jnp.int32, (BLOCK, 2 * BLOCK), 1))
        bias_ref[0:BLOCK, :] = jnp.where(dist >= 0, jnp.where(dist <= BLOCK, 0.0, MASKED), MASKED).astype(F32)
        row = lax.broadcasted_iota(jnp.int32, (2 * BLOCK, 2 * BLOCK), 0)
        col = lax.broadcasted_iota(jnp.int32, (2 * BLOCK, 2 * BLOCK), 1)
        same_seq = (row // BLOCK) == (col // BLOCK)
        bias_ref[BLOCK:, :] = jnp.where(same_seq, jnp.where(col <= row, 0.0, MASKED), MASKED).astype(F32)

    bias_both = bias_ref[0:BLOCK, :]
    bias_first = bias_ref[0:BLOCK, BLOCK:]
    bias_pair = bias_ref[BLOCK:, :]

    groups = ((q0, k0, v0), (q1, k1, v1), (q2, k2, v2))
    for g, (q_ref, k_ref, v_ref) in enumerate(groups):
        d = DIL_GROUPS[g][1]
        nb = seq // d // BLOCK

        def store(r, n, o, lse, g=g, d=d):
            if d == 1:
                rows = pl.ds(BLOCK * n, BLOCK)
            elif d <= SUB:
                rows = pl.ds(r + d * BLOCK * n, BLOCK, stride=d)
            else:
                rows = pl.ds((r % SUB) * (seq // SUB) + r // SUB + (d // SUB) * BLOCK * n, BLOCK,
                             stride=d // SUB)
            slot = g if d <= SUB else N_DIL
            o_scr[slot, rows, :] = o
            lse_scr[slot, rows, :] = lse

        def first_block(r, q_ref=q_ref, k_ref=k_ref, v_ref=v_ref, store=store):
            o, lse = _attn_block(q_ref[0, r, 0:BLOCK, :], k_ref[0, r, 0:BLOCK, :],
                                 v_ref[0, r, 0:BLOCK, :], bias_first)
            store(r, 0, o, lse)

        def later_block(r, n, q_ref=q_ref, k_ref=k_ref, v_ref=v_ref, store=store):
            q_rows, kv_rows = pl.ds(n * BLOCK, BLOCK), pl.ds((n - 1) * BLOCK, 2 * BLOCK)
            o, lse = _attn_block(q_ref[0, r, q_rows, :], k_ref[0, r, kv_rows, :], v_ref[0, r, kv_rows, :],
                                 bias_both)
            store(r, n, o, lse)

        def block_pair(i, q_ref=q_ref, k_ref=k_ref, v_ref=v_ref, store=store):
            def both(ref):
                return jnp.concatenate([ref[0, 2 * i], ref[0, 2 * i + 1]], axis=0)

            o, lse = _attn_block(both(q_ref), both(k_ref), both(v_ref), bias_pair)
            store(2 * i, 0, o[:BLOCK], lse[:BLOCK])
            store(2 * i + 1, 0, o[BLOCK:], lse[BLOCK:])

        if nb > 1:
            for r in range(d):
                first_block(r)
                for n in range(1, nb):
                    later_block(r, n)
        else:
            for i in range(d // 2):
                block_pair(i)

        if d > SUB:
            part = seq // SUB
            for r_lo in range(SUB):
                for scr in (o_scr, lse_scr):
                    scr[g, pl.ds(r_lo, part, stride=SUB), :] = scr[N_DIL, r_lo * part:(r_lo + 1) * part, :]

    rc = COMBINE_ROWS

    for r0 in range(0, seq, rc):
        rows = slice(r0, r0 + rc)
        l0, l1, l2 = lse_scr[0, rows, :], lse_scr[1, rows, :], lse_scr[2, rows, :]
        m = jnp.maximum(jnp.maximum(l0, l1), l2)
        e0, e1, e2 = jnp.exp2(l0 - m), jnp.exp2(l1 - m), jnp.exp2(l2 - m)
        o = (e0 * o_scr[0, rows, :] + e1 * o_scr[1, rows, :] + e2 * o_scr[2, rows, :]) / (e0 + e1 + e2)
        z = z0[0, 0, rows, :].astype(F32)
        y_ref[0, rows, :] = (o * _silu(z)).astype(BF16)


def _attention(p0, p1, p2):
    B, _, S, _ = p0.shape
    H = HEADS_PER_GROUP

    def spec(d, sec):
        return pl.BlockSpec((1, d, S // d, LANES), lambda b, h: (b, 0, 0, sec * H + h))

    in_specs = [spec(1, 0), spec(1, 1), spec(1, 2), spec(1, 3)]
    for d in (DIL_GROUPS[1][1], DIL_GROUPS[2][1]):
        in_specs += [spec(d, 0), spec(d, 1), spec(d, 2)]
    return pl.pallas_call(
        functools.partial(_attn_kernel, seq=S),
        out_shape=jax.ShapeDtypeStruct((B, S, ATTN_OUT), BF16),
        grid=(B, H),
        in_specs=in_specs,
        out_specs=pl.BlockSpec((1, S, LANES), lambda b, h: (b, 0, h)),
        scratch_shapes=[pltpu.VMEM((N_DIL + 1, S, LANES), F32)] * 2 + [pltpu.VMEM((3 * BLOCK, 2 * BLOCK), F32)],
        compiler_params=_params(2),
        name="dilated_attention",
    )(p0, p0, p0, p0, p1, p1, p1, p2, p2, p2)


def _out_proj_kernel(y_ref, w_ref, x_ref, gate_ref, o_ref, acc_a, acc_b):
    bufs = (acc_a, acc_b)

    def matmul(c, slot):
        bufs[slot][...] = jnp.dot(y_ref[...], w_ref[:, c * MXU_N:(c + 1) * MXU_N],
                                  preferred_element_type=F32)

    def finish(c, slot):
        cols = slice(c * MXU_N, (c + 1) * MXU_N)
        o_ref[:, cols] = x_ref[:, cols] + gate_ref[0][:, cols] * bufs[slot][...]

    _pipelined_subtiles(o_ref.shape[1] // MXU_N, matmul, finish)


def _out_proj(y, w_bf16, x, gate, *, tm=OUT_TM):
    B, S, D = x.shape
    M, K = y.shape
    tiles_m = S // tm
    out = pl.pallas_call(
        _out_proj_kernel,
        out_shape=jax.ShapeDtypeStruct((M, D), F32),
        grid=(M // tm,),
        in_specs=[pl.BlockSpec((tm, K), lambda i: (i, 0)),
                  pl.BlockSpec((K, D), lambda i: (0, 0), pipeline_mode=pl.Buffered(1)),
                  pl.BlockSpec((tm, D), lambda i: (i, 0)),
                  pl.BlockSpec((1, 1, D), lambda i: (i // tiles_m, 0, 0))],
        out_specs=pl.BlockSpec((tm, D), lambda i: (i, 0)),
        scratch_shapes=[pltpu.VMEM((tm, MXU_N), F32)] * 2,
        compiler_params=_params(1),
        name="out_proj",
    )(y, w_bf16, x.reshape(M, D), gate)
    return out.reshape(B, S, D)


def _sgu_proj_kernel(*refs, tm, tn, n_gelu):
    x_refs, (sc_ref, sh_ref, g_ref, w_ref, o_ref, h_ref, acc_a, acc_b) = refs[:X_CHUNKS], refs[X_CHUNKS:]
    j = pl.program_id(1)
    bufs = (acc_a, acc_b)

    @pl.when(j == 0)
    def _():
        _norm_prologue(x_refs, sc_ref, sh_ref, g_ref, h_ref, None, d=1, tm=tm)

    def matmul(c, slot):
        bufs[slot][...] = jnp.dot(h_ref[...], w_ref[:, c * MXU_N:(c + 1) * MXU_N].astype(BF16),
                                  preferred_element_type=F32)

    def finish(c, slot, gelu):
        a = bufs[slot][...]
        o_ref[:, c * MXU_N:(c + 1) * MXU_N] = (_gelu_tanh(a) if gelu else a).astype(BF16)

    @pl.when(j < n_gelu)
    def _():
        _pipelined_subtiles(tn // MXU_N, matmul, functools.partial(finish, gelu=True))

    @pl.when(j >= n_gelu)
    def _():
        _pipelined_subtiles(tn // MXU_N, matmul, functools.partial(finish, gelu=False))


def _sgu_proj(x, scale, shift, norm_g, w_in, layer, *, tm=PROJ_TM, tn=PROJ_TN):
    B, S, D = x.shape
    N = w_in.shape[2]
    tiles_m = S // tm
    return pl.pallas_call(
        functools.partial(_sgu_proj_kernel, tm=tm, tn=tn, n_gelu=2 * SGU_WIDTH // tn),
        out_shape=jax.ShapeDtypeStruct((B * S, N), BF16),
        grid=(B * tiles_m, N // tn),
        in_specs=_x_specs(tiles_m, B * tiles_m, tm) + [
                  pl.BlockSpec((1, 1, D), lambda i, j: (i // tiles_m, 0, 0)),
                  pl.BlockSpec((1, 1, D), lambda i, j: (i // tiles_m, 0, 0)),
                  pl.BlockSpec((1, D), lambda i, j: (0, 0)),
                  pl.BlockSpec((None, D, tn), lambda i, j: (layer, 0, j))],
        out_specs=pl.BlockSpec((tm, tn), lambda i, j: (i, j)),
        scratch_shapes=[pltpu.VMEM((tm, D), BF16)] + [pltpu.VMEM((tm, MXU_N), F32)] * 2,
        compiler_params=_params(2),
        name="sgu_proj",
    )(*[x] * X_CHUNKS, scale, shift, norm_g, w_in)


def _sgu_core_kernel(u_ref, v_ref, z_ref, lng_ref, lnb_ref, ws_ref, bst_ref, y_ref, wm_ref, vn_ref, *, tc):
    t_out = lax.broadcasted_iota(jnp.int32, (SGU_CHUNK, SGU_CHUNK), 0)
    s_in = lax.broadcasted_iota(jnp.int32, (SGU_CHUNK, SGU_CHUNK), 1)
    causal = s_in <= t_out
    for g in range(SGU_GROUPS):
        wm_ref[g] = jnp.where(causal, ws_ref[g], 0.0).astype(BF16)
    gw = SGU_WIDTH // SGU_GROUPS

    def chunk(ci, carry):
        rows = pl.ds(pl.multiple_of(ci * SGU_CHUNK, SGU_CHUNK), SGU_CHUNK)
        v = v_ref[rows, :].astype(F32)
        mu = jnp.mean(v, axis=-1, keepdims=True)
        vc = v - mu
        var = jnp.mean(vc * vc, axis=-1, keepdims=True)
        vn_ref[...] = (vc * lax.rsqrt(var + NORM_EPS) * lng_ref[...] + lnb_ref[...]).astype(BF16)
        for g in range(SGU_GROUPS):
            cols = slice(g * gw, (g + 1) * gw)
            sv = jnp.dot(wm_ref[g], vn_ref[:, cols], preferred_element_type=F32) + bst_ref[:, g:g + 1]
            z = z_ref[rows, cols].astype(F32)
            y_ref[rows, cols] = (u_ref[rows, cols].astype(F32) * sv * _silu(z)).astype(BF16)
        return carry

    lax.fori_loop(0, tc // SGU_CHUNK, chunk, 0)


def _sgu_core(proj, ln_g, ln_b, ws, bs, *, tc=SGU_TC):
    M = proj.shape[0]
    E = SGU_WIDTH
    return pl.pallas_call(
        functools.partial(_sgu_core_kernel, tc=tc),
        out_shape=jax.ShapeDtypeStruct((M, E), BF16),
        grid=(M // tc,),
        in_specs=[pl.BlockSpec((tc, E), lambda i: (i, 0)),
                  pl.BlockSpec((tc, E), lambda i: (i, 1)),
                  pl.BlockSpec((tc, E), lambda i: (i, 2)),
                  pl.BlockSpec((1, E), lambda i: (0, 0)),
                  pl.BlockSpec((1, E), lambda i: (0, 0)),
                  pl.BlockSpec((SGU_GROUPS, SGU_CHUNK, SGU_CHUNK), lambda i: (0, 0, 0)),
                  pl.BlockSpec((SGU_CHUNK, SGU_GROUPS), lambda i: (0, 0))],
        out_specs=pl.BlockSpec((tc, E), lambda i: (i, 0)),
        scratch_shapes=[pltpu.VMEM((SGU_GROUPS, SGU_CHUNK, SGU_CHUNK), BF16),
                        pltpu.VMEM((SGU_CHUNK, E), BF16)],
        compiler_params=_params(1),
        name="sgu_core",
    )(proj, proj, proj, ln_g.reshape(1, E), ln_b.reshape(1, E), ws, bs.T)


def _conv_mix_kernel(*refs, tm, tn, n_glu, tiles_per_seq):
    x_refs, refs = refs[:X_CHUNKS], refs[X_CHUNKS:]
    (sc_ref, sh_ref, g_ref, wa_ref, wb_ref, dw_ref, db_ref, off_ref, lng_ref, lnb_ref,
     y_ref, h_ref, acc_a, acc_b, gate_a, gate_b, ext_ref, halo_ref, cv_ref, s1_ref, s2_ref) = refs
    i, j = pl.program_id(0), pl.program_id(1)
    per_sub = MXU_N // LANES
    per_step = tn // LANES
    accs, gates = (acc_a, acc_b), (gate_a, gate_b)
    first_of_seq = (i % tiles_per_seq) == 0
    lead = HALO - (CONV_K - 1)
    rc = CONV_ROWS

    @pl.when(j == 0)
    def _():
        _norm_prologue(x_refs, sc_ref, sh_ref, g_ref, h_ref, None, d=1, tm=tm)
        s1_ref[...] = jnp.zeros_like(s1_ref)
        s2_ref[...] = jnp.zeros_like(s2_ref)

    def matmul(c, slot, glu):
        cols = slice(c * MXU_N, (c + 1) * MXU_N)
        accs[slot][...] = jnp.dot(h_ref[...], wa_ref[:, cols], preferred_element_type=F32)
        if glu:
            gates[slot][...] = jnp.dot(h_ref[...], wb_ref[:, cols], preferred_element_type=F32)

    def finish_glu(c, slot):
        for s in range(per_sub):
            lanes = slice(s * LANES, (s + 1) * LANES)
            slab = j * per_step + c * per_sub + s
            stage = (c % 2) * per_sub + s
            ext_ref[stage, 0:HALO, :] = jnp.where(first_of_seq, 0.0, halo_ref[slab])
            for r0 in range(0, tm, rc):
                ext_ref[stage, HALO + r0:HALO + r0 + rc, :] = (
                    accs[slot][r0:r0 + rc, lanes] * _sigmoid(gates[slot][r0:r0 + rc, lanes]))
            halo_ref[slab] = ext_ref[stage, tm:tm + HALO, :]
            for r0 in range(0, tm, rc):
                acc = jnp.broadcast_to(db_ref[slab], (rc, LANES))
                for k in range(CONV_K):
                    acc = acc + dw_ref[slab, k:k + 1, :] * ext_ref[stage, r0 + lead + k:r0 + lead + k + rc, :]
                cv_ref[slab, r0:r0 + rc, :] = acc.astype(BF16)
                dev = acc - off_ref[...]
                s1_ref[r0:r0 + rc, :] += dev
                s2_ref[r0:r0 + rc, :] += dev * dev

    def finish_z(c, slot):
        mu, inv = s1_ref[...], s2_ref[...]
        for s in range(per_sub):
            lanes = slice(s * LANES, (s + 1) * LANES)
            slab = (j - n_glu) * per_step + c * per_sub + s
            t = (cv_ref[slab].astype(F32) - mu) * inv * lng_ref[slab] + lnb_ref[slab]
            y_ref[:, c * MXU_N + s * LANES:c * MXU_N + (s + 1) * LANES] = (
                _silu(t) * _silu(accs[slot][:, lanes])).astype(BF16)

    @pl.when(j < n_glu)
    def _():
        _pipelined_subtiles(tn // MXU_N, functools.partial(matmul, glu=True), finish_glu)

    @pl.when(j == n_glu)
    def _():
        inv_e = 1.0 / CONV_WIDTH
        mu = jnp.sum(s1_ref[...], axis=-1, keepdims=True) * inv_e
        var = jnp.maximum(jnp.sum(s2_ref[...], axis=-1, keepdims=True) * inv_e - mu * mu, 0.0)
        s1_ref[...] = jnp.broadcast_to(mu, s1_ref.shape) + off_ref[...]
        s2_ref[...] = jnp.broadcast_to(lax.rsqrt(var + NORM_EPS), s2_ref.shape)

    @pl.when(j >= n_glu)
    def _():
        _pipelined_subtiles(tn // MXU_N, functools.partial(matmul, glu=False), finish_z)


def _conv_mix(x, scale, shift, norm_g, w_bf16, dw_w, dw_b, ln_g, ln_b, *, tm=CONV_TM, tn=PROJ_TN):
    B, S, D = x.shape
    E = CONV_WIDTH
    nj = E // tn
    n_slabs = E // LANES
    tiles_m = S // tm
    w_slabs = dw_w.reshape(CONV_K, n_slabs, LANES).transpose(1, 0, 2)

    def slabs(v):
        return v.reshape(n_slabs, 1, LANES)

    def whole(shape):
        return pl.BlockSpec(shape, lambda i, j: (0,) * len(shape))

    return pl.pallas_call(
        functools.partial(_conv_mix_kernel, tm=tm, tn=tn, n_glu=nj, tiles_per_seq=tiles_m),
        out_shape=jax.ShapeDtypeStruct((B * S, E), BF16),
        grid=(B * tiles_m, 2 * nj),
        in_specs=_x_specs(tiles_m, B * tiles_m, tm) + [
                  pl.BlockSpec((1, 1, D), lambda i, j: (i // tiles_m, 0, 0)),
                  pl.BlockSpec((1, 1, D), lambda i, j: (i // tiles_m, 0, 0)),
                  pl.BlockSpec((1, D), lambda i, j: (0, 0)),
                  pl.BlockSpec((D, tn), lambda i, j: (0, jnp.where(j < nj, j, nj + j))),
                  pl.BlockSpec((D, tn), lambda i, j: (0, jnp.minimum(nj + j, 2 * nj - 1))),
                  whole((n_slabs, CONV_K, LANES)), whole((n_slabs, 1, LANES)), whole((1, LANES)),
                  whole((n_slabs, 1, LANES)), whole((n_slabs, 1, LANES))],
        out_specs=pl.BlockSpec((tm, tn), lambda i, j: (i, jnp.maximum(j - nj, 0))),
        scratch_shapes=[pltpu.VMEM((tm, D), BF16)] + [pltpu.VMEM((tm, MXU_N), F32)] * 4
        + [pltpu.VMEM((2 * MXU_N // LANES, tm + HALO, LANES), F32),
           pltpu.VMEM((n_slabs, HALO, LANES), F32),
           pltpu.VMEM((n_slabs, tm, LANES), BF16),
           pltpu.VMEM((tm, LANES), F32), pltpu.VMEM((tm, LANES), F32)],
        compiler_params=_params(2),
        name="conv_mix",
    )(*[x] * X_CHUNKS, scale, shift, norm_g, w_bf16, w_bf16, w_slabs, slabs(dw_b),
      jnp.full((1, LANES), jnp.mean(dw_b), F32), slabs(ln_g), slabs(ln_b))


def _rotate_half_matrix():
    half = ROPE_DIM // 2
    k = jnp.arange(HEAD_DIM)[:, None]
    j = jnp.arange(HEAD_DIM)[None, :]
    p = (jnp.where((j < half) & (k == j + half), -1.0, 0.0)
         + jnp.where((j >= half) & (j < ROPE_DIM) & (k == j - half), 1.0, 0.0))
    return jnp.kron(jnp.eye(MXU_N // HEAD_DIM, dtype=F32), p).astype(BF16)


def _rope_tables(positions):
    B, S = positions.shape
    inv_freq = jnp.power(ROPE_THETA, -jnp.arange(0, ROPE_DIM, 2, dtype=F32) / ROPE_DIM)
    ang = positions.astype(F32)[..., None] * inv_freq
    cos, sin = lax.optimization_barrier((jnp.cos(ang), jnp.sin(ang)))
    rest = HEAD_DIM - ROPE_DIM
    cos_t = jnp.concatenate([cos, cos, jnp.ones((B, S, rest), F32)], axis=-1)
    sin_t = jnp.concatenate([sin, sin, jnp.zeros((B, S, rest), F32)], axis=-1)
    return [tuple(t.reshape(B, S // d, d, HEAD_DIM).transpose(0, 2, 1, 3) for t in (cos_t, sin_t))
            for _, d in DIL_GROUPS]


def _rope_gains(q_gain, k_gain):
    half = ROPE_DIM // 2
    gains = jnp.stack([q_gain, k_gain])
    lane = jnp.arange(HEAD_DIM)
    partner = jnp.where(lane < half, jnp.roll(gains, -half, axis=-1), jnp.roll(gains, half, axis=-1))
    return jnp.stack([gains, partner], axis=1)


def kernel(x, c, positions, ada_w, ada_b, norm_g, attn_w_in, attn_q_gain, attn_k_gain, attn_w_out,
           sgu_w_in, sgu_ln_g, sgu_ln_b, sgu_ws, sgu_bs, sgu_w_out,
           conv_w_in, conv_dw_w, conv_dw_b, conv_ln_g, conv_ln_b, conv_w_out):
    B, S, D = x.shape
    mod = _ada_modulation(c, ada_w, ada_b)
    tables = _rope_tables(positions)
    for i in range(DEPTH):
        shift = mod[i, :, None, :D]
        scale = mod[i, :, None, D:2 * D]
        gate = mod[i, :, None, 2 * D:]
        g_row = norm_g[i].reshape(1, D)
        kind, l = i % N_MIXERS, i // N_MIXERS
        if kind == 0:
            gains = _rope_gains(attn_q_gain[l] * (LOG2_E * HEAD_DIM ** -0.5), attn_k_gain[l])
            proj = [_attn_proj(x, scale, shift, g_row, attn_w_in, l, gains, tables[g], group=g,
                               with_gate=(g == 0)) for g in range(N_DIL)]
            y = _attention(*proj).reshape(B * S, ATTN_OUT)
            x = _out_proj(y, attn_w_out[l].astype(BF16), x, gate)
        elif kind == 1:
            proj = _sgu_proj(x, scale, shift, g_row, sgu_w_in, l)
            y = _sgu_core(proj, sgu_ln_g[l], sgu_ln_b[l], sgu_ws[l], sgu_bs[l])
            x = _out_proj(y, sgu_w_out[l].astype(BF16), x, gate)
        else:
            y = _conv_mix(x, scale, shift, g_row, conv_w_in[l].astype(BF16), conv_dw_w[l], conv_dw_b[l],
                          conv_ln_g[l], conv_ln_b[l])
            x = _out_proj(y, conv_w_out[l].astype(BF16), x, gate)
    return x
```

```python
import functools

import jax
import jax.numpy as jnp
from jax import lax
from jax.experimental import pallas as pl
from jax.experimental.pallas import tpu as pltpu

D_MODEL = 2048
DEPTH = 4
N_MIXERS = 3
HEAD_DIM = 128
DIL_GROUPS = ((128, 1), (512, 4), (2048, 16))
N_DIL = len(DIL_GROUPS)
HEADS_PER_GROUP = D_MODEL // HEAD_DIM
ATTN_OUT = HEADS_PER_GROUP * HEAD_DIM
ATTN_QKV = N_DIL * ATTN_OUT
ROPE_DIM = HEAD_DIM // 4
ROPE_THETA = 500000.0
BLOCK = 128
SGU_WIDTH = 2 * D_MODEL
SGU_CHUNK = 128
SGU_GROUPS = 16
CONV_WIDTH = 2 * D_MODEL
CONV_K = 31
NORM_EPS = 1e-6

LANES = 128
MXU_N = 256
SUB = 4
HALO = 32
VMEM_LIMIT = 56 * 1024 * 1024
MASKED = -1e30
LOG2_E = 1.4426950408889634

PROJ_TM, PROJ_TN = 1024, 1024
CONV_TM = 512
OUT_TM = 512
SGU_TC = 512
ADA_ROWS, ADA_TN = 16, 1024
NORM_ROWS = 128
ROPE_ROWS = 32
CONV_ROWS = 64
COMBINE_ROWS = 256
X_CHUNKS = 4

F32 = jnp.float32
BF16 = jnp.bfloat16


def _params(n_axes):
    return pltpu.CompilerParams(dimension_semantics=("arbitrary",) * n_axes,
                                vmem_limit_bytes=VMEM_LIMIT)


def _sigmoid(x):
    return 1.0 / (1.0 + jnp.exp(-x))


def _silu(x):
    return x * _sigmoid(x)


def _gelu_tanh(x):
    return x * (0.5 * (1.0 + jnp.tanh(0.7978845608028654 * (x + 0.044715 * (x * x * x)))))


def _x_specs(tiles_m, n_row_tiles, tm, n_col_steps):
    assert n_col_steps > X_CHUNKS

    def spec(c):
        def index(i, j):
            t = jnp.minimum(i + jnp.where(j > c, 1, 0), n_row_tiles - 1)
            return (t // tiles_m, t % tiles_m, c)
        return pl.BlockSpec((1, tm, D_MODEL // X_CHUNKS), index)

    return [spec(c) for c in range(X_CHUNKS)]


def _pipelined_subtiles(n_sub, matmul, finish):
    for c in range(n_sub + 1):
        if c < n_sub:
            matmul(c, c % 2)
        if c >= 1:
            finish(c - 1, (c - 1) % 2)


def _split_bf16(v):
    hi = v.astype(BF16)
    return hi, (v - hi.astype(F32)).astype(BF16)


def _ada_kernel(c_ref, w_ref, b_ref, o_ref):
    c_hi, c_lo = _split_bf16(_silu(c_ref[...]))
    w_hi, w_lo = _split_bf16(w_ref[0])
    rows = c_hi.shape[0]
    both = jnp.dot(jnp.concatenate([c_hi, c_lo], axis=0), w_hi, preferred_element_type=F32)
    o_ref[0] = (both[:rows] + both[rows:] + jnp.dot(c_hi, w_lo, preferred_element_type=F32)
                + b_ref[0])


def _ada_modulation(c, ada_w, ada_b):
    B, D = c.shape
    rows, tn = ADA_ROWS, ADA_TN
    c8 = jnp.pad(c, ((0, rows - B), (0, 0)))
    out = pl.pallas_call(
        _ada_kernel,
        out_shape=jax.ShapeDtypeStruct((DEPTH, rows, 3 * D), F32),
        grid=(DEPTH, 3 * D // tn),
        in_specs=[pl.BlockSpec((rows, D), lambda l, j: (0, 0)),
                  pl.BlockSpec((1, D, tn), lambda l, j: (l, 0, j)),
                  pl.BlockSpec((1, 1, tn), lambda l, j: (l, 0, j))],
        out_specs=pl.BlockSpec((1, rows, tn), lambda l, j: (l, 0, j)),
        compiler_params=_params(2),
        name="ada_modulation",
    )(c8, ada_w, ada_b.reshape(DEPTH, 1, 3 * D))
    return out[:, :B]


def _norm_prologue(x_refs, sc_ref, sh_ref, g_ref, h_ref, work_ref, *, d, tm):
    rc = NORM_ROWS
    a = g_ref[...] * (1.0 + sc_ref[0])
    sh = sh_ref[0]

    def chunk_rows(i):
        return pl.ds(pl.multiple_of(i * rc, rc), rc)

    def x_rows(rows):
        return jnp.concatenate([x_ref[0, rows, :] for x_ref in x_refs], axis=1)

    def x_slab(c):
        per_chunk = D_MODEL // X_CHUNKS // LANES
        lane0 = (c % per_chunk) * LANES
        return x_refs[c // per_chunk][0, :, lane0:lane0 + LANES]

    def inv_rms(x):
        return lax.rsqrt(jnp.mean(x * x, axis=-1, keepdims=True) + NORM_EPS)

    if d == 1:
        def body(i, carry):
            x = x_rows(chunk_rows(i))
            h_ref[chunk_rows(i), :] = (x * inv_rms(x) * a + sh).astype(BF16)
            return carry

        lax.fori_loop(0, tm // rc, body, 0)
        return

    INV, NAT, TMP = 0, 1, 2

    def stats(i, carry):
        work_ref[INV, chunk_rows(i), :] = jnp.broadcast_to(inv_rms(x_rows(chunk_rows(i))), (rc, LANES))
        return carry

    lax.fori_loop(0, tm // rc, stats, 0)
    rows = tm // d
    for c in range(D_MODEL // LANES):
        cols = slice(c * LANES, (c + 1) * LANES)
        work_ref[NAT] = x_slab(c) * work_ref[INV] * a[:, cols] + sh[:, cols]
        if d <= SUB:
            for r in range(d):
                h_ref[r * rows:(r + 1) * rows, cols] = work_ref[NAT, pl.ds(r, rows, stride=d), :].astype(BF16)
            continue
        part = tm // SUB
        for r_lo in range(SUB):
            work_ref[TMP, r_lo * part:(r_lo + 1) * part, :] = work_ref[NAT, pl.ds(r_lo, part, stride=SUB), :]
        for r in range(d):
            src = pl.ds((r % SUB) * part + r // SUB, rows, stride=d // SUB)
            h_ref[r * rows:(r + 1) * rows, cols] = work_ref[TMP, src, :].astype(BF16)


def _attn_proj_kernel(*refs, d, tm, tn, n_rope):
    x_refs, refs = refs[:X_CHUNKS], refs[X_CHUNKS:]
    (sc_ref, sh_ref, g_ref, w_ref, perm_ref, gain_ref, cos_ref, sin_ref,
     o_ref, h_ref, acc_a, acc_b, part_ref, *work) = refs
    j = pl.program_id(1)
    rows = tm // d
    rc = min(ROPE_ROWS, rows)
    per_res = rows // rc
    bufs = (acc_a, acc_b)

    @pl.when(j == 0)
    def _():
        _norm_prologue(x_refs, sc_ref, sh_ref, g_ref, h_ref, work[0] if work else None, d=d, tm=tm)

    def matmul(c, slot):
        bufs[slot][...] = jnp.dot(h_ref[...], w_ref[:, c * MXU_N:(c + 1) * MXU_N].astype(BF16),
                                  preferred_element_type=F32)

    def finish(c, slot, rope):
        acc_ref, c0 = bufs[slot], c * MXU_N
        if rope:
            part_ref[...] = jnp.dot(acc_ref[...].astype(BF16), perm_ref[...], preferred_element_type=F32)
            gains = gain_ref[j // (n_rope // 2)]
        for ci in range(tm // rc):
            r, m0, p0 = ci // per_res, (ci % per_res) * rc, ci * rc
            if not rope:
                o_ref[0, r, m0:m0 + rc, c0:c0 + MXU_N] = acc_ref[p0:p0 + rc, :].astype(BF16)
                continue
            cos = cos_ref[0, r, m0:m0 + rc, :] * gains[0:1]
            sin = sin_ref[0, r, m0:m0 + rc, :] * gains[1:2]
            for hc in range(MXU_N // LANES):
                cols = slice(hc * LANES, (hc + 1) * LANES)
                a = acc_ref[p0:p0 + rc, cols]
                ms = jnp.sum(a * a, axis=-1, keepdims=True) * (1.0 / HEAD_DIM)
                rot = (a * cos + part_ref[p0:p0 + rc, cols] * sin) * lax.rsqrt(ms + NORM_EPS)
                o_ref[0, r, m0:m0 + rc, c0 + hc * LANES:c0 + (hc + 1) * LANES] = rot.astype(BF16)

    @pl.when(j < n_rope)
    def _():
        _pipelined_subtiles(tn // MXU_N, matmul, functools.partial(finish, rope=True))

    @pl.when(j >= n_rope)
    def _():
        _pipelined_subtiles(tn // MXU_N, matmul, functools.partial(finish, rope=False))


def _attn_proj(x, scale, shift, norm_g, w_in, layer, gains, tables, *, group, with_gate, tm=PROJ_TM, tn=PROJ_TN):
    B, S, D = x.shape
    d = DIL_GROUPS[group][1]
    n_sec = 4 if with_gate else 3
    per_sec = ATTN_OUT // tn
    tiles_m = S // tm

    def w_index(i, j):
        sec = j // per_sec
        base = jnp.where(sec < 3, sec * (ATTN_QKV // tn) + group * per_sec, 3 * (ATTN_QKV // tn))
        return (layer, 0, base + j % per_sec)

    tab_spec = pl.BlockSpec((1, d, tm // d, LANES), lambda i, j: (i // tiles_m, 0, i % tiles_m, 0))
    scratch = [pltpu.VMEM((tm, D), BF16)] + [pltpu.VMEM((tm, MXU_N), F32)] * 3
    if d > 1:
        scratch.append(pltpu.VMEM((3, tm, LANES), F32))
    return pl.pallas_call(
        functools.partial(_attn_proj_kernel, d=d, tm=tm, tn=tn, n_rope=2 * per_sec),
        out_shape=jax.ShapeDtypeStruct((B, d, S // d, n_sec * ATTN_OUT), BF16),
        grid=(B * tiles_m, n_sec * per_sec),
        in_specs=_x_specs(tiles_m, B * tiles_m, tm, n_sec * per_sec) + [
                  pl.BlockSpec((1, 1, D), lambda i, j: (i // tiles_m, 0, 0)),
                  pl.BlockSpec((1, 1, D), lambda i, j: (i // tiles_m, 0, 0)),
                  pl.BlockSpec((1, D), lambda i, j: (0, 0)),
                  pl.BlockSpec((None, D, tn), w_index),
                  pl.BlockSpec((MXU_N, MXU_N), lambda i, j: (0, 0)),
                  pl.BlockSpec((2, 2, LANES), lambda i, j: (0, 0, 0)),
                  tab_spec, tab_spec],
        out_specs=pl.BlockSpec((1, d, tm // d, tn), lambda i, j: (i // tiles_m, 0, i % tiles_m, j)),
        scratch_shapes=scratch,
        compiler_params=_params(2),
        name=f"attn_proj_g{group}",
    )(*[x] * X_CHUNKS, scale, shift, norm_g, w_in, _rotate_half_matrix(), gains, *tables)


def _attn_block(q, k, v, bias):
    s = lax.dot_general(q, k, (((1,), (1,)), ((), ())), preferred_element_type=F32) + bias
    m = jnp.max(s, axis=-1, keepdims=True)
    p = jnp.exp2(s - m).astype(BF16)
    v_ext = jnp.concatenate([v, jnp.ones(v.shape, BF16)], axis=1)
    ol = jnp.dot(p, v_ext, preferred_element_type=F32)
    l = ol[:, HEAD_DIM:]
    return ol[:, :HEAD_DIM] * (1.0 / l), m + jnp.log2(l)


def _attn_kernel(q0, k0, v0, z0, q1, k1, v1, q2, k2, v2, y_ref, o_scr, lse_scr, bias_ref, *, seq):
    @pl.when((pl.program_id(0) == 0) & (pl.program_id(1) == 0))
    def _():
        dist = (lax.broadcasted_iota(jnp.int32, (BLOCK, 2 * BLOCK), 0) + BLOCK
                - lax.broadcasted_iota(jnp.int32, (BLOCK, 2 * BLOCK), 1))
        bias_ref[0:BLOCK, :] = jnp.where(dist >= 0, jnp.where(dist <= BLOCK, 0.0, MASKED), MASKED).astype(F32)
        row = lax.broadcasted_iota(jnp.int32, (2 * BLOCK, 2 * BLOCK), 0)
        col = lax.broadcasted_iota(jnp.int32, (2 * BLOCK, 2 * BLOCK), 1)
        same_seq = (row // BLOCK) == (col // BLOCK)
        bias_ref[BLOCK:, :] = jnp.where(same_seq, jnp.where(col <= row, 0.0, MASKED), MASKED).astype(F32)

    bias_both = bias_ref[0:BLOCK, :]
    bias_first = bias_ref[0:BLOCK, BLOCK:]
    bias_pair = bias_ref[BLOCK:, :]

    groups = ((q0, k0, v0), (q1, k1, v1), (q2, k2, v2))
    for g, (q_ref, k_ref, v_ref) in enumerate(groups):
        d = DIL_GROUPS[g][1]
        nb = seq // d // BLOCK

        def store(r, n, o, lse, g=g, d=d):
            if d == 1:
                rows = pl.ds(BLOCK * n, BLOCK)
            elif d <= SUB:
                rows = pl.ds(r + d * BLOCK * n, BLOCK, stride=d)
            else:
                rows = pl.ds((r % SUB) * (seq // SUB) + r // SUB + (d // SUB) * BLOCK * n, BLOCK,
                             stride=d // SUB)
            slot = g if d <= SUB else N_DIL
            o_scr[slot, rows, :] = o
            lse_scr[slot, rows, :] = lse

        def first_block(r, q_ref=q_ref, k_ref=k_ref, v_ref=v_ref, store=store):
            o, lse = _attn_block(q_ref[0, r, 0:BLOCK, :], k_ref[0, r, 0:BLOCK, :],
                                 v_ref[0, r, 0:BLOCK, :], bias_first)
            store(r, 0, o, lse)

        def later_block(r, n, q_ref=q_ref, k_ref=k_ref, v_ref=v_ref, store=store):
            q_rows, kv_rows = pl.ds(n * BLOCK, BLOCK), pl.ds((n - 1) * BLOCK, 2 * BLOCK)
            o, lse = _attn_block(q_ref[0, r, q_rows, :], k_ref[0, r, kv_rows, :], v_ref[0, r, kv_rows, :],
                                 bias_both)
            store(r, n, o, lse)

        def block_pair(i, q_ref=q_ref, k_ref=k_ref, v_ref=v_ref, store=store):
            def both(ref):
                return jnp.concatenate([ref[0, 2 * i], ref[0, 2 * i + 1]], axis=0)

            o, lse = _attn_block(both(q_ref), both(k_ref), both(v_ref), bias_pair)
            store(2 * i, 0, o[:BLOCK], lse[:BLOCK])
            store(2 * i + 1, 0, o[BLOCK:], lse[BLOCK:])

        if nb > 1:
            for r in range(d):
                first_block(r)
                for n in range(1, nb):
                    later_block(r, n)
        else:
            for i in range(d // 2):
                block_pair(i)

        if d > SUB:
            part = seq // SUB
            for r_lo in range(SUB):
                for scr in (o_scr, lse_scr):
                    scr[g, pl.ds(r_lo, part, stride=SUB), :] = scr[N_DIL, r_lo * part:(r_lo + 1) * part, :]

    rc = COMBINE_ROWS

    for r0 in range(0, seq, rc):
        rows = slice(r0, r0 + rc)
        l0, l1, l2 = lse_scr[0, rows, :], lse_scr[1, rows, :], lse_scr[2, rows, :]
        m = jnp.maximum(jnp.maximum(l0, l1), l2)
        e0, e1, e2 = jnp.exp2(l0 - m), jnp.exp2(l1 - m), jnp.exp2(l2 - m)
        o = (e0 * o_scr[0, rows, :] + e1 * o_scr[1, rows, :] + e2 * o_scr[2, rows, :]) / (e0 + e1 + e2)
        z = z0[0, 0, rows, :].astype(F32)
        y_ref[0, rows, :] = (o * _silu(z)).astype(BF16)


def _attention(p0, p1, p2):
    B, _, S, _ = p0.shape
    H = HEADS_PER_GROUP

    def spec(d, sec):
        return pl.BlockSpec((1, d, S // d, LANES), lambda b, h: (b, 0, 0, sec * H + h))

    in_specs = [spec(1, 0), spec(1, 1), spec(1, 2), spec(1, 3)]
    for d in (DIL_GROUPS[1][1], DIL_GROUPS[2][1]):
        in_specs += [spec(d, 0), spec(d, 1), spec(d, 2)]
    return pl.pallas_call(
        functools.partial(_attn_kernel, seq=S),
        out_shape=jax.ShapeDtypeStruct((B, S, ATTN_OUT), BF16),
        grid=(B, H),
        in_specs=in_specs,
        out_specs=pl.BlockSpec((1, S, LANES), lambda b, h: (b, 0, h)),
        scratch_shapes=[pltpu.VMEM((N_DIL + 1, S, LANES), F32)] * 2 + [pltpu.VMEM((3 * BLOCK, 2 * BLOCK), F32)],
        compiler_params=_params(2),
        name="dilated_attention",
    )(p0, p0, p0, p0, p1, p1, p1, p2, p2, p2)


def _out_proj_kernel(y_ref, w_ref, x_ref, gate_ref, o_ref, acc_a, acc_b):
    bufs = (acc_a, acc_b)

    def matmul(c, slot):
        bufs[slot][...] = jnp.dot(y_ref[...], w_ref[:, c * MXU_N:(c + 1) * MXU_N],
                                  preferred_element_type=F32)

    def finish(c, slot):
        cols = slice(c * MXU_N, (c + 1) * MXU_N)
        o_ref[:, cols] = x_ref[:, cols] + gate_ref[0][:, cols] * bufs[slot][...]

    _pipelined_subtiles(o_ref.shape[1] // MXU_N, matmul, finish)


def _out_proj(y, w_bf16, x, gate, *, tm=OUT_TM):
    B, S, D = x.shape
    M, K = y.shape
    tiles_m = S // tm
    out = pl.pallas_call(
        _out_proj_kernel,
        out_shape=jax.ShapeDtypeStruct((M, D), F32),
        grid=(M // tm,),
        in_specs=[pl.BlockSpec((tm, K), lambda i: (i, 0)),
                  pl.BlockSpec((K, D), lambda i: (0, 0), pipeline_mode=pl.Buffered(1)),
                  pl.BlockSpec((tm, D), lambda i: (i, 0)),
                  pl.BlockSpec((1, 1, D), lambda i: (i // tiles_m, 0, 0))],
        out_specs=pl.BlockSpec((tm, D), lambda i: (i, 0)),
        scratch_shapes=[pltpu.VMEM((tm, MXU_N), F32)] * 2,
        compiler_params=_params(1),
        name="out_proj",
    )(y, w_bf16, x.reshape(M, D), gate)
    return out.reshape(B, S, D)


def _sgu_proj_kernel(*refs, tm, tn, n_gelu):
    x_refs, (sc_ref, sh_ref, g_ref, w_ref, o_ref, h_ref, acc_a, acc_b) = refs[:X_CHUNKS], refs[X_CHUNKS:]
    j = pl.program_id(1)
    bufs = (acc_a, acc_b)

    @pl.when(j == 0)
    def _():
        _norm_prologue(x_refs, sc_ref, sh_ref, g_ref, h_ref, None, d=1, tm=tm)

    def matmul(c, slot):
        bufs[slot][...] = jnp.dot(h_ref[...], w_ref[:, c * MXU_N:(c + 1) * MXU_N].astype(BF16),
                                  preferred_element_type=F32)

    def finish(c, slot, gelu):
        a = bufs[slot][...]
        o_ref[:, c * MXU_N:(c + 1) * MXU_N] = (_gelu_tanh(a) if gelu else a).astype(BF16)

    @pl.when(j < n_gelu)
    def _():
        _pipelined_subtiles(tn // MXU_N, matmul, functools.partial(finish, gelu=True))

    @pl.when(j >= n_gelu)
    def _():
        _pipelined_subtiles(tn // MXU_N, matmul, functools.partial(finish, gelu=False))


def _sgu_proj(x, scale, shift, norm_g, w_in, layer, *, tm=PROJ_TM, tn=PROJ_TN):
    B, S, D = x.shape
    N = w_in.shape[2]
    tiles_m = S // tm
    return pl.pallas_call(
        functools.partial(_sgu_proj_kernel, tm=tm, tn=tn, n_gelu=2 * SGU_WIDTH // tn),
        out_shape=jax.ShapeDtypeStruct((B * S, N), BF16),
        grid=(B * tiles_m, N // tn),
        in_specs=_x_specs(tiles_m, B * tiles_m, tm, N // tn) + [
                  pl.BlockSpec((1, 1, D), lambda i, j: (i // tiles_m, 0, 0)),
                  pl.BlockSpec((1, 1, D), lambda i, j: (i // tiles_m, 0, 0)),
                  pl.BlockSpec((1, D), lambda i, j: (0, 0)),
                  pl.BlockSpec((None, D, tn), lambda i, j: (layer, 0, j))],
        out_specs=pl.BlockSpec((tm, tn), lambda i, j: (i, j)),
        scratch_shapes=[pltpu.VMEM((tm, D), BF16)] + [pltpu.VMEM((tm, MXU_N), F32)] * 2,
        compiler_params=_params(2),
        name="sgu_proj",
    )(*[x] * X_CHUNKS, scale, shift, norm_g, w_in)


def _sgu_core_kernel(u_ref, v_ref, z_ref, lng_ref, lnb_ref, ws_ref, bst_ref, y_ref, wm_ref, vn_ref, *, tc):
    t_out = lax.broadcasted_iota(jnp.int32, (SGU_CHUNK, SGU_CHUNK), 0)
    s_in = lax.broadcasted_iota(jnp.int32, (SGU_CHUNK, SGU_CHUNK), 1)
    causal = s_in <= t_out
    for g in range(SGU_GROUPS):
        wm_ref[g] = jnp.where(causal, ws_ref[g], 0.0).astype(BF16)
    gw = SGU_WIDTH // SGU_GROUPS

    def chunk(ci, carry):
        rows = pl.ds(pl.multiple_of(ci * SGU_CHUNK, SGU_CHUNK), SGU_CHUNK)
        v = v_ref[rows, :].astype(F32)
        mu = jnp.mean(v, axis=-1, keepdims=True)
        vc = v - mu
        var = jnp.mean(vc * vc, axis=-1, keepdims=True)
        vn_ref[...] = (vc * lax.rsqrt(var + NORM_EPS) * lng_ref[...] + lnb_ref[...]).astype(BF16)
        for g in range(SGU_GROUPS):
            cols = slice(g * gw, (g + 1) * gw)
            sv = jnp.dot(wm_ref[g], vn_ref[:, cols], preferred_element_type=F32) + bst_ref[:, g:g + 1]
            z = z_ref[rows, cols].astype(F32)
            y_ref[rows, cols] = (u_ref[rows, cols].astype(F32) * sv * _silu(z)).astype(BF16)
        return carry

    lax.fori_loop(0, tc // SGU_CHUNK, chunk, 0)


def _sgu_core(proj, ln_g, ln_b, ws, bs, *, tc=SGU_TC):
    M = proj.shape[0]
    E = SGU_WIDTH
    return pl.pallas_call(
        functools.partial(_sgu_core_kernel, tc=tc),
        out_shape=jax.ShapeDtypeStruct((M, E), BF16),
        grid=(M // tc,),
        in_specs=[pl.BlockSpec((tc, E), lambda i: (i, 0)),
                  pl.BlockSpec((tc, E), lambda i: (i, 1)),
                  pl.BlockSpec((tc, E), lambda i: (i, 2)),
                  pl.BlockSpec((1, E), lambda i: (0, 0)),
                  pl.BlockSpec((1, E), lambda i: (0, 0)),
                  pl.BlockSpec((SGU_GROUPS, SGU_CHUNK, SGU_CHUNK), lambda i: (0, 0, 0)),
                  pl.BlockSpec((SGU_CHUNK, SGU_GROUPS), lambda i: (0, 0))],
        out_specs=pl.BlockSpec((tc, E), lambda i: (i, 0)),
        scratch_shapes=[pltpu.VMEM((SGU_GROUPS, SGU_CHUNK, SGU_CHUNK), BF16),
                        pltpu.VMEM((SGU_CHUNK, E), BF16)],
        compiler_params=_params(1),
        name="sgu_core",
    )(proj, proj, proj, ln_g.reshape(1, E), ln_b.reshape(1, E), ws, bs.T)


def _conv_mix_kernel(*refs, tm, tn, n_glu, tiles_per_seq):
    x_refs, refs = refs[:X_CHUNKS], refs[X_CHUNKS:]
    (sc_ref, sh_ref, g_ref, wa_ref, wb_ref, dw_ref, db_ref, off_ref, lng_ref, lnb_ref,
     y_ref, h_ref, acc_a, acc_b, gate_a, gate_b, ext_ref, halo_ref, cv_ref, s1_ref, s2_ref) = refs
    i, j = pl.program_id(0), pl.program_id(1)
    per_sub = MXU_N // LANES
    per_step = tn // LANES
    accs, gates = (acc_a, acc_b), (gate_a, gate_b)
    first_of_seq = (i % tiles_per_seq) == 0
    lead = HALO - (CONV_K - 1)
    rc = CONV_ROWS

    @pl.when(j == 0)
    def _():
        _norm_prologue(x_refs, sc_ref, sh_ref, g_ref, h_ref, None, d=1, tm=tm)
        s1_ref[...] = jnp.zeros_like(s1_ref)
        s2_ref[...] = jnp.zeros_like(s2_ref)

    def matmul(c, slot, glu):
        cols = slice(c * MXU_N, (c + 1) * MXU_N)
        accs[slot][...] = jnp.dot(h_ref[...], wa_ref[:, cols], preferred_element_type=F32)
        if glu:
            gates[slot][...] = jnp.dot(h_ref[...], wb_ref[:, cols], preferred_element_type=F32)

    def finish_glu(c, slot):
        for s in range(per_sub):
            lanes = slice(s * LANES, (s + 1) * LANES)
            slab = j * per_step + c * per_sub + s
            stage = (c % 2) * per_sub + s
            ext_ref[stage, 0:HALO, :] = jnp.where(first_of_seq, 0.0, halo_ref[slab])
            for r0 in range(0, tm, rc):
                ext_ref[stage, HALO + r0:HALO + r0 + rc, :] = (
                    accs[slot][r0:r0 + rc, lanes] * _sigmoid(gates[slot][r0:r0 + rc, lanes]))
            halo_ref[slab] = ext_ref[stage, tm:tm + HALO, :]
            for r0 in range(0, tm, rc):
                acc = jnp.broadcast_to(db_ref[slab], (rc, LANES))
                for k in range(CONV_K):
                    acc = acc + dw_ref[slab, k:k + 1, :] * ext_ref[stage, r0 + lead + k:r0 + lead + k + rc, :]
                cv_ref[slab, r0:r0 + rc, :] = acc.astype(BF16)
                dev = acc - off_ref[...]
                s1_ref[r0:r0 + rc, :] += dev
                s2_ref[r0:r0 + rc, :] += dev * dev

    def finish_z(c, slot):
        mu, inv = s1_ref[...], s2_ref[...]
        for s in range(per_sub):
            lanes = slice(s * LANES, (s + 1) * LANES)
            slab = (j - n_glu) * per_step + c * per_sub + s
            t = (cv_ref[slab].astype(F32) - mu) * inv * lng_ref[slab] + lnb_ref[slab]
            y_ref[:, c * MXU_N + s * LANES:c * MXU_N + (s + 1) * LANES] = (
                _silu(t) * _silu(accs[slot][:, lanes])).astype(BF16)

    @pl.when(j < n_glu)
    def _():
        _pipelined_subtiles(tn // MXU_N, functools.partial(matmul, glu=True), finish_glu)

    @pl.when(j == n_glu)
    def _():
        inv_e = 1.0 / CONV_WIDTH
        mu = jnp.sum(s1_ref[...], axis=-1, keepdims=True) * inv_e
        var = jnp.maximum(jnp.sum(s2_ref[...], axis=-1, keepdims=True) * inv_e - mu * mu, 0.0)
        s1_ref[...] = jnp.broadcast_to(mu, s1_ref.shape) + off_ref[...]
        s2_ref[...] = jnp.broadcast_to(lax.rsqrt(var + NORM_EPS), s2_ref.shape)

    @pl.when(j >= n_glu)
    def _():
        _pipelined_subtiles(tn // MXU_N, functools.partial(matmul, glu=False), finish_z)


def _conv_mix(x, scale, shift, norm_g, w_bf16, dw_w, dw_b, ln_g, ln_b, *, tm=CONV_TM, tn=PROJ_TN):
    B, S, D = x.shape
    E = CONV_WIDTH
    nj = E // tn
    n_slabs = E // LANES
    tiles_m = S // tm
    w_slabs = dw_w.reshape(CONV_K, n_slabs, LANES).transpose(1, 0, 2)

    def slabs(v):
        return v.reshape(n_slabs, 1, LANES)

    def whole(shape):
        return pl.BlockSpec(shape, lambda i, j: (0,) * len(shape))

    return pl.pallas_call(
        functools.partial(_conv_mix_kernel, tm=tm, tn=tn, n_glu=nj, tiles_per_seq=tiles_m),
        out_shape=jax.ShapeDtypeStruct((B * S, E), BF16),
        grid=(B * tiles_m, 2 * nj),
        in_specs=_x_specs(tiles_m, B * tiles_m, tm, 2 * nj) + [
                  pl.BlockSpec((1, 1, D), lambda i, j: (i // tiles_m, 0, 0)),
                  pl.BlockSpec((1, 1, D), lambda i, j: (i // tiles_m, 0, 0)),
                  pl.BlockSpec((1, D), lambda i, j: (0, 0)),
                  pl.BlockSpec((D, tn), lambda i, j: (0, jnp.where(j < nj, j, nj + j))),
                  pl.BlockSpec((D, tn), lambda i, j: (0, jnp.minimum(nj + j, 2 * nj - 1))),
                  whole((n_slabs, CONV_K, LANES)), whole((n_slabs, 1, LANES)), whole((1, LANES)),
                  whole((n_slabs, 1, LANES)), whole((n_slabs, 1, LANES))],
        out_specs=pl.BlockSpec((tm, tn), lambda i, j: (i, jnp.maximum(j - nj, 0))),
        scratch_shapes=[pltpu.VMEM((tm, D), BF16)] + [pltpu.VMEM((tm, MXU_N), F32)] * 4
        + [pltpu.VMEM((2 * MXU_N // LANES, tm + HALO, LANES), F32),
           pltpu.VMEM((n_slabs, HALO, LANES), F32),
           pltpu.VMEM((n_slabs, tm, LANES), BF16),
           pltpu.VMEM((tm, LANES), F32), pltpu.VMEM((tm, LANES), F32)],
        compiler_params=_params(2),
        name="conv_mix",
    )(*[x] * X_CHUNKS, scale, shift, norm_g, w_bf16, w_bf16, w_slabs, slabs(dw_b),
      jnp.full((1, LANES), jnp.mean(dw_b), F32), slabs(ln_g), slabs(ln_b))


def _rotate_half_matrix():
    half = ROPE_DIM // 2
    k = jnp.arange(HEAD_DIM)[:, None]
    j = jnp.arange(HEAD_DIM)[None, :]
    p = (jnp.where((j < half) & (k == j + half), -1.0, 0.0)
         + jnp.where((j >= half) & (j < ROPE_DIM) & (k == j - half), 1.0, 0.0))
    return jnp.kron(jnp.eye(MXU_N // HEAD_DIM, dtype=F32), p).astype(BF16)


def _rope_tables(positions):
    B, S = positions.shape
    inv_freq = jnp.power(ROPE_THETA, -jnp.arange(0, ROPE_DIM, 2, dtype=F32) / ROPE_DIM)
    ang = positions.astype(F32)[..., None] * inv_freq
    cos, sin = lax.optimization_barrier((jnp.cos(ang), jnp.sin(ang)))
    rest = HEAD_DIM - ROPE_DIM
    cos_t = jnp.concatenate([cos, cos, jnp.ones((B, S, rest), F32)], axis=-1)
    sin_t = jnp.concatenate([sin, sin, jnp.zeros((B, S, rest), F32)], axis=-1)
    return [tuple(t.reshape(B, S // d, d, HEAD_DIM).transpose(0, 2, 1, 3) for t in (cos_t, sin_t))
            for _, d in DIL_GROUPS]


def _rope_gains(q_gain, k_gain):
    half = ROPE_DIM // 2
    gains = jnp.stack([q_gain, k_gain])
    lane = jnp.arange(HEAD_DIM)
    partner = jnp.where(lane < half, jnp.roll(gains, -half, axis=-1), jnp.roll(gains, half, axis=-1))
    return jnp.stack([gains, partner], axis=1)


def kernel(x, c, positions, ada_w, ada_b, norm_g, attn_w_in, attn_q_gain, attn_k_gain, attn_w_out,
           sgu_w_in, sgu_ln_g, sgu_ln_b, sgu_ws, sgu_bs, sgu_w_out,
           conv_w_in, conv_dw_w, conv_dw_b, conv_ln_g, conv_ln_b, conv_w_out):
    B, S, D = x.shape
    mod = _ada_modulation(c, ada_w, ada_b)
    tables = _rope_tables(positions)
    for i in range(DEPTH):
        shift = mod[i, :, None, :D]
        scale = mod[i, :, None, D:2 * D]
        gate = mod[i, :, None, 2 * D:]
        g_row = norm_g[i].reshape(1, D)
        kind, l = i % N_MIXERS, i // N_MIXERS
        if kind == 0:
            gains = _rope_gains(attn_q_gain[l] * (LOG2_E * HEAD_DIM ** -0.5), attn_k_gain[l])
            proj = [_attn_proj(x, scale, shift, g_row, attn_w_in, l, gains, tables[g], group=g,
                               with_gate=(g == 0)) for g in range(N_DIL)]
            y = _attention(*proj).reshape(B * S, ATTN_OUT)
            x = _out_proj(y, attn_w_out[l].astype(BF16), x, gate)
        elif kind == 1:
            proj = _sgu_proj(x, scale, shift, g_row, sgu_w_in, l)
            y = _sgu_core(proj, sgu_ln_g[l], sgu_ln_b[l], sgu_ws[l], sgu_bs[l])
            x = _out_proj(y, sgu_w_out[l].astype(BF16), x, gate)
        else:
            y = _conv_mix(x, scale, shift, g_row, conv_w_in[l].astype(BF16), conv_dw_w[l], conv_dw_b[l],
                          conv_ln_g[l], conv_ln_b[l])
            x = _out_proj(y, conv_w_out[l].astype(BF16), x, gate)
    return x
```

```python
import functools

import jax
import jax.numpy as jnp
from jax import lax
from jax.experimental import pallas as pl
from jax.experimental.pallas import tpu as pltpu

D_MODEL = 2048
DEPTH = 4
N_MIXERS = 3
HEAD_DIM = 128
DIL_GROUPS = ((128, 1), (512, 4), (2048, 16))
N_DIL = len(DIL_GROUPS)
HEADS_PER_GROUP = D_MODEL // HEAD_DIM
ATTN_OUT = HEADS_PER_GROUP * HEAD_DIM
ATTN_QKV = N_DIL * ATTN_OUT
ROPE_DIM = HEAD_DIM // 4
ROPE_THETA = 500000.0
BLOCK = 128
SGU_WIDTH = 2 * D_MODEL
SGU_CHUNK = 128
SGU_GROUPS = 16
CONV_WIDTH = 2 * D_MODEL
CONV_K = 31
NORM_EPS = 1e-6

LANES = 128
MXU_N = 256
SUB = 4
HALO = 32
VMEM_LIMIT = 56 * 1024 * 1024
MASKED = -1e30
LOG2_E = 1.4426950408889634

PROJ_TM, PROJ_TN = 1024, 1024
CONV_TM = 512
OUT_TM = 256
SGU_TC = 512
ADA_ROWS, ADA_TN = 16, 1024
NORM_ROWS = 128
ROPE_ROWS = 32
CONV_ROWS = 64
COMBINE_ROWS = 256
X_CHUNKS = 4

F32 = jnp.float32
BF16 = jnp.bfloat16


def _params(n_axes):
    return pltpu.CompilerParams(dimension_semantics=("arbitrary",) * n_axes,
                                vmem_limit_bytes=VMEM_LIMIT)


def _sigmoid(x):
    return 1.0 / (1.0 + jnp.exp(-x))


def _silu(x):
    return x * _sigmoid(x)


def _gelu_tanh(x):
    return x * (0.5 * (1.0 + jnp.tanh(0.7978845608028654 * (x + 0.044715 * (x * x * x)))))


def _x_specs(tiles_m, n_row_tiles, tm, n_col_steps):
    assert n_col_steps > X_CHUNKS

    def spec(c):
        def index(i, j):
            t = jnp.minimum(i + jnp.where(j > c, 1, 0), n_row_tiles - 1)
            return (t // tiles_m, t % tiles_m, c)
        return pl.BlockSpec((1, tm, D_MODEL // X_CHUNKS), index)

    return [spec(c) for c in range(X_CHUNKS)]


def _pipelined_subtiles(n_sub, matmul, finish):
    for c in range(n_sub + 1):
        if c < n_sub:
            matmul(c, c % 2)
        if c >= 1:
            finish(c - 1, (c - 1) % 2)


def _split_bf16(v):
    hi = v.astype(BF16)
    return hi, (v - hi.astype(F32)).astype(BF16)


def _ada_kernel(c_ref, w_ref, b_ref, o_ref):
    c_hi, c_lo = _split_bf16(_silu(c_ref[...]))
    w_hi, w_lo = _split_bf16(w_ref[0])
    rows = c_hi.shape[0]
    both = jnp.dot(jnp.concatenate([c_hi, c_lo], axis=0), w_hi, preferred_element_type=F32)
    o_ref[0] = (both[:rows] + both[rows:] + jnp.dot(c_hi, w_lo, preferred_element_type=F32)
                + b_ref[0])


def _ada_modulation(c, ada_w, ada_b):
    B, D = c.shape
    rows, tn = ADA_ROWS, ADA_TN
    c8 = jnp.pad(c, ((0, rows - B), (0, 0)))
    out = pl.pallas_call(
        _ada_kernel,
        out_shape=jax.ShapeDtypeStruct((DEPTH, rows, 3 * D), F32),
        grid=(DEPTH, 3 * D // tn),
        in_specs=[pl.BlockSpec((rows, D), lambda l, j: (0, 0)),
                  pl.BlockSpec((1, D, tn), lambda l, j: (l, 0, j)),
                  pl.BlockSpec((1, 1, tn), lambda l, j: (l, 0, j))],
        out_specs=pl.BlockSpec((1, rows, tn), lambda l, j: (l, 0, j)),
        compiler_params=_params(2),
        name="ada_modulation",
    )(c8, ada_w, ada_b.reshape(DEPTH, 1, 3 * D))
    return out[:, :B]


def _norm_prologue(x_refs, sc_ref, sh_ref, g_ref, h_ref, work_ref, *, d, tm):
    rc = NORM_ROWS
    a = g_ref[...] * (1.0 + sc_ref[0])
    sh = sh_ref[0]

    def chunk_rows(i):
        return pl.ds(pl.multiple_of(i * rc, rc), rc)

    def x_rows(rows):
        return jnp.concatenate([x_ref[0, rows, :] for x_ref in x_refs], axis=1)

    def x_slab(c):
        per_chunk = D_MODEL // X_CHUNKS // LANES
        lane0 = (c % per_chunk) * LANES
        return x_refs[c // per_chunk][0, :, lane0:lane0 + LANES]

    def inv_rms(x):
        return lax.rsqrt(jnp.mean(x * x, axis=-1, keepdims=True) + NORM_EPS)

    if d == 1:
        def body(i, carry):
            x = x_rows(chunk_rows(i))
            h_ref[chunk_rows(i), :] = (x * inv_rms(x) * a + sh).astype(BF16)
            return carry

        lax.fori_loop(0, tm // rc, body, 0)
        return

    INV, NAT, TMP = 0, 1, 2

    def stats(i, carry):
        work_ref[INV, chunk_rows(i), :] = jnp.broadcast_to(inv_rms(x_rows(chunk_rows(i))), (rc, LANES))
        return carry

    lax.fori_loop(0, tm // rc, stats, 0)
    rows = tm // d
    for c in range(D_MODEL // LANES):
        cols = slice(c * LANES, (c + 1) * LANES)
        work_ref[NAT] = x_slab(c) * work_ref[INV] * a[:, cols] + sh[:, cols]
        if d <= SUB:
            for r in range(d):
                h_ref[r * rows:(r + 1) * rows, cols] = work_ref[NAT, pl.ds(r, rows, stride=d), :].astype(BF16)
            continue
        part = tm // SUB
        for r_lo in range(SUB):
            work_ref[TMP, r_lo * part:(r_lo + 1) * part, :] = work_ref[NAT, pl.ds(r_lo, part, stride=SUB), :]
        for r in range(d):
            src = pl.ds((r % SUB) * part + r // SUB, rows, stride=d // SUB)
            h_ref[r * rows:(r + 1) * rows, cols] = work_ref[TMP, src, :].astype(BF16)


def _attn_proj_kernel(*refs, d, tm, tn, n_rope):
    x_refs, refs = refs[:X_CHUNKS], refs[X_CHUNKS:]
    (sc_ref, sh_ref, g_ref, w_ref, perm_ref, gain_ref, cos_ref, sin_ref,
     o_ref, h_ref, acc_a, acc_b, part_ref, *work) = refs
    j = pl.program_id(1)
    rows = tm // d
    rc = min(ROPE_ROWS, rows)
    per_res = rows // rc
    bufs = (acc_a, acc_b)

    @pl.when(j == 0)
    def _():
        _norm_prologue(x_refs, sc_ref, sh_ref, g_ref, h_ref, work[0] if work else None, d=d, tm=tm)

    def matmul(c, slot):
        bufs[slot][...] = jnp.dot(h_ref[...], w_ref[:, c * MXU_N:(c + 1) * MXU_N].astype(BF16),
                                  preferred_element_type=F32)

    def finish(c, slot, rope):
        acc_ref, c0 = bufs[slot], c * MXU_N
        if rope:
            part_ref[...] = jnp.dot(acc_ref[...].astype(BF16), perm_ref[...], preferred_element_type=F32)
            gains = gain_ref[j // (n_rope // 2)]
        for ci in range(tm // rc):
            r, m0, p0 = ci // per_res, (ci % per_res) * rc, ci * rc
            if not rope:
                o_ref[0, r, m0:m0 + rc, c0:c0 + MXU_N] = acc_ref[p0:p0 + rc, :].astype(BF16)
                continue
            cos = cos_ref[0, r, m0:m0 + rc, :] * gains[0:1]
            sin = sin_ref[0, r, m0:m0 + rc, :] * gains[1:2]
            for hc in range(MXU_N // LANES):
                cols = slice(hc * LANES, (hc + 1) * LANES)
                a = acc_ref[p0:p0 + rc, cols]
                ms = jnp.sum(a * a, axis=-1, keepdims=True) * (1.0 / HEAD_DIM)
                rot = (a * cos + part_ref[p0:p0 + rc, cols] * sin) * lax.rsqrt(ms + NORM_EPS)
                o_ref[0, r, m0:m0 + rc, c0 + hc * LANES:c0 + (hc + 1) * LANES] = rot.astype(BF16)

    @pl.when(j < n_rope)
    def _():
        _pipelined_subtiles(tn // MXU_N, matmul, functools.partial(finish, rope=True))

    @pl.when(j >= n_rope)
    def _():
        _pipelined_subtiles(tn // MXU_N, matmul, functools.partial(finish, rope=False))


def _attn_proj(x, scale, shift, norm_g, w_in, layer, gains, tables, *, group, with_gate, tm=PROJ_TM, tn=PROJ_TN):
    B, S, D = x.shape
    d = DIL_GROUPS[group][1]
    n_sec = 4 if with_gate else 3
    per_sec = ATTN_OUT // tn
    tiles_m = S // tm

    def w_index(i, j):
        sec = j // per_sec
        base = jnp.where(sec < 3, sec * (ATTN_QKV // tn) + group * per_sec, 3 * (ATTN_QKV // tn))
        return (layer, 0, base + j % per_sec)

    tab_spec = pl.BlockSpec((1, d, tm // d, LANES), lambda i, j: (i // tiles_m, 0, i % tiles_m, 0))
    scratch = [pltpu.VMEM((tm, D), BF16)] + [pltpu.VMEM((tm, MXU_N), F32)] * 3
    if d > 1:
        scratch.append(pltpu.VMEM((3, tm, LANES), F32))
    return pl.pallas_call(
        functools.partial(_attn_proj_kernel, d=d, tm=tm, tn=tn, n_rope=2 * per_sec),
        out_shape=jax.ShapeDtypeStruct((B, d, S // d, n_sec * ATTN_OUT), BF16),
        grid=(B * tiles_m, n_sec * per_sec),
        in_specs=_x_specs(tiles_m, B * tiles_m, tm, n_sec * per_sec) + [
                  pl.BlockSpec((1, 1, D), lambda i, j: (i // tiles_m, 0, 0)),
                  pl.BlockSpec((1, 1, D), lambda i, j: (i // tiles_m, 0, 0)),
                  pl.BlockSpec((1, D), lambda i, j: (0, 0)),
                  pl.BlockSpec((None, D, tn), w_index),
                  pl.BlockSpec((MXU_N, MXU_N), lambda i, j: (0, 0)),
                  pl.BlockSpec((2, 2, LANES), lambda i, j: (0, 0, 0)),
                  tab_spec, tab_spec],
        out_specs=pl.BlockSpec((1, d, tm // d, tn), lambda i, j: (i // tiles_m, 0, i % tiles_m, j)),
        scratch_shapes=scratch,
        compiler_params=_params(2),
        name=f"attn_proj_g{group}",
    )(*[x] * X_CHUNKS, scale, shift, norm_g, w_in, _rotate_half_matrix(), gains, *tables)


def _attn_block(q, k, v, bias):
    s = lax.dot_general(q, k, (((1,), (1,)), ((), ())), preferred_element_type=F32) + bias
    m = jnp.max(s, axis=-1, keepdims=True)
    p = jnp.exp2(s - m).astype(BF16)
    v_ext = jnp.concatenate([v, jnp.ones(v.shape, BF16)], axis=1)
    ol = jnp.dot(p, v_ext, preferred_element_type=F32)
    l = ol[:, HEAD_DIM:]
    return ol[:, :HEAD_DIM] * (1.0 / l), m + jnp.log2(l)


def _attn_kernel(q0, k0, v0, z0, q1, k1, v1, q2, k2, v2, y_ref, o_scr, lse_scr, bias_ref, *, seq):
    @pl.when((pl.program_id(0) == 0) & (pl.program_id(1) == 0))
    def _():
        dist = (lax.broadcasted_iota(jnp.int32, (BLOCK, 2 * BLOCK), 0) + BLOCK
                - lax.broadcasted_iota(jnp.int32, (BLOCK, 2 * BLOCK), 1))
        bias_ref[0:BLOCK, :] = jnp.where(dist >= 0, jnp.where(dist <= BLOCK, 0.0, MASKED), MASKED).astype(F32)
        row = lax.broadcasted_iota(jnp.int32, (2 * BLOCK, 2 * BLOCK), 0)
        col = lax.broadcasted_iota(jnp.int32, (2 * BLOCK, 2 * BLOCK), 1)
        same_seq = (row // BLOCK) == (col // BLOCK)
        bias_ref[BLOCK:, :] = jnp.where(same_seq, jnp.where(col <= row, 0.0, MASKED), MASKED).astype(F32)

    bias_both = bias_ref[0:BLOCK, :]
    bias_first = bias_ref[0:BLOCK, BLOCK:]
    bias_pair = bias_ref[BLOCK:, :]

    groups = ((q0, k0, v0), (q1, k1, v1), (q2, k2, v2))
    for g, (q_ref, k_ref, v_ref) in enumerate(groups):
        d = DIL_GROUPS[g][1]
        nb = seq // d // BLOCK

        def store(r, n, o, lse, g=g, d=d):
            if d == 1:
                rows = pl.ds(BLOCK * n, BLOCK)
            elif d <= SUB:
                rows = pl.ds(r + d * BLOCK * n, BLOCK, stride=d)
            else:
                rows = pl.ds((r % SUB) * (seq // SUB) + r // SUB + (d // SUB) * BLOCK * n, BLOCK,
                             stride=d // SUB)
            slot = g if d <= SUB else N_DIL
            o_scr[slot, rows, :] = o
            lse_scr[slot, rows, :] = lse

        def first_block(r, q_ref=q_ref, k_ref=k_ref, v_ref=v_ref, store=store):
            o, lse = _attn_block(q_ref[0, r, 0:BLOCK, :], k_ref[0, r, 0:BLOCK, :],
                                 v_ref[0, r, 0:BLOCK, :], bias_first)
            store(r, 0, o, lse)

        def later_block(r, n, q_ref=q_ref, k_ref=k_ref, v_ref=v_ref, store=store):
            q_rows, kv_rows = pl.ds(n * BLOCK, BLOCK), pl.ds((n - 1) * BLOCK, 2 * BLOCK)
            o, lse = _attn_block(q_ref[0, r, q_rows, :], k_ref[0, r, kv_rows, :], v_ref[0, r, kv_rows, :],
                                 bias_both)
            store(r, n, o, lse)

        def block_pair(i, q_ref=q_ref, k_ref=k_ref, v_ref=v_ref, store=store):
            def both(ref):
                return jnp.concatenate([ref[0, 2 * i], ref[0, 2 * i + 1]], axis=0)

            o, lse = _attn_block(both(q_ref), both(k_ref), both(v_ref), bias_pair)
            store(2 * i, 0, o[:BLOCK], lse[:BLOCK])
            store(2 * i + 1, 0, o[BLOCK:], lse[BLOCK:])

        if nb > 1:
            for r in range(d):
                first_block(r)
                for n in range(1, nb):
                    later_block(r, n)
        else:
            for i in range(d // 2):
                block_pair(i)

        if d > SUB:
            part = seq // SUB
            for r_lo in range(SUB):
                for scr in (o_scr, lse_scr):
                    scr[g, pl.ds(r_lo, part, stride=SUB), :] = scr[N_DIL, r_lo * part:(r_lo + 1) * part, :]

    rc = COMBINE_ROWS

    for r0 in range(0, seq, rc):
        rows = slice(r0, r0 + rc)
        l0, l1, l2 = lse_scr[0, rows, :], lse_scr[1, rows, :], lse_scr[2, rows, :]
        m = jnp.maximum(jnp.maximum(l0, l1), l2)
        e0, e1, e2 = jnp.exp2(l0 - m), jnp.exp2(l1 - m), jnp.exp2(l2 - m)
        o = (e0 * o_scr[0, rows, :] + e1 * o_scr[1, rows, :] + e2 * o_scr[2, rows, :]) / (e0 + e1 + e2)
        z = z0[0, 0, rows, :].astype(F32)
        y_ref[0, rows, :] = (o * _silu(z)).astype(BF16)


def _attention(p0, p1, p2):
    B, _, S, _ = p0.shape
    H = HEADS_PER_GROUP

    def spec(d, sec):
        return pl.BlockSpec((1, d, S // d, LANES), lambda b, h: (b, 0, 0, sec * H + h))

    in_specs = [spec(1, 0), spec(1, 1), spec(1, 2), spec(1, 3)]
    for d in (DIL_GROUPS[1][1], DIL_GROUPS[2][1]):
        in_specs += [spec(d, 0), spec(d, 1), spec(d, 2)]
    return pl.pallas_call(
        functools.partial(_attn_kernel, seq=S),
        out_shape=jax.ShapeDtypeStruct((B, S, ATTN_OUT), BF16),
        grid=(B, H),
        in_specs=in_specs,
        out_specs=pl.BlockSpec((1, S, LANES), lambda b, h: (b, 0, h)),
        scratch_shapes=[pltpu.VMEM((N_DIL + 1, S, LANES), F32)] * 2 + [pltpu.VMEM((3 * BLOCK, 2 * BLOCK), F32)],
        compiler_params=_params(2),
        name="dilated_attention",
    )(p0, p0, p0, p0, p1, p1, p1, p2, p2, p2)


def _out_proj_kernel(y_ref, w_ref, x_ref, gate_ref, o_ref, acc_a, acc_b):
    bufs = (acc_a, acc_b)

    def matmul(c, slot):
        bufs[slot][...] = jnp.dot(y_ref[...], w_ref[:, c * MXU_N:(c + 1) * MXU_N],
                                  preferred_element_type=F32)

    def finish(c, slot):
        cols = slice(c * MXU_N, (c + 1) * MXU_N)
        o_ref[:, cols] = x_ref[:, cols] + gate_ref[0][:, cols] * bufs[slot][...]

    _pipelined_subtiles(o_ref.shape[1] // MXU_N, matmul, finish)


def _out_proj(y, w_bf16, x, gate, *, tm=OUT_TM):
    B, S, D = x.shape
    M, K = y.shape
    tiles_m = S // tm
    out = pl.pallas_call(
        _out_proj_kernel,
        out_shape=jax.ShapeDtypeStruct((M, D), F32),
        grid=(M // tm,),
        in_specs=[pl.BlockSpec((tm, K), lambda i: (i, 0)),
                  pl.BlockSpec((K, D), lambda i: (0, 0), pipeline_mode=pl.Buffered(1)),
                  pl.BlockSpec((tm, D), lambda i: (i, 0)),
                  pl.BlockSpec((1, 1, D), lambda i: (i // tiles_m, 0, 0))],
        out_specs=pl.BlockSpec((tm, D), lambda i: (i, 0)),
        scratch_shapes=[pltpu.VMEM((tm, MXU_N), F32)] * 2,
        compiler_params=_params(1),
        name="out_proj",
    )(y, w_bf16, x.reshape(M, D), gate)
    return out.reshape(B, S, D)


def _sgu_proj_kernel(*refs, tm, tn, n_gelu):
    x_refs, (sc_ref, sh_ref, g_ref, w_ref, o_ref, h_ref, acc_a, acc_b) = refs[:X_CHUNKS], refs[X_CHUNKS:]
    j = pl.program_id(1)
    bufs = (acc_a, acc_b)

    @pl.when(j == 0)
    def _():
        _norm_prologue(x_refs, sc_ref, sh_ref, g_ref, h_ref, None, d=1, tm=tm)

    def matmul(c, slot):
        bufs[slot][...] = jnp.dot(h_ref[...], w_ref[:, c * MXU_N:(c + 1) * MXU_N].astype(BF16),
                                  preferred_element_type=F32)

    def finish(c, slot, gelu):
        a = bufs[slot][...]
        o_ref[:, c * MXU_N:(c + 1) * MXU_N] = (_gelu_tanh(a) if gelu else a).astype(BF16)

    @pl.when(j < n_gelu)
    def _():
        _pipelined_subtiles(tn // MXU_N, matmul, functools.partial(finish, gelu=True))

    @pl.when(j >= n_gelu)
    def _():
        _pipelined_subtiles(tn // MXU_N, matmul, functools.partial(finish, gelu=False))


def _sgu_proj(x, scale, shift, norm_g, w_in, layer, *, tm=PROJ_TM, tn=PROJ_TN):
    B, S, D = x.shape
    N = w_in.shape[2]
    tiles_m = S // tm
    return pl.pallas_call(
        functools.partial(_sgu_proj_kernel, tm=tm, tn=tn, n_gelu=2 * SGU_WIDTH // tn),
        out_shape=jax.ShapeDtypeStruct((B * S, N), BF16),
        grid=(B * tiles_m, N // tn),
        in_specs=_x_specs(tiles_m, B * tiles_m, tm, N // tn) + [
                  pl.BlockSpec((1, 1, D), lambda i, j: (i // tiles_m, 0, 0)),
                  pl.BlockSpec((1, 1, D), lambda i, j: (i // tiles_m, 0, 0)),
                  pl.BlockSpec((1, D), lambda i, j: (0, 0)),
                  pl.BlockSpec((None, D, tn), lambda i, j: (layer, 0, j))],
        out_specs=pl.BlockSpec((tm, tn), lambda i, j: (i, j)),
        scratch_shapes=[pltpu.VMEM((tm, D), BF16)] + [pltpu.VMEM((tm, MXU_N), F32)] * 2,
        compiler_params=_params(2),
        name="sgu_proj",
    )(*[x] * X_CHUNKS, scale, shift, norm_g, w_in)


def _sgu_core_kernel(u_ref, v_ref, z_ref, lng_ref, lnb_ref, ws_ref, bst_ref, y_ref, wm_ref, vn_ref, *, tc):
    t_out = lax.broadcasted_iota(jnp.int32, (SGU_CHUNK, SGU_CHUNK), 0)
    s_in = lax.broadcasted_iota(jnp.int32, (SGU_CHUNK, SGU_CHUNK), 1)
    causal = s_in <= t_out
    for g in range(SGU_GROUPS):
        wm_ref[g] = jnp.where(causal, ws_ref[g], 0.0).astype(BF16)
    gw = SGU_WIDTH // SGU_GROUPS

    def chunk(ci, carry):
        rows = pl.ds(pl.multiple_of(ci * SGU_CHUNK, SGU_CHUNK), SGU_CHUNK)
        v = v_ref[rows, :].astype(F32)
        mu = jnp.mean(v, axis=-1, keepdims=True)
        vc = v - mu
        var = jnp.mean(vc * vc, axis=-1, keepdims=True)
        vn_ref[...] = (vc * lax.rsqrt(var + NORM_EPS) * lng_ref[...] + lnb_ref[...]).astype(BF16)
        for g in range(SGU_GROUPS):
            cols = slice(g * gw, (g + 1) * gw)
            sv = jnp.dot(wm_ref[g], vn_ref[:, cols], preferred_element_type=F32) + bst_ref[:, g:g + 1]
            z = z_ref[rows, cols].astype(F32)
            y_ref[rows, cols] = (u_ref[rows, cols].astype(F32) * sv * _silu(z)).astype(BF16)
        return carry

    lax.fori_loop(0, tc // SGU_CHUNK, chunk, 0)


def _sgu_core(proj, ln_g, ln_b, ws, bs, *, tc=SGU_TC):
    M = proj.shape[0]
    E = SGU_WIDTH
    return pl.pallas_call(
        functools.partial(_sgu_core_kernel, tc=tc),
        out_shape=jax.ShapeDtypeStruct((M, E), BF16),
        grid=(M // tc,),
        in_specs=[pl.BlockSpec((tc, E), lambda i: (i, 0)),
                  pl.BlockSpec((tc, E), lambda i: (i, 1)),
                  pl.BlockSpec((tc, E), lambda i: (i, 2)),
                  pl.BlockSpec((1, E), lambda i: (0, 0)),
                  pl.BlockSpec((1, E), lambda i: (0, 0)),
                  pl.BlockSpec((SGU_GROUPS, SGU_CHUNK, SGU_CHUNK), lambda i: (0, 0, 0)),
                  pl.BlockSpec((SGU_CHUNK, SGU_GROUPS), lambda i: (0, 0))],
        out_specs=pl.BlockSpec((tc, E), lambda i: (i, 0)),
        scratch_shapes=[pltpu.VMEM((SGU_GROUPS, SGU_CHUNK, SGU_CHUNK), BF16),
                        pltpu.VMEM((SGU_CHUNK, E), BF16)],
        compiler_params=_params(1),
        name="sgu_core",
    )(proj, proj, proj, ln_g.reshape(1, E), ln_b.reshape(1, E), ws, bs.T)


def _conv_mix_kernel(*refs, tm, tn, n_glu, tiles_per_seq):
    x_refs, refs = refs[:X_CHUNKS], refs[X_CHUNKS:]
    (sc_ref, sh_ref, g_ref, wa_ref, wb_ref, dw_ref, db_ref, off_ref, lng_ref, lnb_ref,
     y_ref, h_ref, acc_a, acc_b, gate_a, gate_b, ext_ref, halo_ref, cv_ref, s1_ref, s2_ref) = refs
    i, j = pl.program_id(0), pl.program_id(1)
    per_sub = MXU_N // LANES
    per_step = tn // LANES
    accs, gates = (acc_a, acc_b), (gate_a, gate_b)
    first_of_seq = (i % tiles_per_seq) == 0
    lead = HALO - (CONV_K - 1)
    rc = CONV_ROWS

    @pl.when(j == 0)
    def _():
        _norm_prologue(x_refs, sc_ref, sh_ref, g_ref, h_ref, None, d=1, tm=tm)
        s1_ref[...] = jnp.zeros_like(s1_ref)
        s2_ref[...] = jnp.zeros_like(s2_ref)

    def matmul(c, slot, glu):
        cols = slice(c * MXU_N, (c + 1) * MXU_N)
        accs[slot][...] = jnp.dot(h_ref[...], wa_ref[:, cols], preferred_element_type=F32)
        if glu:
            gates[slot][...] = jnp.dot(h_ref[...], wb_ref[:, cols], preferred_element_type=F32)

    def finish_glu(c, slot):
        for s in range(per_sub):
            lanes = slice(s * LANES, (s + 1) * LANES)
            slab = j * per_step + c * per_sub + s
            stage = (c % 2) * per_sub + s
            ext_ref[stage, 0:HALO, :] = jnp.where(first_of_seq, 0.0, halo_ref[slab])
            for r0 in range(0, tm, rc):
                ext_ref[stage, HALO + r0:HALO + r0 + rc, :] = (
                    accs[slot][r0:r0 + rc, lanes] * _sigmoid(gates[slot][r0:r0 + rc, lanes]))
            halo_ref[slab] = ext_ref[stage, tm:tm + HALO, :]
            for r0 in range(0, tm, rc):
                acc = jnp.broadcast_to(db_ref[slab], (rc, LANES))
                for k in range(CONV_K):
                    acc = acc + dw_ref[slab, k:k + 1, :] * ext_ref[stage, r0 + lead + k:r0 + lead + k + rc, :]
                cv_ref[slab, r0:r0 + rc, :] = acc.astype(BF16)
                dev = acc - off_ref[...]
                s1_ref[r0:r0 + rc, :] += dev
                s2_ref[r0:r0 + rc, :] += dev * dev

    def finish_z(c, slot):
        mu, inv = s1_ref[...], s2_ref[...]
        for s in range(per_sub):
            lanes = slice(s * LANES, (s + 1) * LANES)
            slab = (j - n_glu) * per_step + c * per_sub + s
            t = (cv_ref[slab].astype(F32) - mu) * inv * lng_ref[slab] + lnb_ref[slab]
            y_ref[:, c * MXU_N + s * LANES:c * MXU_N + (s + 1) * LANES] = (
                _silu(t) * _silu(accs[slot][:, lanes])).astype(BF16)

    @pl.when(j < n_glu)
    def _():
        _pipelined_subtiles(tn // MXU_N, functools.partial(matmul, glu=True), finish_glu)

    @pl.when(j == n_glu)
    def _():
        inv_e = 1.0 / CONV_WIDTH
        mu = jnp.sum(s1_ref[...], axis=-1, keepdims=True) * inv_e
        var = jnp.maximum(jnp.sum(s2_ref[...], axis=-1, keepdims=True) * inv_e - mu * mu, 0.0)
        s1_ref[...] = jnp.broadcast_to(mu, s1_ref.shape) + off_ref[...]
        s2_ref[...] = jnp.broadcast_to(lax.rsqrt(var + NORM_EPS), s2_ref.shape)

    @pl.when(j >= n_glu)
    def _():
        _pipelined_subtiles(tn // MXU_N, functools.partial(matmul, glu=False), finish_z)


def _conv_mix(x, scale, shift, norm_g, w_bf16, dw_w, dw_b, ln_g, ln_b, *, tm=CONV_TM, tn=PROJ_TN):
    B, S, D = x.shape
    E = CONV_WIDTH
    nj = E // tn
    n_slabs = E // LANES
    tiles_m = S // tm
    w_slabs = dw_w.reshape(CONV_K, n_slabs, LANES).transpose(1, 0, 2)

    def slabs(v):
        return v.reshape(n_slabs, 1, LANES)

    def whole(shape):
        return pl.BlockSpec(shape, lambda i, j: (0,) * len(shape))

    return pl.pallas_call(
        functools.partial(_conv_mix_kernel, tm=tm, tn=tn, n_glu=nj, tiles_per_seq=tiles_m),
        out_shape=jax.ShapeDtypeStruct((B * S, E), BF16),
        grid=(B * tiles_m, 2 * nj),
        in_specs=_x_specs(tiles_m, B * tiles_m, tm, 2 * nj) + [
                  pl.BlockSpec((1, 1, D), lambda i, j: (i // tiles_m, 0, 0)),
                  pl.BlockSpec((1, 1, D), lambda i, j: (i // tiles_m, 0, 0)),
                  pl.BlockSpec((1, D), lambda i, j: (0, 0)),
                  pl.BlockSpec((D, tn), lambda i, j: (0, jnp.where(j < nj, j, nj + j))),
                  pl.BlockSpec((D, tn), lambda i, j: (0, jnp.minimum(nj + j, 2 * nj - 1))),
                  whole((n_slabs, CONV_K, LANES)), whole((n_slabs, 1, LANES)), whole((1, LANES)),
                  whole((n_slabs, 1, LANES)), whole((n_slabs, 1, LANES))],
        out_specs=pl.BlockSpec((tm, tn), lambda i, j: (i, jnp.maximum(j - nj, 0))),
        scratch_shapes=[pltpu.VMEM((tm, D), BF16)] + [pltpu.VMEM((tm, MXU_N), F32)] * 4
        + [pltpu.VMEM((2 * MXU_N // LANES, tm + HALO, LANES), F32),
           pltpu.VMEM((n_slabs, HALO, LANES), F32),
           pltpu.VMEM((n_slabs, tm, LANES), BF16),
           pltpu.VMEM((tm, LANES), F32), pltpu.VMEM((tm, LANES), F32)],
        compiler_params=_params(2),
        name="conv_mix",
    )(*[x] * X_CHUNKS, scale, shift, norm_g, w_bf16, w_bf16, w_slabs, slabs(dw_b),
      jnp.full((1, LANES), jnp.mean(dw_b), F32), slabs(ln_g), slabs(ln_b))


def _rotate_half_matrix():
    half = ROPE_DIM // 2
    k = jnp.arange(HEAD_DIM)[:, None]
    j = jnp.arange(HEAD_DIM)[None, :]
    p = (jnp.where((j < half) & (k == j + half), -1.0, 0.0)
         + jnp.where((j >= half) & (j < ROPE_DIM) & (k == j - half), 1.0, 0.0))
    return jnp.kron(jnp.eye(MXU_N // HEAD_DIM, dtype=F32), p).astype(BF16)


def _rope_tables(positions):
    B, S = positions.shape
    inv_freq = jnp.power(ROPE_THETA, -jnp.arange(0, ROPE_DIM, 2, dtype=F32) / ROPE_DIM)
    ang = positions.astype(F32)[..., None] * inv_freq
    cos, sin = lax.optimization_barrier((jnp.cos(ang), jnp.sin(ang)))
    rest = HEAD_DIM - ROPE_DIM
    cos_t = jnp.concatenate([cos, cos, jnp.ones((B, S, rest), F32)], axis=-1)
    sin_t = jnp.concatenate([sin, sin, jnp.zeros((B, S, rest), F32)], axis=-1)
    return [tuple(t.reshape(B, S // d, d, HEAD_DIM).transpose(0, 2, 1, 3) for t in (cos_t, sin_t))
            for _, d in DIL_GROUPS]


def _rope_gains(q_gain, k_gain):
    half = ROPE_DIM // 2
    gains = jnp.stack([q_gain, k_gain])
    lane = jnp.arange(HEAD_DIM)
    partner = jnp.where(lane < half, jnp.roll(gains, -half, axis=-1), jnp.roll(gains, half, axis=-1))
    return jnp.stack([gains, partner], axis=1)


def kernel(x, c, positions, ada_w, ada_b, norm_g, attn_w_in, attn_q_gain, attn_k_gain, attn_w_out,
           sgu_w_in, sgu_ln_g, sgu_ln_b, sgu_ws, sgu_bs, sgu_w_out,
           conv_w_in, conv_dw_w, conv_dw_b, conv_ln_g, conv_ln_b, conv_w_out):
    B, S, D = x.shape
    mod = _ada_modulation(c, ada_w, ada_b)
    tables = _rope_tables(positions)
    for i in range(DEPTH):
        shift = mod[i, :, None, :D]
        scale = mod[i, :, None, D:2 * D]
        gate = mod[i, :, None, 2 * D:]
        g_row = norm_g[i].reshape(1, D)
        kind, l = i % N_MIXERS, i // N_MIXERS
        if kind == 0:
            gains = _rope_gains(attn_q_gain[l] * (LOG2_E * HEAD_DIM ** -0.5), attn_k_gain[l])
            proj = [_attn_proj(x, scale, shift, g_row, attn_w_in, l, gains, tables[g], group=g,
                               with_gate=(g == 0)) for g in range(N_DIL)]
            y = _attention(*proj).reshape(B * S, ATTN_OUT)
            x = _out_proj(y, attn_w_out[l].astype(BF16), x, gate)
        elif kind == 1:
            proj = _sgu_proj(x, scale, shift, g_row, sgu_w_in, l)
            y = _sgu_core(proj, sgu_ln_g[l], sgu_ln_b[l], sgu_ws[l], sgu_bs[l])
            x = _out_proj(y, sgu_w_out[l].astype(BF16), x, gate)
        else:
            y = _conv_mix(x, scale, shift, g_row, conv_w_in[l].astype(BF16), conv_dw_w[l], conv_dw_b[l],
                          conv_ln_g[l], conv_ln_b[l])
            x = _out_proj(y, conv_w_out[l].astype(BF16), x, gate)
    return x
```
